```python
import math
import jax, jax.numpy as jnp
from jax import lax
import numpy as np

D_MODEL = 2048
BATCH = 4
SEQ = 2048
DEPTH = 1

SSM_EXPAND = 2
SSM_D_INNER = SSM_EXPAND * D_MODEL
SSM_HEAD_DIM = 64
SSM_HEADS = SSM_D_INNER // SSM_HEAD_DIM
SSM_GROUPS = 8
SSM_HEADS_PER_GROUP = SSM_HEADS // SSM_GROUPS
SSM_STATE = 128
SSM_CONV = 4
SSM_CHUNK = 256
SSM_CONV_CH = SSM_D_INNER + 2 * SSM_GROUPS * SSM_STATE

ATT_HEAD_DIM = 64
ATT_HEADS = D_MODEL // (2 * ATT_HEAD_DIM)
ATT_QK = 2 * ATT_HEADS * ATT_HEAD_DIM
ATT_V = ATT_HEADS * 2 * ATT_HEAD_DIM
Q_BLOCK = 128
ROPE_THETA = 10000.0

FFN_HIDDEN = (((8 * D_MODEL + 2) // 3 + 255) // 256) * 256

IN_SIZES = (SSM_D_INNER, SSM_CONV_CH, SSM_HEADS, ATT_QK, ATT_QK, ATT_V, D_MODEL, D_MODEL)
IN_TOTAL = sum(IN_SIZES)

NORM_EPS = 1e-6
SUBLN_EPS = 1e-5

kernel_name = "hybrid_gated_ssd_diffattn_swiglu"


def _in_splits():
    out, acc = [], 0
    for n in IN_SIZES[:-1]:
        acc += n
        out.append(acc)
    return out


def rms_norm(x, w, eps):
    xf = x.astype(jnp.float32)
    y = xf * lax.rsqrt(jnp.mean(xf * xf, axis=-1, keepdims=True) + eps)
    return (y * w.astype(jnp.float32)).astype(x.dtype)


def rotary(x, positions):
    half = x.shape[-1] // 2
    inv_freq = ROPE_THETA ** (-jnp.arange(half, dtype=jnp.float32) / half)
    ang = positions.astype(jnp.float32)[:, None] * inv_freq[None, :]
    cos = jnp.cos(ang)[None, :, None, :]
    sin = jnp.sin(ang)[None, :, None, :]
    xf = x.astype(jnp.float32)
    x1, x2 = xf[..., :half], xf[..., half:]
    return jnp.concatenate([x1 * cos - x2 * sin, x2 * cos + x1 * sin], axis=-1).astype(x.dtype)


def causal_depthwise_conv(u, w, b):
    k = w.shape[0]
    out = lax.conv_general_dilated(
        u, w[:, None, :].astype(u.dtype), window_strides=(1,), padding=[(k - 1, 0)],
        dimension_numbers=('NWC', 'WIO', 'NWC'), feature_group_count=u.shape[-1])
    return out + b.astype(u.dtype)


def ssd_chunked(xs, dt, A, Bm, Cm):
    b, s, g, r, p = xs.shape
    n = Bm.shape[-1]
    l = SSM_CHUNK
    pad = (-s) % l
    xdt = xs.astype(jnp.float32) * dt[..., None]
    dA = dt * A
    Bf = Bm.astype(jnp.float32)
    Cf = Cm.astype(jnp.float32)
    if pad:
        xdt = jnp.pad(xdt, ((0, 0), (0, pad), (0, 0), (0, 0), (0, 0)))
        dA = jnp.pad(dA, ((0, 0), (0, pad), (0, 0), (0, 0)))
        Bf = jnp.pad(Bf, ((0, 0), (0, pad), (0, 0), (0, 0)))
        Cf = jnp.pad(Cf, ((0, 0), (0, pad), (0, 0), (0, 0)))
    c = (s + pad) // l
    xdt = xdt.reshape(b, c, l, g, r, p)
    Bc = Bf.reshape(b, c, l, g, n)
    Cc = Cf.reshape(b, c, l, g, n)
    a_cs = jnp.cumsum(jnp.transpose(dA.reshape(b, c, l, g, r), (0, 3, 4, 1, 2)), axis=-1)

    causal = jnp.tril(jnp.ones((l, l), dtype=bool))
    decay_in = jnp.exp(jnp.where(causal, a_cs[..., :, None] - a_cs[..., None, :], -jnp.inf))
    cb = jnp.einsum('bclgn,bcsgn->bgcls', Cc, Bc)
    y_diag = jnp.einsum('bgrcls,bcsgrp->bclgrp', cb[:, :, None] * decay_in, xdt)

    decay_to_end = jnp.transpose(jnp.exp(a_cs[..., -1:] - a_cs), (0, 3, 4, 1, 2))
    states = jnp.einsum('bclgn,bclgrp->bcgrpn', Bc, xdt * decay_to_end[..., None])

    a_tot = jnp.pad(a_cs[..., -1], ((0, 0), (0, 0), (0, 0), (1, 0)))
    t_cs = jnp.cumsum(a_tot, axis=-1)
    chunk_causal = jnp.tril(jnp.ones((c + 1, c + 1), dtype=bool))
    decay_chunk = jnp.exp(jnp.where(chunk_causal, t_cs[..., :, None] - t_cs[..., None, :], -jnp.inf))
    states = jnp.concatenate([jnp.zeros_like(states[:, :1]), states], axis=1)
    states_in = jnp.einsum('bgrzc,bcgrpn->bzgrpn', decay_chunk, states)[:, :-1]

    decay_from_start = jnp.transpose(jnp.exp(a_cs), (0, 3, 4, 1, 2))
    y_off = jnp.einsum('bclgn,bcgrpn->bclgrp', Cc, states_in) * decay_from_start[..., None]

    y = (y_diag + y_off).reshape(b, c * l, g, r, p)
    return y[:, :s]


def mamba2_branch(z, xbc, dt_raw, conv_w, conv_b, dt_bias, a_log, d_skip, norm_w):
    b, s, _ = z.shape
    G, R, P, N = SSM_GROUPS, SSM_HEADS_PER_GROUP, SSM_HEAD_DIM, SSM_STATE
    xbc = jax.nn.silu(causal_depthwise_conv(xbc, conv_w, conv_b))
    xs, Bm, Cm = jnp.split(xbc, [SSM_D_INNER, SSM_D_INNER + G * N], axis=-1)
    xs = xs.reshape(b, s, G, R, P)
    Bm = Bm.reshape(b, s, G, N)
    Cm = Cm.reshape(b, s, G, N)
    dt = jax.nn.softplus(dt_raw.astype(jnp.float32) + dt_bias.astype(jnp.float32)).reshape(b, s, G, R)
    A = -jnp.exp(a_log.astype(jnp.float32)).reshape(G, R)
    y = ssd_chunked(xs, dt, A, Bm, Cm)
    y = y + d_skip.astype(jnp.float32).reshape(G, R)[..., None] * xs.astype(jnp.float32)
    y = y.reshape(b, s, SSM_D_INNER) * jax.nn.silu(z.astype(jnp.float32))
    y = y.reshape(b, s, G, SSM_D_INNER // G)
    y = y * lax.rsqrt(jnp.mean(y * y, axis=-1, keepdims=True) + SUBLN_EPS)
    y = y.reshape(b, s, SSM_D_INNER) * norm_w.astype(jnp.float32)
    return y.astype(z.dtype)


def diff_attention(q, k, v, q_norm_w, k_norm_w, lq1, lk1, lq2, lk2, subln_w, lam_init):
    b, s, _ = q.shape
    H, Dh = ATT_HEADS, ATT_HEAD_DIM
    q = q.reshape(b, s, 2 * H, Dh)
    k = k.reshape(b, s, 2 * H, Dh)
    v = v.reshape(b, s, H, 2 * Dh).astype(jnp.float32)
    pos = jnp.arange(s, dtype=jnp.int32)
    q = rotary(rms_norm(q, q_norm_w, NORM_EPS), pos).astype(jnp.float32)
    k = rotary(rms_norm(k, k_norm_w, NORM_EPS), pos).astype(jnp.float32)
    f = lambda t: t.astype(jnp.float32)
    lam = jnp.exp(jnp.sum(f(lq1) * f(lk1))) - jnp.exp(jnp.sum(f(lq2) * f(lk2))) + lam_init
    scale = Dh ** -0.5
    outs = []
    for i in range(s // Q_BLOCK):
        q0 = i * Q_BLOCK
        kv = q0 + Q_BLOCK
        sc = jnp.einsum('bqhd,bkhd->bhqk', q[:, q0:kv], k[:, :kv]) * scale
        mask = jnp.arange(kv)[None, :] <= (q0 + jnp.arange(Q_BLOCK))[:, None]
        sc = jnp.where(mask, sc, -jnp.inf)
        pr = jax.nn.softmax(sc, axis=-1).reshape(b, H, 2, Q_BLOCK, kv)
        wts = pr[:, :, 0] - lam * pr[:, :, 1]
        outs.append(jnp.einsum('bhqk,bkhe->bqhe', wts, v[:, :kv]))
    o = jnp.concatenate(outs, axis=1)
    o = rms_norm(o, subln_w, SUBLN_EPS) * (1.0 - lam_init)
    return o.reshape(b, s, ATT_V).astype(q.dtype)


def setup_inputs(seed: int = 0) -> dict:
    key = jax.random.key(seed)
    ks = jax.random.split(key, 24)
    f32 = jnp.float32
    L = DEPTH

    def normal(k, shape, scale):
        return jax.random.normal(k, shape, f32) * scale

    dt0 = jnp.exp(jax.random.uniform(ks[5], (L, SSM_HEADS), f32, math.log(1e-3), math.log(1e-1)))
    return {
        'x': normal(ks[0], (BATCH, SEQ, D_MODEL), 1.0),
        'mix_norm_w': 1.0 + normal(ks[1], (L, D_MODEL), 0.02),
        'w_in': normal(ks[2], (L, D_MODEL, IN_TOTAL), D_MODEL ** -0.5),
        'conv_w': normal(ks[3], (L, SSM_CONV, SSM_CONV_CH), SSM_CONV ** -0.5),
        'conv_b': normal(ks[4], (L, SSM_CONV_CH), 0.02),
        'dt_bias': dt0 + jnp.log(-jnp.expm1(-dt0)),
        'a_log': jnp.log(jax.random.uniform(ks[6], (L, SSM_HEADS), f32, 1.0, 16.0)),
        'd_skip': 1.0 + normal(ks[7], (L, SSM_HEADS), 0.02),
        'ssm_norm_w': 1.0 + normal(ks[8], (L, SSM_D_INNER), 0.02),
        'w_ssm_proj': normal(ks[9], (L, SSM_D_INNER, D_MODEL), SSM_D_INNER ** -0.5),
        'q_norm_w': 1.0 + normal(ks[10], (L, ATT_HEAD_DIM), 0.02),
        'k_norm_w': 1.0 + normal(ks[11], (L, ATT_HEAD_DIM), 0.02),
        'lambda_q1': normal(ks[12], (L, ATT_HEAD_DIM), 0.1),
        'lambda_k1': normal(ks[13], (L, ATT_HEAD_DIM), 0.1),
        'lambda_q2': normal(ks[14], (L, ATT_HEAD_DIM), 0.1),
        'lambda_k2': normal(ks[15], (L, ATT_HEAD_DIM), 0.1),
        'subln_w': 1.0 + normal(ks[16], (L, 2 * ATT_HEAD_DIM), 0.02),
        'w_attn_proj': normal(ks[17], (L, ATT_V, D_MODEL), ATT_V ** -0.5),
        'w_out': normal(ks[18], (L, D_MODEL, D_MODEL), D_MODEL ** -0.5),
        'ffn_norm_w': 1.0 + normal(ks[19], (L, D_MODEL), 0.02),
        'w_ffn_gate': normal(ks[20], (L, D_MODEL, FFN_HIDDEN), D_MODEL ** -0.5),
        'w_ffn_up': normal(ks[21], (L, D_MODEL, FFN_HIDDEN), D_MODEL ** -0.5),
        'w_ffn_down': normal(ks[22], (L, FFN_HIDDEN, D_MODEL), FFN_HIDDEN ** -0.5),
    }


def reference(x, mix_norm_w, w_in, conv_w, conv_b, dt_bias, a_log, d_skip, ssm_norm_w, w_ssm_proj,
              q_norm_w, k_norm_w, lambda_q1, lambda_k1, lambda_q2, lambda_k2, subln_w, w_attn_proj,
              w_out, ffn_norm_w, w_ffn_gate, w_ffn_up, w_ffn_down):
    splits = _in_splits()
    for layer in range(DEPTH):
        lam_init = 0.8 - 0.6 * math.exp(-0.3 * layer)
        h = rms_norm(x, mix_norm_w[layer], NORM_EPS)
        proj = jnp.einsum('bsd,dn->bsn', h, w_in[layer])
        z, xbc, dt_raw, q, k, v, g_ssm, g_att = jnp.split(proj, splits, axis=-1)
        y_ssm = mamba2_branch(z, xbc, dt_raw, conv_w[layer], conv_b[layer], dt_bias[layer],
                              a_log[layer], d_skip[layer], ssm_norm_w[layer])
        y_att = diff_attention(q, k, v, q_norm_w[layer], k_norm_w[layer], lambda_q1[layer],
                               lambda_k1[layer], lambda_q2[layer], lambda_k2[layer],
                               subln_w[layer], lam_init)
        b_ssm = jnp.einsum('bse,ed->bsd', y_ssm, w_ssm_proj[layer])
        b_att = jnp.einsum('bse,ed->bsd', y_att, w_attn_proj[layer])
        merged = jax.nn.sigmoid(g_ssm) * b_ssm + jax.nn.sigmoid(g_att) * b_att
        x = x + jnp.einsum('bsd,de->bse', merged, w_out[layer]).astype(x.dtype)
        h = rms_norm(x, ffn_norm_w[layer], NORM_EPS)
        a = jax.nn.silu(jnp.einsum('bsd,df->bsf', h, w_ffn_gate[layer])) * jnp.einsum('bsd,df->bsf', h, w_ffn_up[layer])
        x = x + jnp.einsum('bsf,fd->bsd', a, w_ffn_down[layer]).astype(x.dtype)
    return x
```

```python
import functools
import math

import jax
import jax.numpy as jnp
from jax import lax
from jax.experimental import pallas as pl
from jax.experimental.pallas import tpu as pltpu

F32 = jnp.float32
BF16 = jnp.bfloat16

D_MODEL = 2048
SSM_D_INNER = 2 * D_MODEL
SSM_HEAD_DIM = 64
SSM_HEADS = SSM_D_INNER // SSM_HEAD_DIM
SSM_GROUPS = 8
SSM_HEADS_PER_GROUP = SSM_HEADS // SSM_GROUPS
SSM_GROUP_WIDTH = SSM_D_INNER // SSM_GROUPS
SSM_STATE = 128
SSM_CONV = 4
SSM_CHUNK = 256
SSM_CONV_CH = SSM_D_INNER + 2 * SSM_GROUPS * SSM_STATE
ATT_HEAD_DIM = 64
ATT_HEADS = D_MODEL // (2 * ATT_HEAD_DIM)
ATT_PAIR = 2 * ATT_HEAD_DIM
ATT_QK = 2 * ATT_HEADS * ATT_HEAD_DIM
ATT_V = ATT_HEADS * 2 * ATT_HEAD_DIM
ROPE_THETA = 10000.0
FFN_HIDDEN = (((8 * D_MODEL + 2) // 3 + 255) // 256) * 256
NORM_EPS = 1e-6
SUBLN_EPS = 1e-5
LAM_INIT = 0.8 - 0.6 * math.exp(-0.3 * 0)

COL_Z = 0
COL_X = COL_Z + SSM_D_INNER
COL_B = COL_X + SSM_D_INNER
COL_C = COL_B + SSM_GROUPS * SSM_STATE
COL_Q = COL_C + SSM_GROUPS * SSM_STATE
COL_K = COL_Q + ATT_QK
COL_V = COL_K + ATT_QK
COL_GS = COL_V + ATT_V
COL_GA = COL_GS + D_MODEL
PROJ_COLS = COL_GA + D_MODEL
LANE = 128
DT_PAD = LANE

VMEM_LIMIT = 56 * 1024 * 1024


def _cparams(sem):
    return pltpu.CompilerParams(dimension_semantics=sem, vmem_limit_bytes=VMEM_LIMIT)


INPROJ_TM = 1024
INPROJ_TN = 1024
NORM_ROWS = 256


def _rms_rows(x, w, eps):
    ms = jnp.mean(x * x, axis=-1, keepdims=True)
    return x * lax.rsqrt(ms + eps) * w


def _inproj_kernel(x_ref, nw_ref, w_ref, wdt_ref, out_ref, dt_ref, h_ref):
    @pl.when(pl.program_id(1) == 0)
    def _():
        def body(i, carry):
            rows = pl.ds(pl.multiple_of(i * NORM_ROWS, NORM_ROWS), NORM_ROWS)
            h_ref[rows, :] = _rms_rows(x_ref[rows, :], nw_ref[...], NORM_EPS).astype(BF16)
            return carry
        lax.fori_loop(0, INPROJ_TM // NORM_ROWS, body, 0)
        dt_ref[...] = jnp.dot(h_ref[...], wdt_ref[...], preferred_element_type=F32)

    out_ref[...] = jnp.dot(h_ref[...], w_ref[...], preferred_element_type=F32).astype(out_ref.dtype)


def _inproj(x2d, norm_w, w_main, w_dt):
    m = x2d.shape[0]
    return pl.pallas_call(
        _inproj_kernel,
        out_shape=(jax.ShapeDtypeStruct((m, PROJ_COLS), BF16), jax.ShapeDtypeStruct((m, DT_PAD), F32)),
        grid=(m // INPROJ_TM, PROJ_COLS // INPROJ_TN),
        in_specs=[
            pl.BlockSpec((INPROJ_TM, D_MODEL), lambda i, j: (i, 0)),
            pl.BlockSpec((1, D_MODEL), lambda i, j: (0, 0)),
            pl.BlockSpec((D_MODEL, INPROJ_TN), lambda i, j: (0, j)),
            pl.BlockSpec((D_MODEL, DT_PAD), lambda i, j: (0, 0)),
        ],
        out_specs=(
            pl.BlockSpec((INPROJ_TM, INPROJ_TN), lambda i, j: (i, j)),
            pl.BlockSpec((INPROJ_TM, DT_PAD), lambda i, j: (i, 0)),
        ),
        scratch_shapes=[pltpu.VMEM((INPROJ_TM, D_MODEL), BF16)],
        compiler_params=_cparams(("parallel", "arbitrary")),
        name="inproj",
    )(x2d, norm_w, w_main, w_dt)


L = SSM_CHUNK
HPG = SSM_HEADS_PER_GROUP
GW = SSM_GROUP_WIDTH
TAIL = 8


def _conv_silu(in_ref, tail_ref, w_ref, b_ref, ext_ref, width):
    cur = in_ref[...].astype(F32)
    ext_ref[0:TAIL, 0:width] = tail_ref[...]
    ext_ref[TAIL:TAIL + L, 0:width] = cur
    w = w_ref[...]
    u = b_ref[...] + w[3:4, :] * cur
    for k in range(SSM_CONV - 1):
        off = TAIL - (SSM_CONV - 1) + k
        u = u + w[k:k + 1, :] * ext_ref[off:off + L, 0:width]
    tail_ref[...] = cur[L - TAIL:L, :]
    return u * jax.nn.sigmoid(u)


def _ssd_kernel(x_ref, b_ref, c_ref, z_ref, dtc_ref, dtr_ref,
                cwx_ref, cwb_ref, cwc_ref, cbx_ref, cbb_ref, cbc_ref,
                biasc_ref, alogc_ref, dskipc_ref, biasr_ref, alogr_ref, nw_ref,
                y_ref,
                state_ref, tailx_ref, tailb_ref, tailc_ref, ext_ref, xs_ref, xde_ref, ycat_ref):
    @pl.when(pl.program_id(2) == 0)
    def _():
        state_ref[...] = jnp.zeros_like(state_ref)
        tailx_ref[...] = jnp.zeros_like(tailx_ref)
        tailb_ref[...] = jnp.zeros_like(tailb_ref)
        tailc_ref[...] = jnp.zeros_like(tailc_ref)

    xs_ref[...] = _conv_silu(x_ref, tailx_ref, cwx_ref, cbx_ref, ext_ref, GW)
    bm = _conv_silu(b_ref, tailb_ref, cwb_ref, cbb_ref, ext_ref, SSM_STATE)
    cm = _conv_silu(c_ref, tailc_ref, cwc_ref, cbc_ref, ext_ref, SSM_STATE)
    bm_bf = bm.astype(BF16)
    cm_bf = cm.astype(BF16)
    bt_bf = bm.T.astype(BF16)

    dtc = jax.nn.softplus(dtc_ref[...] + biasc_ref[...])
    dtr = jax.nn.softplus(dtr_ref[...] + biasr_ref[...])
    dac = dtc * (-jnp.exp(alogc_ref[...]))
    dar = dtr * (-jnp.exp(alogr_ref[...]))
    row = lax.broadcasted_iota(jnp.int32, (L, L), 0)
    col = lax.broadcasted_iota(jnp.int32, (L, L), 1)
    causal = row >= col
    tri = causal.astype(F32)
    tri_t = (row <= col).astype(F32)
    acs_c = jnp.dot(tri, dac, preferred_element_type=F32, precision=lax.Precision.HIGHEST)
    acs_r = jnp.dot(dar, tri_t, preferred_element_type=F32, precision=lax.Precision.HIGHEST)
    a_last = acs_c[L - 1:L, :]
    dec_start = jnp.exp(acs_c)
    dec_end = jnp.exp(a_last - acs_c)
    dec_chunk = jnp.exp(a_last)
    dskip = dskipc_ref[...]

    cb = lax.dot_general(cm_bf, bm_bf, (((1,), (1,)), ((), ())), preferred_element_type=F32)
    y_off = jnp.dot(cm_bf, state_ref[...].astype(BF16), preferred_element_type=F32)

    lane = lax.broadcasted_iota(jnp.int32, (1, LANE), 1)
    first = lane < SSM_HEAD_DIM

    def pair(vals, p):
        return jnp.where(first, vals[:, 2 * p:2 * p + 1], vals[:, 2 * p + 1:2 * p + 2])

    for p in range(HPG // 2):
        lanes = slice(p * LANE, (p + 1) * LANE)
        xs_p = xs_ref[:, lanes]
        xdt_p = xs_p * pair(dtc, p)
        xdt_bf = xdt_p.astype(BF16)
        xde_ref[:, lanes] = (xdt_p * pair(dec_end, p)).astype(BF16)
        y_heads = []
        for r in (2 * p, 2 * p + 1):
            dec = jnp.exp(jnp.where(causal, acs_c[:, r:r + 1] - acs_r[r:r + 1, :], -jnp.inf))
            y_heads.append(jnp.dot((cb * dec).astype(BF16), xdt_bf, preferred_element_type=F32))
        y_diag = jnp.where(first, y_heads[0], y_heads[1])
        ycat_ref[:, lanes] = y_diag + y_off[:, lanes] * pair(dec_start, p) + pair(dskip, p) * xs_p

    upd = jnp.dot(bt_bf, xde_ref[...], preferred_element_type=F32)
    for p in range(HPG // 2):
        lanes = slice(p * LANE, (p + 1) * LANE)
        state_ref[:, lanes] = state_ref[:, lanes] * pair(dec_chunk, p) + upd[:, lanes]

    z = z_ref[...].astype(F32)
    y = ycat_ref[...] * (z * jax.nn.sigmoid(z))
    y = y * lax.rsqrt(jnp.mean(y * y, axis=-1, keepdims=True) + SUBLN_EPS)
    y_ref[...] = (y * nw_ref[...]).astype(y_ref.dtype)


def _ssd(proj3, dt_col, dt_row, conv_w, conv_b, bias_c, alog_c, dskip_c, bias_r, alog_r, norm_w):
    b, s, _ = proj3.shape
    nc = s // L
    xblk = COL_X // GW
    bblk = COL_B // SSM_STATE
    cblk = COL_C // SSM_STATE
    cw_b = SSM_D_INNER // SSM_STATE
    cw_c = cw_b + SSM_GROUPS
    in_specs = [
        pl.BlockSpec((None, L, GW), lambda bi, g, c: (bi, c, xblk + g)),
        pl.BlockSpec((None, L, SSM_STATE), lambda bi, g, c: (bi, c, bblk + g)),
        pl.BlockSpec((None, L, SSM_STATE), lambda bi, g, c: (bi, c, cblk + g)),
        pl.BlockSpec((None, L, GW), lambda bi, g, c: (bi, c, g)),
        pl.BlockSpec((None, None, L, HPG), lambda bi, g, c: (bi, g, c, 0)),
        pl.BlockSpec((None, None, HPG, L), lambda bi, g, c: (bi, g, 0, c)),
        pl.BlockSpec((SSM_CONV, GW), lambda bi, g, c: (0, g)),
        pl.BlockSpec((SSM_CONV, SSM_STATE), lambda bi, g, c: (0, cw_b + g)),
        pl.BlockSpec((SSM_CONV, SSM_STATE), lambda bi, g, c: (0, cw_c + g)),
        pl.BlockSpec((1, GW), lambda bi, g, c: (0, g)),
        pl.BlockSpec((1, SSM_STATE), lambda bi, g, c: (0, cw_b + g)),
        pl.BlockSpec((1, SSM_STATE), lambda bi, g, c: (0, cw_c + g)),
        pl.BlockSpec((None, 1, HPG), lambda bi, g, c: (g, 0, 0)),
        pl.BlockSpec((None, 1, HPG), lambda bi, g, c: (g, 0, 0)),
        pl.BlockSpec((None, 1, HPG), lambda bi, g, c: (g, 0, 0)),
        pl.BlockSpec((None, HPG, 1), lambda bi, g, c: (g, 0, 0)),
        pl.BlockSpec((None, HPG, 1), lambda bi, g, c: (g, 0, 0)),
        pl.BlockSpec((1, GW), lambda bi, g, c: (0, g)),
    ]
    return pl.pallas_call(
        _ssd_kernel,
        out_shape=jax.ShapeDtypeStruct((b, s, SSM_D_INNER), BF16),
        grid=(b, SSM_GROUPS, nc),
        in_specs=in_specs,
        out_specs=pl.BlockSpec((None, L, GW), lambda bi, g, c: (bi, c, g)),
        scratch_shapes=[
            pltpu.VMEM((SSM_STATE, GW), F32),
            pltpu.VMEM((TAIL, GW), F32),
            pltpu.VMEM((TAIL, SSM_STATE), F32),
            pltpu.VMEM((TAIL, SSM_STATE), F32),
            pltpu.VMEM((TAIL + L, GW), F32),
            pltpu.VMEM((L, GW), F32),
            pltpu.VMEM((L, GW), BF16),
            pltpu.VMEM((L, GW), F32),
        ],
        compiler_params=_cparams(("parallel", "parallel", "arbitrary")),
        name="ssd",
    )(proj3, proj3, proj3, proj3, dt_col, dt_row, conv_w, conv_w, conv_w, conv_b, conv_b, conv_b,
      bias_c, alog_c, dskip_c, bias_r, alog_r, norm_w)


ATT_TQ = 256
ROPE_ROWS = 256


def _norm_rope(x, w, cos, sin_signed, group_ones):
    ss = jnp.dot((x * x).astype(BF16), group_ones, preferred_element_type=F32)
    xn = x * lax.rsqrt(ss * (1.0 / ATT_HEAD_DIM) + NORM_EPS) * w
    lane = lax.broadcasted_iota(jnp.int32, (1, ATT_PAIR), 1)
    low_half = (lane % ATT_HEAD_DIM) < (ATT_HEAD_DIM // 2)
    half = ATT_HEAD_DIM // 2
    rot = jnp.where(low_half, pltpu.roll(xn, ATT_PAIR - half, 1), pltpu.roll(xn, half, 1))
    return xn * cos + rot * sin_signed


def _attn_kernel(q_ref, k_ref, v_ref, cos_ref, sin_ref, qw_ref, kw_ref,
                 lq1_ref, lk1_ref, lq2_ref, lk2_ref, sw_ref, o_ref, krot_ref):
    qi = pl.program_id(2)
    ri = lax.broadcasted_iota(jnp.int32, (ATT_PAIR, ATT_PAIR), 0) // ATT_HEAD_DIM
    ci = lax.broadcasted_iota(jnp.int32, (ATT_PAIR, ATT_PAIR), 1) // ATT_HEAD_DIM
    group_ones = (ri == ci).astype(BF16)

    @pl.when(qi == 0)
    def _():
        def body(i, carry):
            rows = pl.ds(pl.multiple_of(i * ROPE_ROWS, ROPE_ROWS), ROPE_ROWS)
            kr = _norm_rope(k_ref[rows, :].astype(F32), kw_ref[...], cos_ref[rows, :], sin_ref[rows, :], group_ones)
            krot_ref[rows, :] = kr.astype(BF16)
            return carry
        lax.fori_loop(0, k_ref.shape[0] // ROPE_ROWS, body, 0)

    qrows = pl.ds(pl.multiple_of(qi * ATT_TQ, ATT_TQ), ATT_TQ)
    q = _norm_rope(q_ref[...].astype(F32), qw_ref[...], cos_ref[qrows, :], sin_ref[qrows, :], group_ones)
    q = q * (ATT_HEAD_DIM ** -0.5)
    lane = lax.broadcasted_iota(jnp.int32, (1, ATT_PAIR), 1)
    q1 = jnp.where(lane < ATT_HEAD_DIM, q, 0.0).astype(BF16)
    q2 = jnp.where(lane >= ATT_HEAD_DIM, q, 0.0).astype(BF16)

    def scores(qh, kj):
        return lax.dot_general(qh, kj, (((1,), (1,)), ((), ())), preferred_element_type=F32)

    def update(s, vj, m, l, acc):
        m_new = jnp.maximum(m, jnp.max(s, axis=-1, keepdims=True))
        alpha = jnp.exp(m - m_new)
        p = jnp.exp(s - m_new)
        l_new = alpha * l + jnp.sum(p, axis=-1, keepdims=True)
        acc_new = alpha * acc + jnp.dot(p.astype(BF16), vj, preferred_element_type=F32)
        return m_new, l_new, acc_new

    def kv_step(j, carry, masked):
        m1, l1, a1, m2, l2, a2 = carry
        rows = pl.ds(pl.multiple_of(j * ATT_TQ, ATT_TQ), ATT_TQ)
        kj = krot_ref[rows, :]
        vj = v_ref[rows, :]
        s1 = scores(q1, kj)
        s2 = scores(q2, kj)
        if masked:
            r = lax.broadcasted_iota(jnp.int32, (ATT_TQ, ATT_TQ), 0)
            c = lax.broadcasted_iota(jnp.int32, (ATT_TQ, ATT_TQ), 1)
            s1 = jnp.where(r >= c, s1, -jnp.inf)
            s2 = jnp.where(r >= c, s2, -jnp.inf)
        m1, l1, a1 = update(s1, vj, m1, l1, a1)
        m2, l2, a2 = update(s2, vj, m2, l2, a2)
        return m1, l1, a1, m2, l2, a2

    neg = jnp.full((ATT_TQ, 1), -jnp.inf, F32)
    zero = jnp.zeros((ATT_TQ, 1), F32)
    zacc = jnp.zeros((ATT_TQ, ATT_PAIR), F32)
    carry = kv_step(qi, (neg, zero, zacc, neg, zero, zacc), True)
    carry = lax.fori_loop(0, qi, lambda j, cr: kv_step(j, cr, False), carry)
    m1, l1, a1, m2, l2, a2 = carry

    lam = (jnp.exp(jnp.sum(lq1_ref[...] * lk1_ref[...], axis=-1, keepdims=True))
           - jnp.exp(jnp.sum(lq2_ref[...] * lk2_ref[...], axis=-1, keepdims=True)) + LAM_INIT)
    o = a1 / l1 - lam * (a2 / l2)
    o = o * lax.rsqrt(jnp.mean(o * o, axis=-1, keepdims=True) + SUBLN_EPS) * sw_ref[...]
    o_ref[...] = (o * (1.0 - LAM_INIT)).astype(o_ref.dtype)


def _attn(proj3, cos_t, sin_t, qw, kw, lq1, lk1, lq2, lk2, sw):
    b, s, _ = proj3.shape
    qblk = COL_Q // ATT_PAIR
    kblk = COL_K // ATT_PAIR
    vblk = COL_V // ATT_PAIR
    small = lambda width: pl.BlockSpec((1, width), lambda bi, h, qi: (0, 0))
    return pl.pallas_call(
        _attn_kernel,
        out_shape=jax.ShapeDtypeStruct((b, s, ATT_V), BF16),
        grid=(b, ATT_HEADS, s // ATT_TQ),
        in_specs=[
            pl.BlockSpec((None, ATT_TQ, ATT_PAIR), lambda bi, h, qi: (bi, qi, qblk + h)),
            pl.BlockSpec((None, s, ATT_PAIR), lambda bi, h, qi: (bi, 0, kblk + h)),
            pl.BlockSpec((None, s, ATT_PAIR), lambda bi, h, qi: (bi, 0, vblk + h)),
            pl.BlockSpec((s, ATT_PAIR), lambda bi, h, qi: (0, 0)),
            pl.BlockSpec((s, ATT_PAIR), lambda bi, h, qi: (0, 0)),
            small(ATT_PAIR), small(ATT_PAIR),
            small(ATT_HEAD_DIM), small(ATT_HEAD_DIM), small(ATT_HEAD_DIM), small(ATT_HEAD_DIM),
            small(ATT_PAIR),
        ],
        out_specs=pl.BlockSpec((None, ATT_TQ, ATT_PAIR), lambda bi, h, qi: (bi, qi, h)),
        scratch_shapes=[pltpu.VMEM((s, ATT_PAIR), BF16)],
        compiler_params=_cparams(("parallel", "parallel", "arbitrary")),
        name="attn",
    )(proj3, proj3, proj3, cos_t, sin_t, qw, kw, lq1, lk1, lq2, lk2, sw)


MERGE_TM = 512
MERGE_TN = 512


def _merge_kernel(ys_ref, ws_ref, ya_ref, wa_ref, gs_ref, ga_ref, o_ref):
    bs = jnp.dot(ys_ref[...], ws_ref[...], preferred_element_type=F32)
    ba = jnp.dot(ya_ref[...], wa_ref[...], preferred_element_type=F32)
    gs = jax.nn.sigmoid(gs_ref[...].astype(F32))
    ga = jax.nn.sigmoid(ga_ref[...].astype(F32))
    o_ref[...] = (gs * bs + ga * ba).astype(o_ref.dtype)


def _merge(y_ssm, w_ssm, y_att, w_att, proj):
    m = y_ssm.shape[0]
    gsblk = COL_GS // MERGE_TN
    gablk = COL_GA // MERGE_TN
    return pl.pallas_call(
        _merge_kernel,
        out_shape=jax.ShapeDtypeStruct((m, D_MODEL), BF16),
        grid=(m // MERGE_TM, D_MODEL // MERGE_TN),
        in_specs=[
            pl.BlockSpec((MERGE_TM, SSM_D_INNER), lambda i, j: (i, 0)),
            pl.BlockSpec((SSM_D_INNER, MERGE_TN), lambda i, j: (0, j)),
            pl.BlockSpec((MERGE_TM, ATT_V), lambda i, j: (i, 0)),
            pl.BlockSpec((ATT_V, MERGE_TN), lambda i, j: (0, j)),
            pl.BlockSpec((MERGE_TM, MERGE_TN), lambda i, j: (i, gsblk + j)),
            pl.BlockSpec((MERGE_TM, MERGE_TN), lambda i, j: (i, gablk + j)),
        ],
        out_specs=pl.BlockSpec((MERGE_TM, MERGE_TN), lambda i, j: (i, j)),
        compiler_params=_cparams(("parallel", "parallel")),
        name="merge",
    )(y_ssm, w_ssm, y_att, w_att, proj, proj)


OUTPROJ_TM = 256


def _outproj_kernel(x_ref, m_ref, w_ref, nw_ref, x1_ref, h2_ref):
    x1 = x_ref[...] + jnp.dot(m_ref[...], w_ref[...], preferred_element_type=F32)
    x1_ref[...] = x1
    h2_ref[...] = _rms_rows(x1, nw_ref[...], NORM_EPS).astype(h2_ref.dtype)


def _outproj(x2d, merged, w_out, norm_w):
    m = x2d.shape[0]
    return pl.pallas_call(
        _outproj_kernel,
        out_shape=(jax.ShapeDtypeStruct((m, D_MODEL), F32), jax.ShapeDtypeStruct((m, D_MODEL), BF16)),
        grid=(m // OUTPROJ_TM,),
        in_specs=[
            pl.BlockSpec((OUTPROJ_TM, D_MODEL), lambda i: (i, 0)),
            pl.BlockSpec((OUTPROJ_TM, D_MODEL), lambda i: (i, 0)),
            pl.BlockSpec((D_MODEL, D_MODEL), lambda i: (0, 0)),
            pl.BlockSpec((1, D_MODEL), lambda i: (0, 0)),
        ],
        out_specs=(
            pl.BlockSpec((OUTPROJ_TM, D_MODEL), lambda i: (i, 0)),
            pl.BlockSpec((OUTPROJ_TM, D_MODEL), lambda i: (i, 0)),
        ),
        compiler_params=_cparams(("parallel",)),
        name="outproj",
    )(x2d, merged, w_out, norm_w)


FFN_TM = 1024
FFN_TF = 512
DOWN_TM = 1024
DOWN_TN = 1024
DOWN_TK = FFN_HIDDEN // 4


def _ffn_up_kernel(h_ref, wg_ref, wu_ref, a_ref):
    h = h_ref[...]
    g = jnp.dot(h, wg_ref[...], preferred_element_type=F32)
    u = jnp.dot(h, wu_ref[...], preferred_element_type=F32)
    a_ref[...] = (g * jax.nn.sigmoid(g) * u).astype(a_ref.dtype)


def _ffn_up(h2, w_gate, w_up):
    m = h2.shape[0]
    return pl.pallas_call(
        _ffn_up_kernel,
        out_shape=jax.ShapeDtypeStruct((m, FFN_HIDDEN), BF16),
        grid=(m // FFN_TM, FFN_HIDDEN // FFN_TF),
        in_specs=[
            pl.BlockSpec((FFN_TM, D_MODEL), lambda i, j: (i, 0)),
            pl.BlockSpec((D_MODEL, FFN_TF), lambda i, j: (0, j)),
            pl.BlockSpec((D_MODEL, FFN_TF), lambda i, j: (0, j)),
        ],
        out_specs=pl.BlockSpec((FFN_TM, FFN_TF), lambda i, j: (i, j)),
        compiler_params=_cparams(("parallel", "parallel")),
        name="ffn_up",
    )(h2, w_gate, w_up)


def _ffn_down_kernel(a_ref, w_ref, x1_ref, o_ref, acc_ref):
    k = pl.program_id(2)

    @pl.when(k == 0)
    def _():
        acc_ref[...] = x1_ref[...]

    acc_ref[...] += jnp.dot(a_ref[...], w_ref[...], preferred_element_type=F32)

    @pl.when(k == pl.num_programs(2) - 1)
    def _():
        o_ref[...] = acc_ref[...]


def _ffn_down(a, w_down, x1):
    m = a.shape[0]
    return pl.pallas_call(
        _ffn_down_kernel,
        out_shape=jax.ShapeDtypeStruct((m, D_MODEL), F32),
        grid=(m // DOWN_TM, D_MODEL // DOWN_TN, FFN_HIDDEN // DOWN_TK),
        in_specs=[
            pl.BlockSpec((DOWN_TM, DOWN_TK), lambda i, j, k: (i, k)),
            pl.BlockSpec((DOWN_TK, DOWN_TN), lambda i, j, k: (k, j)),
            pl.BlockSpec((DOWN_TM, DOWN_TN), lambda i, j, k: (i, j)),
        ],
        out_specs=pl.BlockSpec((DOWN_TM, DOWN_TN), lambda i, j, k: (i, j)),
        scratch_shapes=[pltpu.VMEM((DOWN_TM, DOWN_TN), F32)],
        compiler_params=_cparams(("parallel", "parallel", "arbitrary")),
        name="ffn_down",
    )(a, w_down, x1)


def _rope_tables(seq):
    half = ATT_HEAD_DIM // 2
    inv_freq = ROPE_THETA ** (-jnp.arange(half, dtype=F32) / half)
    ang = jnp.arange(seq, dtype=jnp.int32).astype(F32)[:, None] * inv_freq[None, :]
    cos = jnp.cos(ang)
    sin = jnp.sin(ang)
    reps = ATT_PAIR // ATT_HEAD_DIM
    cos_t = jnp.tile(jnp.concatenate([cos, cos], axis=-1), (1, reps))
    sin_t = jnp.tile(jnp.concatenate([-sin, sin], axis=-1), (1, reps))
    return cos_t, sin_t


def _layer(x, mix_norm_w, w_in, conv_w, conv_b, dt_bias, a_log, d_skip, ssm_norm_w, w_ssm_proj,
           q_norm_w, k_norm_w, lambda_q1, lambda_k1, lambda_q2, lambda_k2, subln_w, w_attn_proj,
           w_out, ffn_norm_w, w_ffn_gate, w_ffn_up, w_ffn_down):
    b, s, d = x.shape
    m = b * s
    x2d = x.reshape(m, d)

    dt0 = SSM_D_INNER + SSM_CONV_CH
    dt1 = dt0 + SSM_HEADS
    w_main = jnp.concatenate([w_in[:, :dt0], w_in[:, dt1:]], axis=1).astype(BF16)
    w_dt = jnp.pad(w_in[:, dt0:dt1], ((0, 0), (0, DT_PAD - SSM_HEADS))).astype(BF16)

    proj, dt_raw = _inproj(x2d, mix_norm_w.reshape(1, d), w_main, w_dt)
    proj3 = proj.reshape(b, s, PROJ_COLS)

    dt_g = dt_raw[:, :SSM_HEADS].reshape(b, s, SSM_GROUPS, HPG)
    dt_col = jnp.transpose(dt_g, (0, 2, 1, 3))
    dt_row = jnp.transpose(dt_g, (0, 2, 3, 1))
    per_head_c = lambda v: v.astype(F32).reshape(SSM_GROUPS, 1, HPG)
    per_head_r = lambda v: v.astype(F32).reshape(SSM_GROUPS, HPG, 1)
    y_ssm = _ssd(proj3, dt_col, dt_row, conv_w, conv_b.reshape(1, SSM_CONV_CH),
                 per_head_c(dt_bias), per_head_c(a_log), per_head_c(d_skip),
                 per_head_r(dt_bias), per_head_r(a_log), ssm_norm_w.reshape(1, SSM_D_INNER))

    cos_t, sin_t = _rope_tables(s)
    pair_w = lambda v: jnp.tile(v.astype(F32), ATT_PAIR // ATT_HEAD_DIM).reshape(1, ATT_PAIR)
    lam_w = lambda v: v.astype(F32).reshape(1, ATT_HEAD_DIM)
    y_att = _attn(proj3, cos_t, sin_t, pair_w(q_norm_w), pair_w(k_norm_w),
                  lam_w(lambda_q1), lam_w(lambda_k1), lam_w(lambda_q2), lam_w(lambda_k2),
                  subln_w.astype(F32).reshape(1, ATT_PAIR))

    merged = _merge(y_ssm.reshape(m, SSM_D_INNER), w_ssm_proj.astype(BF16),
                    y_att.reshape(m, ATT_V), w_attn_proj.astype(BF16), proj)
    x1, h2 = _outproj(x2d, merged, w_out.astype(BF16), ffn_norm_w.reshape(1, d))
    a = _ffn_up(h2, w_ffn_gate.astype(BF16), w_ffn_up.astype(BF16))
    out = _ffn_down(a, w_ffn_down.astype(BF16), x1)
    return out.reshape(b, s, d)


def kernel(x, mix_norm_w, w_in, conv_w, conv_b, dt_bias, a_log, d_skip, ssm_norm_w, w_ssm_proj, q_norm_w, k_norm_w, lambda_q1, lambda_k1, lambda_q2, lambda_k2, subln_w, w_attn_proj, w_out, ffn_norm_w, w_ffn_gate, w_ffn_up, w_ffn_down):
    depth = w_in.shape[0]
    assert depth == 1, "LAM_INIT is derived for a single layer"
    layer = lambda v: v[0]
    return _layer(x, *(layer(v) for v in (
        mix_norm_w, w_in, conv_w, conv_b, dt_bias, a_log, d_skip, ssm_norm_w, w_ssm_proj, q_norm_w, k_norm_w,
        lambda_q1, lambda_k1, lambda_q2, lambda_k2, subln_w, w_attn_proj, w_out, ffn_norm_w, w_ffn_gate,
        w_ffn_up, w_ffn_down)))
```

```python
import functools
import math

import jax
import jax.numpy as jnp
from jax import lax
from jax.experimental import pallas as pl
from jax.experimental.pallas import tpu as pltpu

F32 = jnp.float32
BF16 = jnp.bfloat16

D_MODEL = 2048
SSM_D_INNER = 2 * D_MODEL
SSM_HEAD_DIM = 64
SSM_HEADS = SSM_D_INNER // SSM_HEAD_DIM
SSM_GROUPS = 8
SSM_HEADS_PER_GROUP = SSM_HEADS // SSM_GROUPS
SSM_GROUP_WIDTH = SSM_D_INNER // SSM_GROUPS
SSM_STATE = 128
SSM_CONV = 4
SSM_CHUNK = 256
SSM_CONV_CH = SSM_D_INNER + 2 * SSM_GROUPS * SSM_STATE
ATT_HEAD_DIM = 64
ATT_HEADS = D_MODEL // (2 * ATT_HEAD_DIM)
ATT_PAIR = 2 * ATT_HEAD_DIM
ATT_QK = 2 * ATT_HEADS * ATT_HEAD_DIM
ATT_V = ATT_HEADS * 2 * ATT_HEAD_DIM
ROPE_THETA = 10000.0
FFN_HIDDEN = (((8 * D_MODEL + 2) // 3 + 255) // 256) * 256
NORM_EPS = 1e-6
SUBLN_EPS = 1e-5
LAM_INIT = 0.8 - 0.6 * math.exp(-0.3 * 0)

COL_Z = 0
COL_X = COL_Z + SSM_D_INNER
COL_B = COL_X + SSM_D_INNER
COL_C = COL_B + SSM_GROUPS * SSM_STATE
COL_Q = COL_C + SSM_GROUPS * SSM_STATE
COL_K = COL_Q + ATT_QK
COL_V = COL_K + ATT_QK
COL_GS = COL_V + ATT_V
COL_GA = COL_GS + D_MODEL
PROJ_COLS = COL_GA + D_MODEL
LANE = 128
DT_PAD = LANE

VMEM_LIMIT = 56 * 1024 * 1024


def _cparams(sem):
    return pltpu.CompilerParams(dimension_semantics=sem, vmem_limit_bytes=VMEM_LIMIT)


INPROJ_TM = 1024
INPROJ_TN = 1024
NORM_ROWS = 256


def _rms_rows(x, w, eps):
    ms = jnp.mean(x * x, axis=-1, keepdims=True)
    return x * lax.rsqrt(ms + eps) * w


def _inproj_kernel(x_ref, nw_ref, w_ref, wdt_ref, out_ref, dt_ref, h_ref):
    @pl.when(pl.program_id(1) == 0)
    def _():
        def body(i, carry):
            rows = pl.ds(pl.multiple_of(i * NORM_ROWS, NORM_ROWS), NORM_ROWS)
            h_ref[rows, :] = _rms_rows(x_ref[rows, :], nw_ref[...], NORM_EPS).astype(BF16)
            return carry
        lax.fori_loop(0, INPROJ_TM // NORM_ROWS, body, 0)
        dt_ref[...] = jnp.dot(h_ref[...], wdt_ref[...], preferred_element_type=F32)

    out_ref[...] = jnp.dot(h_ref[...], w_ref[...], preferred_element_type=F32).astype(out_ref.dtype)


def _inproj(x2d, norm_w, w_main, w_dt):
    m = x2d.shape[0]
    return pl.pallas_call(
        _inproj_kernel,
        out_shape=(jax.ShapeDtypeStruct((m, PROJ_COLS), BF16), jax.ShapeDtypeStruct((m, DT_PAD), F32)),
        grid=(m // INPROJ_TM, PROJ_COLS // INPROJ_TN),
        in_specs=[
            pl.BlockSpec((INPROJ_TM, D_MODEL), lambda i, j: (i, 0)),
            pl.BlockSpec((1, D_MODEL), lambda i, j: (0, 0)),
            pl.BlockSpec((D_MODEL, INPROJ_TN), lambda i, j: (0, j)),
            pl.BlockSpec((D_MODEL, DT_PAD), lambda i, j: (0, 0)),
        ],
        out_specs=(
            pl.BlockSpec((INPROJ_TM, INPROJ_TN), lambda i, j: (i, j)),
            pl.BlockSpec((INPROJ_TM, DT_PAD), lambda i, j: (i, 0)),
        ),
        scratch_shapes=[pltpu.VMEM((INPROJ_TM, D_MODEL), BF16)],
        compiler_params=_cparams(("parallel", "arbitrary")),
        name="inproj",
    )(x2d, norm_w, w_main, w_dt)


L = SSM_CHUNK
HPG = SSM_HEADS_PER_GROUP
GW = SSM_GROUP_WIDTH
TAIL = 8


def _conv_silu(in_ref, tail_ref, w_ref, b_ref, ext_ref, width):
    cur = in_ref[...].astype(F32)
    ext_ref[0:TAIL, 0:width] = tail_ref[...]
    ext_ref[TAIL:TAIL + L, 0:width] = cur
    w = w_ref[...]
    u = b_ref[...] + w[3:4, :] * cur
    for k in range(SSM_CONV - 1):
        off = TAIL - (SSM_CONV - 1) + k
        u = u + w[k:k + 1, :] * ext_ref[off:off + L, 0:width]
    tail_ref[...] = cur[L - TAIL:L, :]
    return u * jax.nn.sigmoid(u)


def _ssd_kernel(x_ref, b_ref, c_ref, z_ref, dtc_ref, dtr_ref,
                cwx_ref, cwb_ref, cwc_ref, cbx_ref, cbb_ref, cbc_ref,
                biasc_ref, alogc_ref, dskipc_ref, biasr_ref, alogr_ref, nw_ref,
                y_ref,
                state_ref, tailx_ref, tailb_ref, tailc_ref, ext_ref, xs_ref, xde_ref, ycat_ref):
    @pl.when(pl.program_id(2) == 0)
    def _():
        state_ref[...] = jnp.zeros_like(state_ref)
        tailx_ref[...] = jnp.zeros_like(tailx_ref)
        tailb_ref[...] = jnp.zeros_like(tailb_ref)
        tailc_ref[...] = jnp.zeros_like(tailc_ref)

    xs_ref[...] = _conv_silu(x_ref, tailx_ref, cwx_ref, cbx_ref, ext_ref, GW)
    bm = _conv_silu(b_ref, tailb_ref, cwb_ref, cbb_ref, ext_ref, SSM_STATE)
    cm = _conv_silu(c_ref, tailc_ref, cwc_ref, cbc_ref, ext_ref, SSM_STATE)
    bm_bf = bm.astype(BF16)
    cm_bf = cm.astype(BF16)
    bt_bf = bm.T.astype(BF16)

    dtc = jax.nn.softplus(dtc_ref[...] + biasc_ref[...])
    dtr = jax.nn.softplus(dtr_ref[...] + biasr_ref[...])
    dac = dtc * (-jnp.exp(alogc_ref[...]))
    dar = dtr * (-jnp.exp(alogr_ref[...]))
    row = lax.broadcasted_iota(jnp.int32, (L, L), 0)
    col = lax.broadcasted_iota(jnp.int32, (L, L), 1)
    causal = row >= col
    tri = causal.astype(F32)
    tri_t = (row <= col).astype(F32)
    acs_c = jnp.dot(tri, dac, preferred_element_type=F32, precision=lax.Precision.HIGHEST)
    acs_r = jnp.dot(dar, tri_t, preferred_element_type=F32, precision=lax.Precision.HIGHEST)
    a_last = acs_c[L - 1:L, :]
    dec_start = jnp.exp(acs_c)
    dec_end = jnp.exp(a_last - acs_c)
    dec_chunk = jnp.exp(a_last)
    dskip = dskipc_ref[...]

    cb = lax.dot_general(cm_bf, bm_bf, (((1,), (1,)), ((), ())), preferred_element_type=F32)
    y_off = jnp.dot(cm_bf, state_ref[...].astype(BF16), preferred_element_type=F32)

    lane = lax.broadcasted_iota(jnp.int32, (1, LANE), 1)
    first = lane < SSM_HEAD_DIM

    def pair(vals, p):
        return jnp.where(first, vals[:, 2 * p:2 * p + 1], vals[:, 2 * p + 1:2 * p + 2])

    for p in range(HPG // 2):
        lanes = slice(p * LANE, (p + 1) * LANE)
        xs_p = xs_ref[:, lanes]
        xdt_p = xs_p * pair(dtc, p)
        xdt_bf = xdt_p.astype(BF16)
        xde_ref[:, lanes] = (xdt_p * pair(dec_end, p)).astype(BF16)
        y_heads = []
        for r in (2 * p, 2 * p + 1):
            dec = jnp.exp(jnp.where(causal, acs_c[:, r:r + 1] - acs_r[r:r + 1, :], -jnp.inf))
            y_heads.append(jnp.dot((cb * dec).astype(BF16), xdt_bf, preferred_element_type=F32))
        y_diag = jnp.where(first, y_heads[0], y_heads[1])
        ycat_ref[:, lanes] = y_diag + y_off[:, lanes] * pair(dec_start, p) + pair(dskip, p) * xs_p

    upd = jnp.dot(bt_bf, xde_ref[...], preferred_element_type=F32)
    for p in range(HPG // 2):
        lanes = slice(p * LANE, (p + 1) * LANE)
        state_ref[:, lanes] = state_ref[:, lanes] * pair(dec_chunk, p) + upd[:, lanes]

    z = z_ref[...].astype(F32)
    y = ycat_ref[...] * (z * jax.nn.sigmoid(z))
    y = y * lax.rsqrt(jnp.mean(y * y, axis=-1, keepdims=True) + SUBLN_EPS)
    y_ref[...] = (y * nw_ref[...]).astype(y_ref.dtype)


def _ssd(proj3, dt_col, dt_row, conv_w, conv_b, bias_c, alog_c, dskip_c, bias_r, alog_r, norm_w):
    b, s, _ = proj3.shape
    nc = s // L
    xblk = COL_X // GW
    bblk = COL_B // SSM_STATE
    cblk = COL_C // SSM_STATE
    cw_b = SSM_D_INNER // SSM_STATE
    cw_c = cw_b + SSM_GROUPS
    in_specs = [
        pl.BlockSpec((None, L, GW), lambda bi, g, c: (bi, c, xblk + g)),
        pl.BlockSpec((None, L, SSM_STATE), lambda bi, g, c: (bi, c, bblk + g)),
        pl.BlockSpec((None, L, SSM_STATE), lambda bi, g, c: (bi, c, cblk + g)),
        pl.BlockSpec((None, L, GW), lambda bi, g, c: (bi, c, g)),
        pl.BlockSpec((None, None, L, HPG), lambda bi, g, c: (bi, g, c, 0)),
        pl.BlockSpec((None, None, HPG, L), lambda bi, g, c: (bi, g, 0, c)),
        pl.BlockSpec((SSM_CONV, GW), lambda bi, g, c: (0, g)),
        pl.BlockSpec((SSM_CONV, SSM_STATE), lambda bi, g, c: (0, cw_b + g)),
        pl.BlockSpec((SSM_CONV, SSM_STATE), lambda bi, g, c: (0, cw_c + g)),
        pl.BlockSpec((1, GW), lambda bi, g, c: (0, g)),
        pl.BlockSpec((1, SSM_STATE), lambda bi, g, c: (0, cw_b + g)),
        pl.BlockSpec((1, SSM_STATE), lambda bi, g, c: (0, cw_c + g)),
        pl.BlockSpec((None, 1, HPG), lambda bi, g, c: (g, 0, 0)),
        pl.BlockSpec((None, 1, HPG), lambda bi, g, c: (g, 0, 0)),
        pl.BlockSpec((None, 1, HPG), lambda bi, g, c: (g, 0, 0)),
        pl.BlockSpec((None, HPG, 1), lambda bi, g, c: (g, 0, 0)),
        pl.BlockSpec((None, HPG, 1), lambda bi, g, c: (g, 0, 0)),
        pl.BlockSpec((1, GW), lambda bi, g, c: (0, g)),
    ]
    return pl.pallas_call(
        _ssd_kernel,
        out_shape=jax.ShapeDtypeStruct((b, s, SSM_D_INNER), BF16),
        grid=(b, SSM_GROUPS, nc),
        in_specs=in_specs,
        out_specs=pl.BlockSpec((None, L, GW), lambda bi, g, c: (bi, c, g)),
        scratch_shapes=[
            pltpu.VMEM((SSM_STATE, GW), F32),
            pltpu.VMEM((TAIL, GW), F32),
            pltpu.VMEM((TAIL, SSM_STATE), F32),
            pltpu.VMEM((TAIL, SSM_STATE), F32),
            pltpu.VMEM((TAIL + L, GW), F32),
            pltpu.VMEM((L, GW), F32),
            pltpu.VMEM((L, GW), BF16),
            pltpu.VMEM((L, GW), F32),
        ],
        compiler_params=_cparams(("parallel", "parallel", "arbitrary")),
        name="ssd",
    )(proj3, proj3, proj3, proj3, dt_col, dt_row, conv_w, conv_w, conv_w, conv_b, conv_b, conv_b,
      bias_c, alog_c, dskip_c, bias_r, alog_r, norm_w)


ATT_TQ = 512
ATT_TK = 512
ROPE_ROWS = 1024
ATT_HEADS_PER_PAIR = 2


def _norm_rope(x, w, cos, sin_signed, group_ones):
    ss = jnp.dot((x * x).astype(BF16), group_ones, preferred_element_type=F32)
    xn = x * lax.rsqrt(ss * (1.0 / ATT_HEAD_DIM) + NORM_EPS) * w
    lane = lax.broadcasted_iota(jnp.int32, (1, ATT_PAIR), 1)
    low_half = (lane % ATT_HEAD_DIM) < (ATT_HEAD_DIM // 2)
    half = ATT_HEAD_DIM // 2
    rot = jnp.where(low_half, pltpu.roll(xn, ATT_PAIR - half, 1), pltpu.roll(xn, half, 1))
    return xn * cos + rot * sin_signed


def _fold_lanes(x, op):
    out = x[:, 0:LANE]
    for t in range(1, x.shape[1] // LANE):
        out = op(out, x[:, t * LANE:(t + 1) * LANE])
    return out


def _attn_kernel(q_ref, k_ref, v_ref, cos_ref, sin_ref, qw_ref, kw_ref,
                 lq1_ref, lk1_ref, lq2_ref, lk2_ref, sw_ref, o_ref,
                 krot_ref, qm_ref, s_ref, stat_ref, acc_ref):
    qi = pl.program_id(2)
    ri = lax.broadcasted_iota(jnp.int32, (ATT_PAIR, ATT_PAIR), 0) // ATT_HEAD_DIM
    ci = lax.broadcasted_iota(jnp.int32, (ATT_PAIR, ATT_PAIR), 1) // ATT_HEAD_DIM
    group_ones = (ri == ci).astype(BF16)

    @pl.when(qi == 0)
    def _():
        def body(i, carry):
            rows = pl.ds(pl.multiple_of(i * ROPE_ROWS, ROPE_ROWS), ROPE_ROWS)
            kr = _norm_rope(k_ref[rows, :].astype(F32), kw_ref[...], cos_ref[rows, :], sin_ref[rows, :], group_ones)
            krot_ref[rows, :] = kr.astype(BF16)
            return carry
        lax.fori_loop(0, k_ref.shape[0] // ROPE_ROWS, body, 0)

    qrows = pl.ds(pl.multiple_of(qi * ATT_TQ, ATT_TQ), ATT_TQ)
    q = _norm_rope(q_ref[...].astype(F32), qw_ref[...], cos_ref[qrows, :], sin_ref[qrows, :], group_ones)
    q = q * (ATT_HEAD_DIM ** -0.5)
    lane = lax.broadcasted_iota(jnp.int32, (1, ATT_PAIR), 1)
    qm_ref[0] = jnp.where(lane < ATT_HEAD_DIM, q, 0.0).astype(BF16)
    qm_ref[1] = jnp.where(lane >= ATT_HEAD_DIM, q, 0.0).astype(BF16)

    def kv_rows(j):
        return pl.ds(pl.multiple_of(j * ATT_TK, ATT_TK), ATT_TK)

    def score_block(j, masked):
        kj = krot_ref[kv_rows(j), :]
        for h in range(ATT_HEADS_PER_PAIR):
            s = lax.dot_general(qm_ref[h], kj, (((1,), (1,)), ((), ())), preferred_element_type=F32)
            if masked:
                r = lax.broadcasted_iota(jnp.int32, (ATT_TQ, ATT_TK), 0)
                c = lax.broadcasted_iota(jnp.int32, (ATT_TQ, ATT_TK), 1)
                s = jnp.where(r >= c, s, -jnp.inf)
            s_ref[h, j] = s
            stat_ref[h] = jnp.maximum(stat_ref[h], _fold_lanes(s, jnp.maximum))

    stat_ref[0:ATT_HEADS_PER_PAIR] = jnp.full((ATT_HEADS_PER_PAIR, ATT_TQ, LANE), -jnp.inf, F32)

    def pass1(j, carry):
        score_block(j, False)
        return carry
    lax.fori_loop(0, qi, pass1, 0)
    score_block(qi, True)

    for h in range(ATT_HEADS_PER_PAIR):
        row_max = jnp.max(stat_ref[h], axis=-1, keepdims=True)
        stat_ref[ATT_HEADS_PER_PAIR + h] = jnp.broadcast_to(row_max, (ATT_TQ, LANE))
        stat_ref[h] = jnp.zeros((ATT_TQ, LANE), F32)
    acc_ref[...] = jnp.zeros_like(acc_ref)

    def pass2(j, carry):
        vj = v_ref[kv_rows(j), :]
        for h in range(ATT_HEADS_PER_PAIR):
            m = stat_ref[ATT_HEADS_PER_PAIR + h]
            s = s_ref[h, j]
            p = jnp.concatenate(
                [jnp.exp(s[:, t * LANE:(t + 1) * LANE] - m) for t in range(ATT_TK // LANE)], axis=1)
            stat_ref[h] = stat_ref[h] + _fold_lanes(p, jnp.add)
            acc_ref[h] = acc_ref[h] + jnp.dot(p.astype(BF16), vj, preferred_element_type=F32)
        return carry
    lax.fori_loop(0, qi + 1, pass2, 0)

    l1 = jnp.sum(stat_ref[0], axis=-1, keepdims=True)
    l2 = jnp.sum(stat_ref[1], axis=-1, keepdims=True)
    lam = (jnp.exp(jnp.sum(lq1_ref[...] * lk1_ref[...], axis=-1, keepdims=True))
           - jnp.exp(jnp.sum(lq2_ref[...] * lk2_ref[...], axis=-1, keepdims=True)) + LAM_INIT)
    o = acc_ref[0] / l1 - lam * (acc_ref[1] / l2)
    o = o * lax.rsqrt(jnp.mean(o * o, axis=-1, keepdims=True) + SUBLN_EPS) * sw_ref[...]
    o_ref[...] = (o * (1.0 - LAM_INIT)).astype(o_ref.dtype)


def _attn(proj3, cos_t, sin_t, qw, kw, lq1, lk1, lq2, lk2, sw):
    b, s, _ = proj3.shape
    qblk = COL_Q // ATT_PAIR
    kblk = COL_K // ATT_PAIR
    vblk = COL_V // ATT_PAIR
    small = lambda width: pl.BlockSpec((1, width), lambda bi, h, qi: (0, 0))
    return pl.pallas_call(
        _attn_kernel,
        out_shape=jax.ShapeDtypeStruct((b, s, ATT_V), BF16),
        grid=(b, ATT_HEADS, s // ATT_TQ),
        in_specs=[
            pl.BlockSpec((None, ATT_TQ, ATT_PAIR), lambda bi, h, qi: (bi, qi, qblk + h)),
            pl.BlockSpec((None, s, ATT_PAIR), lambda bi, h, qi: (bi, 0, kblk + h)),
            pl.BlockSpec((None, s, ATT_PAIR), lambda bi, h, qi: (bi, 0, vblk + h)),
            pl.BlockSpec((s, ATT_PAIR), lambda bi, h, qi: (0, 0)),
            pl.BlockSpec((s, ATT_PAIR), lambda bi, h, qi: (0, 0)),
            small(ATT_PAIR), small(ATT_PAIR),
            small(ATT_HEAD_DIM), small(ATT_HEAD_DIM), small(ATT_HEAD_DIM), small(ATT_HEAD_DIM),
            small(ATT_PAIR),
        ],
        out_specs=pl.BlockSpec((None, ATT_TQ, ATT_PAIR), lambda bi, h, qi: (bi, qi, h)),
        scratch_shapes=[
            pltpu.VMEM((s, ATT_PAIR), BF16),
            pltpu.VMEM((ATT_HEADS_PER_PAIR, ATT_TQ, ATT_PAIR), BF16),
            pltpu.VMEM((ATT_HEADS_PER_PAIR, s // ATT_TK, ATT_TQ, ATT_TK), F32),
            pltpu.VMEM((2 * ATT_HEADS_PER_PAIR, ATT_TQ, LANE), F32),
            pltpu.VMEM((ATT_HEADS_PER_PAIR, ATT_TQ, ATT_PAIR), F32),
        ],
        compiler_params=_cparams(("parallel", "parallel", "arbitrary")),
        name="attn",
    )(proj3, proj3, proj3, cos_t, sin_t, qw, kw, lq1, lk1, lq2, lk2, sw)


MERGE_TM = 512
MERGE_TN = 512


def _merge_kernel(ys_ref, ws_ref, ya_ref, wa_ref, gs_ref, ga_ref, o_ref):
    bs = jnp.dot(ys_ref[...], ws_ref[...], preferred_element_type=F32)
    ba = jnp.dot(ya_ref[...], wa_ref[...], preferred_element_type=F32)
    gs = jax.nn.sigmoid(gs_ref[...].astype(F32))
    ga = jax.nn.sigmoid(ga_ref[...].astype(F32))
    o_ref[...] = (gs * bs + ga * ba).astype(o_ref.dtype)


def _merge(y_ssm, w_ssm, y_att, w_att, proj):
    m = y_ssm.shape[0]
    gsblk = COL_GS // MERGE_TN
    gablk = COL_GA // MERGE_TN
    return pl.pallas_call(
        _merge_kernel,
        out_shape=jax.ShapeDtypeStruct((m, D_MODEL), BF16),
        grid=(m // MERGE_TM, D_MODEL // MERGE_TN),
        in_specs=[
            pl.BlockSpec((MERGE_TM, SSM_D_INNER), lambda i, j: (i, 0)),
            pl.BlockSpec((SSM_D_INNER, MERGE_TN), lambda i, j: (0, j)),
            pl.BlockSpec((MERGE_TM, ATT_V), lambda i, j: (i, 0)),
            pl.BlockSpec((ATT_V, MERGE_TN), lambda i, j: (0, j)),
            pl.BlockSpec((MERGE_TM, MERGE_TN), lambda i, j: (i, gsblk + j)),
            pl.BlockSpec((MERGE_TM, MERGE_TN), lambda i, j: (i, gablk + j)),
        ],
        out_specs=pl.BlockSpec((MERGE_TM, MERGE_TN), lambda i, j: (i, j)),
        compiler_params=_cparams(("parallel", "parallel")),
        name="merge",
    )(y_ssm, w_ssm, y_att, w_att, proj, proj)


OUTPROJ_TM = 256


def _outproj_kernel(x_ref, m_ref, w_ref, nw_ref, x1_ref, h2_ref):
    x1 = x_ref[...] + jnp.dot(m_ref[...], w_ref[...], preferred_element_type=F32)
    x1_ref[...] = x1
    h2_ref[...] = _rms_rows(x1, nw_ref[...], NORM_EPS).astype(h2_ref.dtype)


def _outproj(x2d, merged, w_out, norm_w):
    m = x2d.shape[0]
    return pl.pallas_call(
        _outproj_kernel,
        out_shape=(jax.ShapeDtypeStruct((m, D_MODEL), F32), jax.ShapeDtypeStruct((m, D_MODEL), BF16)),
        grid=(m // OUTPROJ_TM,),
        in_specs=[
            pl.BlockSpec((OUTPROJ_TM, D_MODEL), lambda i: (i, 0)),
            pl.BlockSpec((OUTPROJ_TM, D_MODEL), lambda i: (i, 0)),
            pl.BlockSpec((D_MODEL, D_MODEL), lambda i: (0, 0)),
            pl.BlockSpec((1, D_MODEL), lambda i: (0, 0)),
        ],
        out_specs=(
            pl.BlockSpec((OUTPROJ_TM, D_MODEL), lambda i: (i, 0)),
            pl.BlockSpec((OUTPROJ_TM, D_MODEL), lambda i: (i, 0)),
        ),
        compiler_params=_cparams(("parallel",)),
        name="outproj",
    )(x2d, merged, w_out, norm_w)


FFN_TM = 1024
FFN_TF = 512
DOWN_TM = 1024
DOWN_TN = 1024
DOWN_TK = FFN_HIDDEN // 4


def _ffn_up_kernel(h_ref, wg_ref, wu_ref, a_ref):
    h = h_ref[...]
    g = jnp.dot(h, wg_ref[...], preferred_element_type=F32)
    u = jnp.dot(h, wu_ref[...], preferred_element_type=F32)
    a_ref[...] = (g * jax.nn.sigmoid(g) * u).astype(a_ref.dtype)


def _ffn_up(h2, w_gate, w_up):
    m = h2.shape[0]
    return pl.pallas_call(
        _ffn_up_kernel,
        out_shape=jax.ShapeDtypeStruct((m, FFN_HIDDEN), BF16),
        grid=(m // FFN_TM, FFN_HIDDEN // FFN_TF),
        in_specs=[
            pl.BlockSpec((FFN_TM, D_MODEL), lambda i, j: (i, 0)),
            pl.BlockSpec((D_MODEL, FFN_TF), lambda i, j: (0, j)),
            pl.BlockSpec((D_MODEL, FFN_TF), lambda i, j: (0, j)),
        ],
        out_specs=pl.BlockSpec((FFN_TM, FFN_TF), lambda i, j: (i, j)),
        compiler_params=_cparams(("parallel", "parallel")),
        name="ffn_up",
    )(h2, w_gate, w_up)


def _ffn_down_kernel(a_ref, w_ref, x1_ref, o_ref, acc_ref):
    k = pl.program_id(2)

    @pl.when(k == 0)
    def _():
        acc_ref[...] = x1_ref[...]

    acc_ref[...] += jnp.dot(a_ref[...], w_ref[...], preferred_element_type=F32)

    @pl.when(k == pl.num_programs(2) - 1)
    def _():
        o_ref[...] = acc_ref[...]


def _ffn_down(a, w_down, x1):
    m = a.shape[0]
    return pl.pallas_call(
        _ffn_down_kernel,
        out_shape=jax.ShapeDtypeStruct((m, D_MODEL), F32),
        grid=(m // DOWN_TM, D_MODEL // DOWN_TN, FFN_HIDDEN // DOWN_TK),
        in_specs=[
            pl.BlockSpec((DOWN_TM, DOWN_TK), lambda i, j, k: (i, k)),
            pl.BlockSpec((DOWN_TK, DOWN_TN), lambda i, j, k: (k, j)),
            pl.BlockSpec((DOWN_TM, DOWN_TN), lambda i, j, k: (i, j)),
        ],
        out_specs=pl.BlockSpec((DOWN_TM, DOWN_TN), lambda i, j, k: (i, j)),
        scratch_shapes=[pltpu.VMEM((DOWN_TM, DOWN_TN), F32)],
        compiler_params=_cparams(("parallel", "parallel", "arbitrary")),
        name="ffn_down",
    )(a, w_down, x1)


def _rope_tables(seq):
    half = ATT_HEAD_DIM // 2
    inv_freq = ROPE_THETA ** (-jnp.arange(half, dtype=F32) / half)
    ang = jnp.arange(seq, dtype=jnp.int32).astype(F32)[:, None] * inv_freq[None, :]
    cos = jnp.cos(ang)
    sin = jnp.sin(ang)
    reps = ATT_PAIR // ATT_HEAD_DIM
    cos_t = jnp.tile(jnp.concatenate([cos, cos], axis=-1), (1, reps))
    sin_t = jnp.tile(jnp.concatenate([-sin, sin], axis=-1), (1, reps))
    return cos_t, sin_t


def _layer(x, mix_norm_w, w_in, conv_w, conv_b, dt_bias, a_log, d_skip, ssm_norm_w, w_ssm_proj,
           q_norm_w, k_norm_w, lambda_q1, lambda_k1, lambda_q2, lambda_k2, subln_w, w_attn_proj,
           w_out, ffn_norm_w, w_ffn_gate, w_ffn_up, w_ffn_down):
    b, s, d = x.shape
    m = b * s
    x2d = x.reshape(m, d)

    dt0 = SSM_D_INNER + SSM_CONV_CH
    dt1 = dt0 + SSM_HEADS
    w_main = jnp.concatenate([w_in[:, :dt0], w_in[:, dt1:]], axis=1).astype(BF16)
    w_dt = jnp.pad(w_in[:, dt0:dt1], ((0, 0), (0, DT_PAD - SSM_HEADS))).astype(BF16)

    proj, dt_raw = _inproj(x2d, mix_norm_w.reshape(1, d), w_main, w_dt)
    proj3 = proj.reshape(b, s, PROJ_COLS)

    dt_g = dt_raw[:, :SSM_HEADS].reshape(b, s, SSM_GROUPS, HPG)
    dt_col = jnp.transpose(dt_g, (0, 2, 1, 3))
    dt_row = jnp.transpose(dt_g, (0, 2, 3, 1))
    per_head_c = lambda v: v.astype(F32).reshape(SSM_GROUPS, 1, HPG)
    per_head_r = lambda v: v.astype(F32).reshape(SSM_GROUPS, HPG, 1)
    y_ssm = _ssd(proj3, dt_col, dt_row, conv_w, conv_b.reshape(1, SSM_CONV_CH),
                 per_head_c(dt_bias), per_head_c(a_log), per_head_c(d_skip),
                 per_head_r(dt_bias), per_head_r(a_log), ssm_norm_w.reshape(1, SSM_D_INNER))

    cos_t, sin_t = _rope_tables(s)
    pair_w = lambda v: jnp.tile(v.astype(F32), ATT_PAIR // ATT_HEAD_DIM).reshape(1, ATT_PAIR)
    lam_w = lambda v: v.astype(F32).reshape(1, ATT_HEAD_DIM)
    y_att = _attn(proj3, cos_t, sin_t, pair_w(q_norm_w), pair_w(k_norm_w),
                  lam_w(lambda_q1), lam_w(lambda_k1), lam_w(lambda_q2), lam_w(lambda_k2),
                  subln_w.astype(F32).reshape(1, ATT_PAIR))

    merged = _merge(y_ssm.reshape(m, SSM_D_INNER), w_ssm_proj.astype(BF16),
                    y_att.reshape(m, ATT_V), w_attn_proj.astype(BF16), proj)
    x1, h2 = _outproj(x2d, merged, w_out.astype(BF16), ffn_norm_w.reshape(1, d))
    a = _ffn_up(h2, w_ffn_gate.astype(BF16), w_ffn_up.astype(BF16))
    out = _ffn_down(a, w_ffn_down.astype(BF16), x1)
    return out.reshape(b, s, d)


def kernel(x, mix_norm_w, w_in, conv_w, conv_b, dt_bias, a_log, d_skip, ssm_norm_w, w_ssm_proj, q_norm_w, k_norm_w, lambda_q1, lambda_k1, lambda_q2, lambda_k2, subln_w, w_attn_proj, w_out, ffn_norm_w, w_ffn_gate, w_ffn_up, w_ffn_down):
    depth = w_in.shape[0]
    assert depth == 1, "LAM_INIT is derived for a single layer"
    layer = lambda v: v[0]
    return _layer(x, *(layer(v) for v in (
        mix_norm_w, w_in, conv_w, conv_b, dt_bias, a_log, d_skip, ssm_norm_w, w_ssm_proj, q_norm_w, k_norm_w,
        lambda_q1, lambda_k1, lambda_q2, lambda_k2, subln_w, w_attn_proj, w_out, ffn_norm_w, w_ffn_gate,
        w_ffn_up, w_ffn_down)))
```

```python
import math

import jax
import jax.numpy as jnp
from jax import lax
from jax.experimental import pallas as pl
from jax.experimental.pallas import tpu as pltpu

F32 = jnp.float32
BF16 = jnp.bfloat16

D_MODEL = 2048
SSM_D_INNER = 2 * D_MODEL
SSM_HEAD_DIM = 64
SSM_HEADS = SSM_D_INNER // SSM_HEAD_DIM
SSM_GROUPS = 8
SSM_HEADS_PER_GROUP = SSM_HEADS // SSM_GROUPS
SSM_GROUP_WIDTH = SSM_D_INNER // SSM_GROUPS
SSM_STATE = 128
SSM_CONV = 4
SSM_CHUNK = 256
SSM_CONV_CH = SSM_D_INNER + 2 * SSM_GROUPS * SSM_STATE
ATT_HEAD_DIM = 64
ATT_HEADS = D_MODEL // (2 * ATT_HEAD_DIM)
ATT_PAIR = 2 * ATT_HEAD_DIM
ATT_QK = 2 * ATT_HEADS * ATT_HEAD_DIM
ATT_V = ATT_HEADS * 2 * ATT_HEAD_DIM
ROPE_THETA = 10000.0
FFN_HIDDEN = (((8 * D_MODEL + 2) // 3 + 255) // 256) * 256
NORM_EPS = 1e-6
SUBLN_EPS = 1e-5
LAM_INIT = 0.8 - 0.6 * math.exp(-0.3 * 0)
LOG2E = math.log2(math.e)

COL_Z = 0
COL_X = COL_Z + SSM_D_INNER
COL_B = COL_X + SSM_D_INNER
COL_C = COL_B + SSM_GROUPS * SSM_STATE
COL_Q = COL_C + SSM_GROUPS * SSM_STATE
COL_K = COL_Q + ATT_QK
COL_V = COL_K + ATT_QK
COL_GS = COL_V + ATT_V
COL_GA = COL_GS + D_MODEL
PROJ_COLS = COL_GA + D_MODEL
LANE = 128
DT_PAD = LANE

VMEM_LIMIT = 56 * 1024 * 1024


def _cparams(sem):
    return pltpu.CompilerParams(dimension_semantics=sem, vmem_limit_bytes=VMEM_LIMIT)


def _silu(u):
    h = 0.5 * u
    return h * jnp.tanh(h) + h


INPROJ_TM = 1024
INPROJ_TN = 1024
NORM_ROWS = 256


def _rms_rows(x, w, eps):
    ms = jnp.mean(x * x, axis=-1, keepdims=True)
    return x * lax.rsqrt(ms + eps) * w


def _inproj_kernel(x_ref, nw_ref, w_ref, wdt_ref, out_ref, dt_ref, h_ref):
    @pl.when(pl.program_id(1) == 0)
    def _():
        def body(i, carry):
            rows = pl.ds(pl.multiple_of(i * NORM_ROWS, NORM_ROWS), NORM_ROWS)
            h_ref[rows, :] = _rms_rows(x_ref[rows, :], nw_ref[...], NORM_EPS).astype(BF16)
            return carry
        lax.fori_loop(0, INPROJ_TM // NORM_ROWS, body, 0)
        dt_ref[...] = jnp.dot(h_ref[...], wdt_ref[...], preferred_element_type=F32)

    out_ref[...] = jnp.dot(h_ref[...], w_ref[...], preferred_element_type=F32).astype(out_ref.dtype)


def _inproj(x2d, norm_w, w_main, w_dt):
    m = x2d.shape[0]
    return pl.pallas_call(
        _inproj_kernel,
        out_shape=(jax.ShapeDtypeStruct((m, PROJ_COLS), BF16), jax.ShapeDtypeStruct((m, DT_PAD), F32)),
        grid=(m // INPROJ_TM, PROJ_COLS // INPROJ_TN),
        in_specs=[
            pl.BlockSpec((INPROJ_TM, D_MODEL), lambda i, j: (i, 0)),
            pl.BlockSpec((1, D_MODEL), lambda i, j: (0, 0)),
            pl.BlockSpec((D_MODEL, INPROJ_TN), lambda i, j: (0, j)),
            pl.BlockSpec((D_MODEL, DT_PAD), lambda i, j: (0, 0)),
        ],
        out_specs=(
            pl.BlockSpec((INPROJ_TM, INPROJ_TN), lambda i, j: (i, j)),
            pl.BlockSpec((INPROJ_TM, DT_PAD), lambda i, j: (i, 0)),
        ),
        scratch_shapes=[pltpu.VMEM((INPROJ_TM, D_MODEL), BF16)],
        compiler_params=_cparams(("parallel", "arbitrary")),
        name="inproj",
    )(x2d, norm_w, w_main, w_dt)


L = SSM_CHUNK
TAIL = 16
TAPS_BACK = SSM_CONV - 1
CONV_TW = 2048
CONV_SLAB = 512


def _xconv_kernel(cur_ref, prev_ref, w_ref, b_ref, shift_ref, tshift_ref, o_ref):
    has_prev = pl.program_id(1) > 0
    for s0 in range(0, CONV_TW, CONV_SLAB):
        cols = slice(s0, s0 + CONV_SLAB)
        cur_bf = cur_ref[:, cols]
        prev_bf = jnp.where(has_prev, prev_ref[:, cols], jnp.zeros((TAIL, CONV_SLAB), BF16))
        cur = cur_bf.astype(F32)
        shifted = jnp.dot(shift_ref[...], cur_bf, preferred_element_type=F32)
        carried = jnp.dot(tshift_ref[...], prev_bf, preferred_element_type=F32)
        w = w_ref[:, cols]
        u = b_ref[:, cols] + w[TAPS_BACK:SSM_CONV, :] * cur
        top = jnp.zeros((TAIL, CONV_SLAB), F32)
        for k in range(TAPS_BACK):
            u = u + w[k:k + 1, :] * shifted[k * L:(k + 1) * L, :]
            top = top + w[k:k + 1, :] * carried[k * TAIL:(k + 1) * TAIL, :]
        u = jnp.concatenate([u[0:TAIL, :] + top, u[TAIL:L, :]], axis=0)
        o_ref[:, cols] = _silu(u).astype(o_ref.dtype)


def _xconv(proj3, conv_w, conv_b):
    b, s, _ = proj3.shape
    nc = s // L
    xblk = COL_X // CONV_TW
    tail_per_block = L // TAIL
    shift = jnp.concatenate([jnp.eye(L, L, k=-(TAPS_BACK - k), dtype=BF16) for k in range(TAPS_BACK)], axis=0)
    tshift = jnp.concatenate(
        [jnp.eye(TAIL, TAIL, k=TAIL - (TAPS_BACK - k), dtype=BF16) for k in range(TAPS_BACK)], axis=0)
    return pl.pallas_call(
        _xconv_kernel,
        out_shape=jax.ShapeDtypeStruct((b, s, SSM_CONV_CH), BF16),
        grid=(b, nc, SSM_CONV_CH // CONV_TW),
        in_specs=[
            pl.BlockSpec((None, L, CONV_TW), lambda bi, c, j: (bi, c, xblk + j)),
            pl.BlockSpec((None, TAIL, CONV_TW),
                         lambda bi, c, j: (bi, jnp.maximum(c * tail_per_block - 1, 0), xblk + j)),
            pl.BlockSpec((SSM_CONV, CONV_TW), lambda bi, c, j: (0, j)),
            pl.BlockSpec((1, CONV_TW), lambda bi, c, j: (0, j)),
            pl.BlockSpec((TAPS_BACK * L, L), lambda bi, c, j: (0, 0)),
            pl.BlockSpec((TAPS_BACK * TAIL, TAIL), lambda bi, c, j: (0, 0)),
        ],
        out_specs=pl.BlockSpec((None, L, CONV_TW), lambda bi, c, j: (bi, c, j)),
        compiler_params=_cparams(("parallel", "parallel", "parallel")),
        name="xconv",
    )(proj3, proj3, conv_w, conv_b, shift, tshift)


HALF = L // 2
HPG = SSM_HEADS_PER_GROUP
GW = SSM_GROUP_WIDTH
PAIRS = HPG // 2


def _split3(x):
    hi = x.astype(BF16)
    rest = x - hi.astype(F32)
    mid = rest.astype(BF16)
    lo = (rest - mid.astype(F32)).astype(BF16)
    return hi, mid, lo


def _ssd_kernel(x_ref, b_ref, c_ref, z_ref, dtc_ref, dtr_ref,
                biasc_ref, alogc_ref, biasr_ref, alogr_ref, dskip_ref, nw_ref, tri_ref, trit_ref,
                y_ref, state_ref, xde_ref, ycat_ref):
    @pl.when(pl.program_id(2) == 0)
    def _():
        state_ref[...] = jnp.zeros_like(state_ref)

    bm_bf = b_ref[...]
    cm_bf = c_ref[...]
    bt_bf = bm_bf.astype(F32).T.astype(BF16)

    dtc = jax.nn.softplus(dtc_ref[...] + biasc_ref[...])
    dtr = jax.nn.softplus(dtr_ref[...] + biasr_ref[...])
    dac = dtc * (-LOG2E * jnp.exp(alogc_ref[...]))
    dar = dtr * (-LOG2E * jnp.exp(alogr_ref[...]))
    acs_c = sum(jnp.dot(tri_ref[...], part, preferred_element_type=F32) for part in _split3(dac))
    acs_r = sum(jnp.dot(part, trit_ref[...], preferred_element_type=F32) for part in _split3(dar))
    a_last = acs_c[L - 1:L, :]
    arow = acs_r - jnp.log2(dtr)
    g_end = jnp.log2(dtc) - acs_c + a_last
    dec_chunk = jnp.exp2(a_last)

    cb = lax.dot_general(cm_bf, bm_bf, (((1,), (1,)), ((), ())), preferred_element_type=F32)
    y_off = jnp.dot(cm_bf, state_ref[...].astype(BF16), preferred_element_type=F32)
    cb00 = cb[0:HALF, 0:HALF]
    cb10 = cb[HALF:L, 0:HALF]
    cb11 = cb[HALF:L, HALF:L]
    causal = (lax.broadcasted_iota(jnp.int32, (HALF, HALF), 0)
              >= lax.broadcasted_iota(jnp.int32, (HALF, HALF), 1))
    lane = lax.broadcasted_iota(jnp.int32, (1, LANE), 1)
    first = lane < SSM_HEAD_DIM

    for p in range(PAIRS):
        lanes = slice(p * LANE, (p + 1) * LANE)
        xs_bf = x_ref[:, lanes]
        xs_p = xs_bf.astype(F32)
        a_cols = [jnp.broadcast_to(acs_c[:, r:r + 1], (L, LANE)) for r in (2 * p, 2 * p + 1)]
        a_pair = jnp.where(first, a_cols[0], a_cols[1])
        g_pair = jnp.where(first, g_end[:, 2 * p:2 * p + 1], g_end[:, 2 * p + 1:2 * p + 2])
        xde_ref[:, lanes] = (xs_p * jnp.exp2(g_pair)).astype(BF16)
        y_heads = []
        for h in range(2):
            a_col = a_cols[h]
            a_row = arow[2 * p + h:2 * p + h + 1, :]
            d00 = jnp.exp2(jnp.where(causal, a_col[0:HALF, :] - a_row[:, 0:HALF], -jnp.inf))
            d10 = jnp.exp2(a_col[HALF:L, :] - a_row[:, 0:HALF])
            d11 = jnp.exp2(jnp.where(causal, a_col[HALF:L, :] - a_row[:, HALF:L], -jnp.inf))
            l_top = (cb00 * d00).astype(BF16)
            l_bot = jnp.concatenate([cb10 * d10, cb11 * d11], axis=1).astype(BF16)
            y_heads.append(jnp.concatenate(
                [jnp.dot(l_top, xs_bf[0:HALF, :], preferred_element_type=F32),
                 jnp.dot(l_bot, xs_bf, preferred_element_type=F32)], axis=0))
        y_diag = jnp.where(first, y_heads[0], y_heads[1])
        ycat_ref[:, lanes] = y_diag + y_off[:, lanes] * jnp.exp2(a_pair) + dskip_ref[:, lanes] * xs_p

    upd = jnp.dot(bt_bf, xde_ref[...], preferred_element_type=F32)
    for p in range(PAIRS):
        lanes = slice(p * LANE, (p + 1) * LANE)
        dec_p = jnp.where(first, dec_chunk[:, 2 * p:2 * p + 1], dec_chunk[:, 2 * p + 1:2 * p + 2])
        state_ref[:, lanes] = state_ref[:, lanes] * dec_p + upd[:, lanes]

    y = ycat_ref[...] * _silu(z_ref[...].astype(F32))
    y = y * lax.rsqrt(jnp.mean(y * y, axis=-1, keepdims=True) + SUBLN_EPS)
    y_ref[...] = (y * nw_ref[...]).astype(y_ref.dtype)


def _ssd(xact, proj3, dt_col, dt_row, bias_c, alog_c, bias_r, alog_r, dskip_row, norm_w):
    b, s, _ = proj3.shape
    nc = s // L
    bblk = SSM_D_INNER // SSM_STATE
    cblk = bblk + SSM_GROUPS
    tri = jnp.tril(jnp.ones((L, L), BF16))
    const = lambda shape: pl.BlockSpec(shape, lambda bi, g, c: (0, 0))
    in_specs = [
        pl.BlockSpec((None, L, GW), lambda bi, g, c: (bi, c, g)),
        pl.BlockSpec((None, L, SSM_STATE), lambda bi, g, c: (bi, c, bblk + g)),
        pl.BlockSpec((None, L, SSM_STATE), lambda bi, g, c: (bi, c, cblk + g)),
        pl.BlockSpec((None, L, GW), lambda bi, g, c: (bi, c, g)),
        pl.BlockSpec((None, None, L, HPG), lambda bi, g, c: (bi, g, c, 0)),
        pl.BlockSpec((None, None, HPG, L), lambda bi, g, c: (bi, g, 0, c)),
        pl.BlockSpec((None, 1, HPG), lambda bi, g, c: (g, 0, 0)),
        pl.BlockSpec((None, 1, HPG), lambda bi, g, c: (g, 0, 0)),
        pl.BlockSpec((None, HPG, 1), lambda bi, g, c: (g, 0, 0)),
        pl.BlockSpec((None, HPG, 1), lambda bi, g, c: (g, 0, 0)),
        pl.BlockSpec((1, GW), lambda bi, g, c: (0, g)),
        pl.BlockSpec((1, GW), lambda bi, g, c: (0, g)),
        const((L, L)),
        const((L, L)),
    ]
    return pl.pallas_call(
        _ssd_kernel,
        out_shape=jax.ShapeDtypeStruct((b, s, SSM_D_INNER), BF16),
        grid=(b, SSM_GROUPS, nc),
        in_specs=in_specs,
        out_specs=pl.BlockSpec((None, L, GW), lambda bi, g, c: (bi, c, g)),
        scratch_shapes=[
            pltpu.VMEM((SSM_STATE, GW), F32),
            pltpu.VMEM((L, GW), BF16),
            pltpu.VMEM((L, GW), F32),
        ],
        compiler_params=_cparams(("parallel", "parallel", "arbitrary")),
        name="ssd",
    )(xact, xact, xact, proj3, dt_col, dt_row, bias_c, alog_c, bias_r, alog_r, dskip_row, norm_w, tri, tri.T)


ATT_TQ = 512
ATT_TK = 512
ROPE_ROWS = 1024
ATT_HEADS_PER_PAIR = 2


def _norm_rope(x, w, cos, sin_signed, group_ones):
    ss = jnp.dot((x * x).astype(BF16), group_ones, preferred_element_type=F32)
    xn = x * lax.rsqrt(ss * (1.0 / ATT_HEAD_DIM) + NORM_EPS) * w
    lane = lax.broadcasted_iota(jnp.int32, (1, ATT_PAIR), 1)
    low_half = (lane % ATT_HEAD_DIM) < (ATT_HEAD_DIM // 2)
    half = ATT_HEAD_DIM // 2
    rot = jnp.where(low_half, pltpu.roll(xn, ATT_PAIR - half, 1), pltpu.roll(xn, half, 1))
    return xn * cos + rot * sin_signed


def _fold_lanes(x, op):
    out = x[:, 0:LANE]
    for t in range(1, x.shape[1] // LANE):
        out = op(out, x[:, t * LANE:(t + 1) * LANE])
    return out


def _attn_kernel(q_ref, k_ref, v_ref, cos_ref, sin_ref, qw_ref, kw_ref,
                 lq1_ref, lk1_ref, lq2_ref, lk2_ref, sw_ref, o_ref,
                 krot_ref, qm_ref, s_ref, stat_ref, acc_ref):
    qi = pl.program_id(2)
    ri = lax.broadcasted_iota(jnp.int32, (ATT_PAIR, ATT_PAIR), 0) // ATT_HEAD_DIM
    ci = lax.broadcasted_iota(jnp.int32, (ATT_PAIR, ATT_PAIR), 1) // ATT_HEAD_DIM
    group_ones = (ri == ci).astype(BF16)

    @pl.when(qi == 0)
    def _():
        def body(i, carry):
            rows = pl.ds(pl.multiple_of(i * ROPE_ROWS, ROPE_ROWS), ROPE_ROWS)
            kr = _norm_rope(k_ref[rows, :].astype(F32), kw_ref[...], cos_ref[rows, :], sin_ref[rows, :], group_ones)
            krot_ref[rows, :] = kr.astype(BF16)
            return carry
        lax.fori_loop(0, k_ref.shape[0] // ROPE_ROWS, body, 0)

    qrows = pl.ds(pl.multiple_of(qi * ATT_TQ, ATT_TQ), ATT_TQ)
    q = _norm_rope(q_ref[...].astype(F32), qw_ref[...], cos_ref[qrows, :], sin_ref[qrows, :], group_ones)
    q = q * (LOG2E * ATT_HEAD_DIM ** -0.5)
    lane = lax.broadcasted_iota(jnp.int32, (1, ATT_PAIR), 1)
    qm_ref[0] = jnp.where(lane < ATT_HEAD_DIM, q, 0.0).astype(BF16)
    qm_ref[1] = jnp.where(lane >= ATT_HEAD_DIM, q, 0.0).astype(BF16)

    def kv_rows(j):
        return pl.ds(pl.multiple_of(j * ATT_TK, ATT_TK), ATT_TK)

    def score_block(j, masked):
        kj = krot_ref[kv_rows(j), :]
        for h in range(ATT_HEADS_PER_PAIR):
            s = lax.dot_general(qm_ref[h], kj, (((1,), (1,)), ((), ())), preferred_element_type=F32)
            if masked:
                r = lax.broadcasted_iota(jnp.int32, (ATT_TQ, ATT_TK), 0)
                c = lax.broadcasted_iota(jnp.int32, (ATT_TQ, ATT_TK), 1)
                s = jnp.where(r >= c, s, -jnp.inf)
            s_ref[h, j] = s
            stat_ref[h] = jnp.maximum(stat_ref[h], _fold_lanes(s, jnp.maximum))

    stat_ref[0:ATT_HEADS_PER_PAIR] = jnp.full((ATT_HEADS_PER_PAIR, ATT_TQ, LANE), -jnp.inf, F32)

    def pass1(j, carry):
        score_block(j, False)
        return carry
    lax.fori_loop(0, qi, pass1, 0)
    score_block(qi, True)

    for h in range(ATT_HEADS_PER_PAIR):
        row_max = jnp.max(stat_ref[h], axis=-1, keepdims=True)
        stat_ref[ATT_HEADS_PER_PAIR + h] = jnp.broadcast_to(row_max, (ATT_TQ, LANE))
        stat_ref[h] = jnp.zeros((ATT_TQ, LANE), F32)
    acc_ref[...] = jnp.zeros_like(acc_ref)

    def pass2(j, carry):
        vj = v_ref[kv_rows(j), :]
        for h in range(ATT_HEADS_PER_PAIR):
            m = stat_ref[ATT_HEADS_PER_PAIR + h]
            s = s_ref[h, j]
            p = jnp.concatenate(
                [jnp.exp2(s[:, t * LANE:(t + 1) * LANE] - m) for t in range(ATT_TK // LANE)], axis=1)
            stat_ref[h] = stat_ref[h] + _fold_lanes(p, jnp.add)
            acc_ref[h] = acc_ref[h] + jnp.dot(p.astype(BF16), vj, preferred_element_type=F32)
        return carry
    lax.fori_loop(0, qi + 1, pass2, 0)

    l1 = jnp.sum(stat_ref[0], axis=-1, keepdims=True)
    l2 = jnp.sum(stat_ref[1], axis=-1, keepdims=True)
    lam = (jnp.exp(jnp.sum(lq1_ref[...] * lk1_ref[...], axis=-1, keepdims=True))
           - jnp.exp(jnp.sum(lq2_ref[...] * lk2_ref[...], axis=-1, keepdims=True)) + LAM_INIT)
    o = acc_ref[0] / l1 - lam * (acc_ref[1] / l2)
    o = o * lax.rsqrt(jnp.mean(o * o, axis=-1, keepdims=True) + SUBLN_EPS) * sw_ref[...]
    o_ref[...] = (o * (1.0 - LAM_INIT)).astype(o_ref.dtype)


def _attn(proj3, cos_t, sin_t, qw, kw, lq1, lk1, lq2, lk2, sw):
    b, s, _ = proj3.shape
    qblk = COL_Q // ATT_PAIR
    kblk = COL_K // ATT_PAIR
    vblk = COL_V // ATT_PAIR
    small = lambda width: pl.BlockSpec((1, width), lambda bi, h, qi: (0, 0))
    return pl.pallas_call(
        _attn_kernel,
        out_shape=jax.ShapeDtypeStruct((b, s, ATT_V), BF16),
        grid=(b, ATT_HEADS, s // ATT_TQ),
        in_specs=[
            pl.BlockSpec((None, ATT_TQ, ATT_PAIR), lambda bi, h, qi: (bi, qi, qblk + h)),
            pl.BlockSpec((None, s, ATT_PAIR), lambda bi, h, qi: (bi, 0, kblk + h)),
            pl.BlockSpec((None, s, ATT_PAIR), lambda bi, h, qi: (bi, 0, vblk + h)),
            pl.BlockSpec((s, ATT_PAIR), lambda bi, h, qi: (0, 0)),
            pl.BlockSpec((s, ATT_PAIR), lambda bi, h, qi: (0, 0)),
            small(ATT_PAIR), small(ATT_PAIR),
            small(ATT_HEAD_DIM), small(ATT_HEAD_DIM), small(ATT_HEAD_DIM), small(ATT_HEAD_DIM),
            small(ATT_PAIR),
        ],
        out_specs=pl.BlockSpec((None, ATT_TQ, ATT_PAIR), lambda bi, h, qi: (bi, qi, h)),
        scratch_shapes=[
            pltpu.VMEM((s, ATT_PAIR), BF16),
            pltpu.VMEM((ATT_HEADS_PER_PAIR, ATT_TQ, ATT_PAIR), BF16),
            pltpu.VMEM((ATT_HEADS_PER_PAIR, s // ATT_TK, ATT_TQ, ATT_TK), F32),
            pltpu.VMEM((2 * ATT_HEADS_PER_PAIR, ATT_TQ, LANE), F32),
            pltpu.VMEM((ATT_HEADS_PER_PAIR, ATT_TQ, ATT_PAIR), F32),
        ],
        compiler_params=_cparams(("parallel", "parallel", "arbitrary")),
        name="attn",
    )(proj3, proj3, proj3, cos_t, sin_t, qw, kw, lq1, lk1, lq2, lk2, sw)


MERGE_TM = 512
MERGE_TN = 1024


def _merge_kernel(ys_ref, ws_ref, ya_ref, wa_ref, gs_ref, ga_ref, o_ref):
    bs = jnp.dot(ys_ref[...], ws_ref[...], preferred_element_type=F32)
    ba = jnp.dot(ya_ref[...], wa_ref[...], preferred_element_type=F32)
    gs = jax.nn.sigmoid(gs_ref[...].astype(F32))
    ga = jax.nn.sigmoid(ga_ref[...].astype(F32))
    o_ref[...] = (gs * bs + ga * ba).astype(o_ref.dtype)


def _merge(y_ssm, w_ssm, y_att, w_att, proj):
    m = y_ssm.shape[0]
    gsblk = COL_GS // MERGE_TN
    gablk = COL_GA // MERGE_TN
    return pl.pallas_call(
        _merge_kernel,
        out_shape=jax.ShapeDtypeStruct((m, D_MODEL), BF16),
        grid=(m // MERGE_TM, D_MODEL // MERGE_TN),
        in_specs=[
            pl.BlockSpec((MERGE_TM, SSM_D_INNER), lambda i, j: (i, 0)),
            pl.BlockSpec((SSM_D_INNER, MERGE_TN), lambda i, j: (0, j)),
            pl.BlockSpec((MERGE_TM, ATT_V), lambda i, j: (i, 0)),
            pl.BlockSpec((ATT_V, MERGE_TN), lambda i, j: (0, j)),
            pl.BlockSpec((MERGE_TM, MERGE_TN), lambda i, j: (i, gsblk + j)),
            pl.BlockSpec((MERGE_TM, MERGE_TN), lambda i, j: (i, gablk + j)),
        ],
        out_specs=pl.BlockSpec((MERGE_TM, MERGE_TN), lambda i, j: (i, j)),
        compiler_params=_cparams(("parallel", "parallel")),
        name="merge",
    )(y_ssm, w_ssm, y_att, w_att, proj, proj)


OUTPROJ_TM = 512


def _outproj_kernel(x_ref, m_ref, w_ref, nw_ref, x1_ref, h2_ref):
    x1 = x_ref[...] + jnp.dot(m_ref[...], w_ref[...], preferred_element_type=F32)
    x1_ref[...] = x1
    h2_ref[...] = _rms_rows(x1, nw_ref[...], NORM_EPS).astype(h2_ref.dtype)


def _outproj(x2d, merged, w_out, norm_w):
    m = x2d.shape[0]
    return pl.pallas_call(
        _outproj_kernel,
        out_shape=(jax.ShapeDtypeStruct((m, D_MODEL), F32), jax.ShapeDtypeStruct((m, D_MODEL), BF16)),
        grid=(m // OUTPROJ_TM,),
        in_specs=[
            pl.BlockSpec((OUTPROJ_TM, D_MODEL), lambda i: (i, 0)),
            pl.BlockSpec((OUTPROJ_TM, D_MODEL), lambda i: (i, 0)),
            pl.BlockSpec((D_MODEL, D_MODEL), lambda i: (0, 0)),
            pl.BlockSpec((1, D_MODEL), lambda i: (0, 0)),
        ],
        out_specs=(
            pl.BlockSpec((OUTPROJ_TM, D_MODEL), lambda i: (i, 0)),
            pl.BlockSpec((OUTPROJ_TM, D_MODEL), lambda i: (i, 0)),
        ),
        compiler_params=_cparams(("parallel",)),
        name="outproj",
    )(x2d, merged, w_out, norm_w)


FFN_TM = 1024
FFN_TF = 512
DOWN_TM = 512
DOWN_TN = 1024


def _ffn_up_kernel(h_ref, wg_ref, wu_ref, a_ref):
    h = h_ref[...]
    g = jnp.dot(h, wg_ref[...], preferred_element_type=F32)
    u = jnp.dot(h, wu_ref[...], preferred_element_type=F32)
    a_ref[...] = (g * jax.nn.sigmoid(g) * u).astype(a_ref.dtype)


def _ffn_up(h2, w_gate, w_up):
    m = h2.shape[0]
    return pl.pallas_call(
        _ffn_up_kernel,
        out_shape=jax.ShapeDtypeStruct((m, FFN_HIDDEN), BF16),
        grid=(m // FFN_TM, FFN_HIDDEN // FFN_TF),
        in_specs=[
            pl.BlockSpec((FFN_TM, D_MODEL), lambda i, j: (i, 0)),
            pl.BlockSpec((D_MODEL, FFN_TF), lambda i, j: (0, j)),
            pl.BlockSpec((D_MODEL, FFN_TF), lambda i, j: (0, j)),
        ],
        out_specs=pl.BlockSpec((FFN_TM, FFN_TF), lambda i, j: (i, j)),
        compiler_params=_cparams(("parallel", "parallel")),
        name="ffn_up",
    )(h2, w_gate, w_up)


def _ffn_down_kernel(a_ref, w_ref, x1_ref, o_ref):
    o_ref[...] = x1_ref[...] + jnp.dot(a_ref[...], w_ref[...], preferred_element_type=F32)


def _ffn_down(a, w_down, x1):
    m = a.shape[0]
    return pl.pallas_call(
        _ffn_down_kernel,
        out_shape=jax.ShapeDtypeStruct((m, D_MODEL), F32),
        grid=(m // DOWN_TM, D_MODEL // DOWN_TN),
        in_specs=[
            pl.BlockSpec((DOWN_TM, FFN_HIDDEN), lambda i, j: (i, 0)),
            pl.BlockSpec((FFN_HIDDEN, DOWN_TN), lambda i, j: (0, j)),
            pl.BlockSpec((DOWN_TM, DOWN_TN), lambda i, j: (i, j)),
        ],
        out_specs=pl.BlockSpec((DOWN_TM, DOWN_TN), lambda i, j: (i, j)),
        compiler_params=_cparams(("parallel", "parallel")),
        name="ffn_down",
    )(a, w_down, x1)


def _rope_tables(seq):
    half = ATT_HEAD_DIM // 2
    inv_freq = ROPE_THETA ** (-jnp.arange(half, dtype=F32) / half)
    ang = jnp.arange(seq, dtype=jnp.int32).astype(F32)[:, None] * inv_freq[None, :]
    cos = jnp.cos(ang)
    sin = jnp.sin(ang)
    reps = ATT_PAIR // ATT_HEAD_DIM
    cos_t = jnp.tile(jnp.concatenate([cos, cos], axis=-1), (1, reps))
    sin_t = jnp.tile(jnp.concatenate([-sin, sin], axis=-1), (1, reps))
    return cos_t, sin_t


def _layer(x, mix_norm_w, w_in, conv_w, conv_b, dt_bias, a_log, d_skip, ssm_norm_w, w_ssm_proj,
           q_norm_w, k_norm_w, lambda_q1, lambda_k1, lambda_q2, lambda_k2, subln_w, w_attn_proj,
           w_out, ffn_norm_w, w_ffn_gate, w_ffn_up, w_ffn_down):
    b, s, d = x.shape
    m = b * s
    x2d = x.reshape(m, d)

    dt0 = SSM_D_INNER + SSM_CONV_CH
    dt1 = dt0 + SSM_HEADS
    w_main = jnp.concatenate([w_in[:, :dt0].astype(BF16), w_in[:, dt1:].astype(BF16)], axis=1)
    w_dt = jnp.pad(w_in[:, dt0:dt1], ((0, 0), (0, DT_PAD - SSM_HEADS))).astype(BF16)

    proj, dt_raw = _inproj(x2d, mix_norm_w.reshape(1, d), w_main, w_dt)
    proj3 = proj.reshape(b, s, PROJ_COLS)

    xact = _xconv(proj3, conv_w, conv_b.reshape(1, SSM_CONV_CH))
    dt_g = dt_raw[:, :SSM_HEADS].reshape(b, s, SSM_GROUPS, HPG)
    dt_col = jnp.transpose(dt_g, (0, 2, 1, 3))
    dt_row = jnp.transpose(dt_g, (0, 2, 3, 1))
    per_head_c = lambda v: v.astype(F32).reshape(SSM_GROUPS, 1, HPG)
    per_head_r = lambda v: v.astype(F32).reshape(SSM_GROUPS, HPG, 1)
    dskip_row = jnp.repeat(d_skip.astype(F32), SSM_HEAD_DIM).reshape(1, SSM_D_INNER)
    y_ssm = _ssd(xact, proj3, dt_col, dt_row,
                 per_head_c(dt_bias), per_head_c(a_log), per_head_r(dt_bias), per_head_r(a_log),
                 dskip_row, ssm_norm_w.reshape(1, SSM_D_INNER))

    cos_t, sin_t = _rope_tables(s)
    pair_w = lambda v: jnp.tile(v.astype(F32), ATT_PAIR // ATT_HEAD_DIM).reshape(1, ATT_PAIR)
    lam_w = lambda v: v.astype(F32).reshape(1, ATT_HEAD_DIM)
    y_att = _attn(proj3, cos_t, sin_t, pair_w(q_norm_w), pair_w(k_norm_w),
                  lam_w(lambda_q1), lam_w(lambda_k1), lam_w(lambda_q2), lam_w(lambda_k2),
                  subln_w.astype(F32).reshape(1, ATT_PAIR))

    merged = _merge(y_ssm.reshape(m, SSM_D_INNER), w_ssm_proj.astype(BF16),
                    y_att.reshape(m, ATT_V), w_attn_proj.astype(BF16), proj)
    x1, h2 = _outproj(x2d, merged, w_out.astype(BF16), ffn_norm_w.reshape(1, d))
    a = _ffn_up(h2, w_ffn_gate.astype(BF16), w_ffn_up.astype(BF16))
    out = _ffn_down(a, w_ffn_down.astype(BF16), x1)
    return out.reshape(b, s, d)


def kernel(x, mix_norm_w, w_in, conv_w, conv_b, dt_bias, a_log, d_skip, ssm_norm_w, w_ssm_proj, q_norm_w, k_norm_w, lambda_q1, lambda_k1, lambda_q2, lambda_k2, subln_w, w_attn_proj, w_out, ffn_norm_w, w_ffn_gate, w_ffn_up, w_ffn_down):
    depth = w_in.shape[0]
    assert depth == 1, "LAM_INIT is derived for a single layer"
    layer = lambda v: v[0]
    return _layer(x, *(layer(v) for v in (
        mix_norm_w, w_in, conv_w, conv_b, dt_bias, a_log, d_skip, ssm_norm_w, w_ssm_proj, q_norm_w, k_norm_w,
        lambda_q1, lambda_k1, lambda_q2, lambda_k2, subln_w, w_attn_proj, w_out, ffn_norm_w, w_ffn_gate,
        w_ffn_up, w_ffn_down)))
```

```python
import math

import jax
import jax.numpy as jnp
from jax import lax
from jax.experimental import pallas as pl
from jax.experimental.pallas import tpu as pltpu

F32 = jnp.float32
BF16 = jnp.bfloat16

D_MODEL = 2048
SSM_D_INNER = 2 * D_MODEL
SSM_HEAD_DIM = 64
SSM_HEADS = SSM_D_INNER // SSM_HEAD_DIM
SSM_GROUPS = 8
SSM_HEADS_PER_GROUP = SSM_HEADS // SSM_GROUPS
SSM_GROUP_WIDTH = SSM_D_INNER // SSM_GROUPS
SSM_STATE = 128
SSM_CONV = 4
SSM_CHUNK = 256
SSM_CONV_CH = SSM_D_INNER + 2 * SSM_GROUPS * SSM_STATE
ATT_HEAD_DIM = 64
ATT_HEADS = D_MODEL // (2 * ATT_HEAD_DIM)
ATT_PAIR = 2 * ATT_HEAD_DIM
ATT_QK = 2 * ATT_HEADS * ATT_HEAD_DIM
ATT_V = ATT_HEADS * 2 * ATT_HEAD_DIM
ROPE_THETA = 10000.0
FFN_HIDDEN = (((8 * D_MODEL + 2) // 3 + 255) // 256) * 256
NORM_EPS = 1e-6
SUBLN_EPS = 1e-5
LAM_INIT = 0.8 - 0.6 * math.exp(-0.3 * 0)
LOG2E = math.log2(math.e)

COL_Z = 0
COL_X = COL_Z + SSM_D_INNER
SSM_COLS = COL_X + SSM_CONV_CH
COL_Q = 0
COL_K = COL_Q + ATT_QK
COL_V = COL_K + ATT_QK
COL_GS = COL_V + ATT_V
COL_GA = COL_GS + D_MODEL
ATT_COLS = COL_GA + D_MODEL
LANE = 128
DT_PAD = LANE

VMEM_LIMIT = 56 * 1024 * 1024


def _cparams(sem):
    return pltpu.CompilerParams(dimension_semantics=sem, vmem_limit_bytes=VMEM_LIMIT)


def _silu(u):
    h = 0.5 * u
    return h * jnp.tanh(h) + h


INPROJ_TM = 1024
INPROJ_TN = 1024
NORM_ROWS = 256


def _rms_rows(x, w, eps):
    ms = jnp.mean(x * x, axis=-1, keepdims=True)
    return x * lax.rsqrt(ms + eps) * w


def _norm_tile(x_ref, nw_ref, h_ref):
    def body(i, carry):
        rows = pl.ds(pl.multiple_of(i * NORM_ROWS, NORM_ROWS), NORM_ROWS)
        h_ref[rows, :] = _rms_rows(x_ref[rows, :], nw_ref[...], NORM_EPS).astype(BF16)
        return carry
    lax.fori_loop(0, INPROJ_TM // NORM_ROWS, body, 0)


def _project_tile(h_ref, w_ref, out_ref):
    w = w_ref[...].astype(BF16)
    out_ref[...] = jnp.dot(h_ref[...], w, preferred_element_type=F32).astype(out_ref.dtype)


def _inproj_dt_kernel(x_ref, nw_ref, w_ref, wdt_ref, out_ref, dt_ref, h_ref):
    @pl.when(pl.program_id(1) == 0)
    def _():
        _norm_tile(x_ref, nw_ref, h_ref)
        dt_ref[...] = jnp.dot(h_ref[...], wdt_ref[...], preferred_element_type=F32)

    _project_tile(h_ref, w_ref, out_ref)


def _inproj_kernel(x_ref, nw_ref, w_ref, out_ref, h_ref):
    @pl.when(pl.program_id(1) == 0)
    def _():
        _norm_tile(x_ref, nw_ref, h_ref)

    _project_tile(h_ref, w_ref, out_ref)


def _inproj(x2d, norm_w, w, n_cols, name, w_dt=None):
    m = x2d.shape[0]
    in_specs = [
        pl.BlockSpec((INPROJ_TM, D_MODEL), lambda i, j: (i, 0)),
        pl.BlockSpec((1, D_MODEL), lambda i, j: (0, 0)),
        pl.BlockSpec((D_MODEL, INPROJ_TN), lambda i, j: (0, j)),
    ]
    out_shape = [jax.ShapeDtypeStruct((m, n_cols), BF16)]
    out_specs = [pl.BlockSpec((INPROJ_TM, INPROJ_TN), lambda i, j: (i, j))]
    operands = [x2d, norm_w, w]
    if w_dt is not None:
        in_specs.append(pl.BlockSpec((D_MODEL, DT_PAD), lambda i, j: (0, 0)))
        out_shape.append(jax.ShapeDtypeStruct((m, DT_PAD), F32))
        out_specs.append(pl.BlockSpec((INPROJ_TM, DT_PAD), lambda i, j: (i, 0)))
        operands.append(w_dt)
    return pl.pallas_call(
        _inproj_kernel if w_dt is None else _inproj_dt_kernel,
        out_shape=tuple(out_shape),
        grid=(m // INPROJ_TM, n_cols // INPROJ_TN),
        in_specs=in_specs,
        out_specs=tuple(out_specs),
        scratch_shapes=[pltpu.VMEM((INPROJ_TM, D_MODEL), BF16)],
        compiler_params=_cparams(("parallel", "arbitrary")),
        name=name,
    )(*operands)


L = SSM_CHUNK
TAIL = 16
TAPS_BACK = SSM_CONV - 1
CONV_TW = 2048
CONV_SLAB = 512


def _xconv_kernel(cur_ref, prev_ref, w_ref, b_ref, shift_ref, tshift_ref, o_ref):
    has_prev = pl.program_id(1) > 0
    for s0 in range(0, CONV_TW, CONV_SLAB):
        cols = slice(s0, s0 + CONV_SLAB)
        cur_bf = cur_ref[:, cols]
        prev_bf = jnp.where(has_prev, prev_ref[:, cols], jnp.zeros((TAIL, CONV_SLAB), BF16))
        cur = cur_bf.astype(F32)
        shifted = jnp.dot(shift_ref[...], cur_bf, preferred_element_type=F32)
        carried = jnp.dot(tshift_ref[...], prev_bf, preferred_element_type=F32)
        w = w_ref[:, cols]
        u = b_ref[:, cols] + w[TAPS_BACK:SSM_CONV, :] * cur
        top = jnp.zeros((TAIL, CONV_SLAB), F32)
        for k in range(TAPS_BACK):
            u = u + w[k:k + 1, :] * shifted[k * L:(k + 1) * L, :]
            top = top + w[k:k + 1, :] * carried[k * TAIL:(k + 1) * TAIL, :]
        u = jnp.concatenate([u[0:TAIL, :] + top, u[TAIL:L, :]], axis=0)
        o_ref[:, cols] = _silu(u).astype(o_ref.dtype)


def _xconv(proj3, conv_w, conv_b):
    b, s, _ = proj3.shape
    nc = s // L
    xblk = COL_X // CONV_TW
    tail_per_block = L // TAIL
    shift = jnp.concatenate([jnp.eye(L, L, k=-(TAPS_BACK - k), dtype=BF16) for k in range(TAPS_BACK)], axis=0)
    tshift = jnp.concatenate(
        [jnp.eye(TAIL, TAIL, k=TAIL - (TAPS_BACK - k), dtype=BF16) for k in range(TAPS_BACK)], axis=0)
    return pl.pallas_call(
        _xconv_kernel,
        out_shape=jax.ShapeDtypeStruct((b, s, SSM_CONV_CH), BF16),
        grid=(b, nc, SSM_CONV_CH // CONV_TW),
        in_specs=[
            pl.BlockSpec((None, L, CONV_TW), lambda bi, c, j: (bi, c, xblk + j)),
            pl.BlockSpec((None, TAIL, CONV_TW),
                         lambda bi, c, j: (bi, jnp.maximum(c * tail_per_block - 1, 0), xblk + j)),
            pl.BlockSpec((SSM_CONV, CONV_TW), lambda bi, c, j: (0, j)),
            pl.BlockSpec((1, CONV_TW), lambda bi, c, j: (0, j)),
            pl.BlockSpec((TAPS_BACK * L, L), lambda bi, c, j: (0, 0)),
            pl.BlockSpec((TAPS_BACK * TAIL, TAIL), lambda bi, c, j: (0, 0)),
        ],
        out_specs=pl.BlockSpec((None, L, CONV_TW), lambda bi, c, j: (bi, c, j)),
        compiler_params=_cparams(("parallel", "parallel", "parallel")),
        name="xconv",
    )(proj3, proj3, conv_w, conv_b, shift, tshift)


HALF = L // 2
HPG = SSM_HEADS_PER_GROUP
GW = SSM_GROUP_WIDTH
PAIRS = HPG // 2


def _split3(x):
    hi = x.astype(BF16)
    rest = x - hi.astype(F32)
    mid = rest.astype(BF16)
    lo = (rest - mid.astype(F32)).astype(BF16)
    return hi, mid, lo


def _ssd_kernel(x_ref, b_ref, c_ref, z_ref, dtc_ref, dtr_ref,
                biasc_ref, alogc_ref, biasr_ref, alogr_ref, dskip_ref, nw_ref, tri_ref, trit_ref,
                y_ref, state_ref, xde_ref, ycat_ref):
    @pl.when(pl.program_id(2) == 0)
    def _():
        state_ref[...] = jnp.zeros_like(state_ref)

    bm_bf = b_ref[...]
    cm_bf = c_ref[...]
    bt_bf = bm_bf.astype(F32).T.astype(BF16)

    dtc = jax.nn.softplus(dtc_ref[...] + biasc_ref[...])
    dtr = jax.nn.softplus(dtr_ref[...] + biasr_ref[...])
    dac = dtc * (-LOG2E * jnp.exp(alogc_ref[...]))
    dar = dtr * (-LOG2E * jnp.exp(alogr_ref[...]))
    acs_c = sum(jnp.dot(tri_ref[...], part, preferred_element_type=F32) for part in _split3(dac))
    acs_r = sum(jnp.dot(part, trit_ref[...], preferred_element_type=F32) for part in _split3(dar))
    a_last = acs_c[L - 1:L, :]
    arow = acs_r - jnp.log2(dtr)
    g_end = jnp.log2(dtc) - acs_c + a_last
    dec_chunk = jnp.exp2(a_last)

    cb = lax.dot_general(cm_bf, bm_bf, (((1,), (1,)), ((), ())), preferred_element_type=F32)
    y_off = jnp.dot(cm_bf, state_ref[...].astype(BF16), preferred_element_type=F32)
    cb00 = cb[0:HALF, 0:HALF]
    cb10 = cb[HALF:L, 0:HALF]
    cb11 = cb[HALF:L, HALF:L]
    causal = (lax.broadcasted_iota(jnp.int32, (HALF, HALF), 0)
              >= lax.broadcasted_iota(jnp.int32, (HALF, HALF), 1))
    lane = lax.broadcasted_iota(jnp.int32, (1, LANE), 1)
    first = lane < SSM_HEAD_DIM

    for p in range(PAIRS):
        lanes = slice(p * LANE, (p + 1) * LANE)
        xs_bf = x_ref[:, lanes]
        xs_p = xs_bf.astype(F32)
        a_cols = [jnp.broadcast_to(acs_c[:, r:r + 1], (L, LANE)) for r in (2 * p, 2 * p + 1)]
        a_pair = jnp.where(first, a_cols[0], a_cols[1])
        g_pair = jnp.where(first, g_end[:, 2 * p:2 * p + 1], g_end[:, 2 * p + 1:2 * p + 2])
        xde_ref[:, lanes] = (xs_p * jnp.exp2(g_pair)).astype(BF16)
        y_heads = []
        for h in range(2):
            a_col = a_cols[h]
            a_row = arow[2 * p + h:2 * p + h + 1, :]
            d00 = jnp.exp2(jnp.where(causal, a_col[0:HALF, :] - a_row[:, 0:HALF], -jnp.inf))
            d10 = jnp.exp2(a_col[HALF:L, :] - a_row[:, 0:HALF])
            d11 = jnp.exp2(jnp.where(causal, a_col[HALF:L, :] - a_row[:, HALF:L], -jnp.inf))
            l_top = (cb00 * d00).astype(BF16)
            l_bot = jnp.concatenate([cb10 * d10, cb11 * d11], axis=1).astype(BF16)
            y_heads.append(jnp.concatenate(
                [jnp.dot(l_top, xs_bf[0:HALF, :], preferred_element_type=F32),
                 jnp.dot(l_bot, xs_bf, preferred_element_type=F32)], axis=0))
        y_diag = jnp.where(first, y_heads[0], y_heads[1])
        ycat_ref[:, lanes] = y_diag + y_off[:, lanes] * jnp.exp2(a_pair) + dskip_ref[:, lanes] * xs_p

    upd = jnp.dot(bt_bf, xde_ref[...], preferred_element_type=F32)
    for p in range(PAIRS):
        lanes = slice(p * LANE, (p + 1) * LANE)
        dec_p = jnp.where(first, dec_chunk[:, 2 * p:2 * p + 1], dec_chunk[:, 2 * p + 1:2 * p + 2])
        state_ref[:, lanes] = state_ref[:, lanes] * dec_p + upd[:, lanes]

    y = ycat_ref[...] * _silu(z_ref[...].astype(F32))
    y = y * lax.rsqrt(jnp.mean(y * y, axis=-1, keepdims=True) + SUBLN_EPS)
    y_ref[...] = (y * nw_ref[...]).astype(y_ref.dtype)


def _ssd(xact, proj3, dt_col, dt_row, bias_c, alog_c, bias_r, alog_r, dskip_row, norm_w):
    b, s, _ = proj3.shape
    nc = s // L
    bblk = SSM_D_INNER // SSM_STATE
    cblk = bblk + SSM_GROUPS
    tri = jnp.tril(jnp.ones((L, L), BF16))
    const = lambda shape: pl.BlockSpec(shape, lambda bi, g, c: (0, 0))
    in_specs = [
        pl.BlockSpec((None, L, GW), lambda bi, g, c: (bi, c, g)),
        pl.BlockSpec((None, L, SSM_STATE), lambda bi, g, c: (bi, c, bblk + g)),
        pl.BlockSpec((None, L, SSM_STATE), lambda bi, g, c: (bi, c, cblk + g)),
        pl.BlockSpec((None, L, GW), lambda bi, g, c: (bi, c, g)),
        pl.BlockSpec((None, None, L, HPG), lambda bi, g, c: (bi, g, c, 0)),
        pl.BlockSpec((None, None, HPG, L), lambda bi, g, c: (bi, g, 0, c)),
        pl.BlockSpec((None, 1, HPG), lambda bi, g, c: (g, 0, 0)),
        pl.BlockSpec((None, 1, HPG), lambda bi, g, c: (g, 0, 0)),
        pl.BlockSpec((None, HPG, 1), lambda bi, g, c: (g, 0, 0)),
        pl.BlockSpec((None, HPG, 1), lambda bi, g, c: (g, 0, 0)),
        pl.BlockSpec((1, GW), lambda bi, g, c: (0, g)),
        pl.BlockSpec((1, GW), lambda bi, g, c: (0, g)),
        const((L, L)),
        const((L, L)),
    ]
    return pl.pallas_call(
        _ssd_kernel,
        out_shape=jax.ShapeDtypeStruct((b, s, SSM_D_INNER), BF16),
        grid=(b, SSM_GROUPS, nc),
        in_specs=in_specs,
        out_specs=pl.BlockSpec((None, L, GW), lambda bi, g, c: (bi, c, g)),
        scratch_shapes=[
            pltpu.VMEM((SSM_STATE, GW), F32),
            pltpu.VMEM((L, GW), BF16),
            pltpu.VMEM((L, GW), F32),
        ],
        compiler_params=_cparams(("parallel", "parallel", "arbitrary")),
        name="ssd",
    )(xact, xact, xact, proj3, dt_col, dt_row, bias_c, alog_c, bias_r, alog_r, dskip_row, norm_w, tri, tri.T)


ATT_TQ = 512
ATT_TK = 512
ROPE_ROWS = 1024
ATT_HEADS_PER_PAIR = 2


def _norm_rope(x, w, cos, sin_signed, group_ones):
    ss = jnp.dot((x * x).astype(BF16), group_ones, preferred_element_type=F32)
    xn = x * lax.rsqrt(ss * (1.0 / ATT_HEAD_DIM) + NORM_EPS) * w
    lane = lax.broadcasted_iota(jnp.int32, (1, ATT_PAIR), 1)
    low_half = (lane % ATT_HEAD_DIM) < (ATT_HEAD_DIM // 2)
    half = ATT_HEAD_DIM // 2
    rot = jnp.where(low_half, pltpu.roll(xn, ATT_PAIR - half, 1), pltpu.roll(xn, half, 1))
    return xn * cos + rot * sin_signed


def _fold_lanes(x, op):
    out = x[:, 0:LANE]
    for t in range(1, x.shape[1] // LANE):
        out = op(out, x[:, t * LANE:(t + 1) * LANE])
    return out


def _attn_kernel(q_ref, k_ref, v_ref, cos_ref, sin_ref, qw_ref, kw_ref,
                 lq1_ref, lk1_ref, lq2_ref, lk2_ref, sw_ref, o_ref,
                 krot_ref, qm_ref, s_ref, stat_ref, acc_ref):
    qi = pl.program_id(2)
    ri = lax.broadcasted_iota(jnp.int32, (ATT_PAIR, ATT_PAIR), 0) // ATT_HEAD_DIM
    ci = lax.broadcasted_iota(jnp.int32, (ATT_PAIR, ATT_PAIR), 1) // ATT_HEAD_DIM
    group_ones = (ri == ci).astype(BF16)

    @pl.when(qi == 0)
    def _():
        def body(i, carry):
            rows = pl.ds(pl.multiple_of(i * ROPE_ROWS, ROPE_ROWS), ROPE_ROWS)
            kr = _norm_rope(k_ref[rows, :].astype(F32), kw_ref[...], cos_ref[rows, :], sin_ref[rows, :], group_ones)
            krot_ref[rows, :] = kr.astype(BF16)
            return carry
        lax.fori_loop(0, k_ref.shape[0] // ROPE_ROWS, body, 0)

    qrows = pl.ds(pl.multiple_of(qi * ATT_TQ, ATT_TQ), ATT_TQ)
    q = _norm_rope(q_ref[...].astype(F32), qw_ref[...], cos_ref[qrows, :], sin_ref[qrows, :], group_ones)
    q = q * (LOG2E * ATT_HEAD_DIM ** -0.5)
    lane = lax.broadcasted_iota(jnp.int32, (1, ATT_PAIR), 1)
    qm_ref[0] = jnp.where(lane < ATT_HEAD_DIM, q, 0.0).astype(BF16)
    qm_ref[1] = jnp.where(lane >= ATT_HEAD_DIM, q, 0.0).astype(BF16)

    def kv_rows(j):
        return pl.ds(pl.multiple_of(j * ATT_TK, ATT_TK), ATT_TK)

    def score_block(j, masked):
        kj = krot_ref[kv_rows(j), :]
        for h in range(ATT_HEADS_PER_PAIR):
            s = lax.dot_general(qm_ref[h], kj, (((1,), (1,)), ((), ())), preferred_element_type=F32)
            if masked:
                r = lax.broadcasted_iota(jnp.int32, (ATT_TQ, ATT_TK), 0)
                c = lax.broadcasted_iota(jnp.int32, (ATT_TQ, ATT_TK), 1)
                s = jnp.where(r >= c, s, -jnp.inf)
            s_ref[h, j] = s
            stat_ref[h] = jnp.maximum(stat_ref[h], _fold_lanes(s, jnp.maximum))

    stat_ref[0:ATT_HEADS_PER_PAIR] = jnp.full((ATT_HEADS_PER_PAIR, ATT_TQ, LANE), -jnp.inf, F32)

    def pass1(j, carry):
        score_block(j, False)
        return carry
    lax.fori_loop(0, qi, pass1, 0)
    score_block(qi, True)

    for h in range(ATT_HEADS_PER_PAIR):
        row_max = jnp.max(stat_ref[h], axis=-1, keepdims=True)
        stat_ref[ATT_HEADS_PER_PAIR + h] = jnp.broadcast_to(row_max, (ATT_TQ, LANE))
        stat_ref[h] = jnp.zeros((ATT_TQ, LANE), F32)
    acc_ref[...] = jnp.zeros_like(acc_ref)

    def pass2(j, carry):
        vj = v_ref[kv_rows(j), :]
        for h in range(ATT_HEADS_PER_PAIR):
            m = stat_ref[ATT_HEADS_PER_PAIR + h]
            s = s_ref[h, j]
            p = jnp.concatenate(
                [jnp.exp2(s[:, t * LANE:(t + 1) * LANE] - m) for t in range(ATT_TK // LANE)], axis=1)
            stat_ref[h] = stat_ref[h] + _fold_lanes(p, jnp.add)
            acc_ref[h] = acc_ref[h] + jnp.dot(p.astype(BF16), vj, preferred_element_type=F32)
        return carry
    lax.fori_loop(0, qi + 1, pass2, 0)

    l1 = jnp.sum(stat_ref[0], axis=-1, keepdims=True)
    l2 = jnp.sum(stat_ref[1], axis=-1, keepdims=True)
    lam = (jnp.exp(jnp.sum(lq1_ref[...] * lk1_ref[...], axis=-1, keepdims=True))
           - jnp.exp(jnp.sum(lq2_ref[...] * lk2_ref[...], axis=-1, keepdims=True)) + LAM_INIT)
    o = acc_ref[0] / l1 - lam * (acc_ref[1] / l2)
    o = o * lax.rsqrt(jnp.mean(o * o, axis=-1, keepdims=True) + SUBLN_EPS) * sw_ref[...]
    o_ref[...] = (o * (1.0 - LAM_INIT)).astype(o_ref.dtype)


def _attn(proj3, cos_t, sin_t, qw, kw, lq1, lk1, lq2, lk2, sw):
    b, s, _ = proj3.shape
    qblk = COL_Q // ATT_PAIR
    kblk = COL_K // ATT_PAIR
    vblk = COL_V // ATT_PAIR
    small = lambda width: pl.BlockSpec((1, width), lambda bi, h, qi: (0, 0))
    return pl.pallas_call(
        _attn_kernel,
        out_shape=jax.ShapeDtypeStruct((b, s, ATT_V), BF16),
        grid=(b, ATT_HEADS, s // ATT_TQ),
        in_specs=[
            pl.BlockSpec((None, ATT_TQ, ATT_PAIR), lambda bi, h, qi: (bi, qi, qblk + h)),
            pl.BlockSpec((None, s, ATT_PAIR), lambda bi, h, qi: (bi, 0, kblk + h)),
            pl.BlockSpec((None, s, ATT_PAIR), lambda bi, h, qi: (bi, 0, vblk + h)),
            pl.BlockSpec((s, ATT_PAIR), lambda bi, h, qi: (0, 0)),
            pl.BlockSpec((s, ATT_PAIR), lambda bi, h, qi: (0, 0)),
            small(ATT_PAIR), small(ATT_PAIR),
            small(ATT_HEAD_DIM), small(ATT_HEAD_DIM), small(ATT_HEAD_DIM), small(ATT_HEAD_DIM),
            small(ATT_PAIR),
        ],
        out_specs=pl.BlockSpec((None, ATT_TQ, ATT_PAIR), lambda bi, h, qi: (bi, qi, h)),
        scratch_shapes=[
            pltpu.VMEM((s, ATT_PAIR), BF16),
            pltpu.VMEM((ATT_HEADS_PER_PAIR, ATT_TQ, ATT_PAIR), BF16),
            pltpu.VMEM((ATT_HEADS_PER_PAIR, s // ATT_TK, ATT_TQ, ATT_TK), F32),
            pltpu.VMEM((2 * ATT_HEADS_PER_PAIR, ATT_TQ, LANE), F32),
            pltpu.VMEM((ATT_HEADS_PER_PAIR, ATT_TQ, ATT_PAIR), F32),
        ],
        compiler_params=_cparams(("parallel", "parallel", "arbitrary")),
        name="attn",
    )(proj3, proj3, proj3, cos_t, sin_t, qw, kw, lq1, lk1, lq2, lk2, sw)


MERGE_TM = 512
MERGE_TN = 1024


def _merge_kernel(ys_ref, ws_ref, ya_ref, wa_ref, gs_ref, ga_ref, o_ref):
    bs = jnp.dot(ys_ref[...], ws_ref[...], preferred_element_type=F32)
    ba = jnp.dot(ya_ref[...], wa_ref[...], preferred_element_type=F32)
    gs = jax.nn.sigmoid(gs_ref[...].astype(F32))
    ga = jax.nn.sigmoid(ga_ref[...].astype(F32))
    o_ref[...] = (gs * bs + ga * ba).astype(o_ref.dtype)


def _merge(y_ssm, w_ssm, y_att, w_att, proj):
    m = y_ssm.shape[0]
    gsblk = COL_GS // MERGE_TN
    gablk = COL_GA // MERGE_TN
    return pl.pallas_call(
        _merge_kernel,
        out_shape=jax.ShapeDtypeStruct((m, D_MODEL), BF16),
        grid=(m // MERGE_TM, D_MODEL // MERGE_TN),
        in_specs=[
            pl.BlockSpec((MERGE_TM, SSM_D_INNER), lambda i, j: (i, 0)),
            pl.BlockSpec((SSM_D_INNER, MERGE_TN), lambda i, j: (0, j)),
            pl.BlockSpec((MERGE_TM, ATT_V), lambda i, j: (i, 0)),
            pl.BlockSpec((ATT_V, MERGE_TN), lambda i, j: (0, j)),
            pl.BlockSpec((MERGE_TM, MERGE_TN), lambda i, j: (i, gsblk + j)),
            pl.BlockSpec((MERGE_TM, MERGE_TN), lambda i, j: (i, gablk + j)),
        ],
        out_specs=pl.BlockSpec((MERGE_TM, MERGE_TN), lambda i, j: (i, j)),
        compiler_params=_cparams(("parallel", "parallel")),
        name="merge",
    )(y_ssm, w_ssm, y_att, w_att, proj, proj)


OUTPROJ_TM = 512


def _outproj_kernel(x_ref, m_ref, w_ref, nw_ref, x1_ref, h2_ref):
    x1 = x_ref[...] + jnp.dot(m_ref[...], w_ref[...], preferred_element_type=F32)
    x1_ref[...] = x1
    h2_ref[...] = _rms_rows(x1, nw_ref[...], NORM_EPS).astype(h2_ref.dtype)


def _outproj(x2d, merged, w_out, norm_w):
    m = x2d.shape[0]
    return pl.pallas_call(
        _outproj_kernel,
        out_shape=(jax.ShapeDtypeStruct((m, D_MODEL), F32), jax.ShapeDtypeStruct((m, D_MODEL), BF16)),
        grid=(m // OUTPROJ_TM,),
        in_specs=[
            pl.BlockSpec((OUTPROJ_TM, D_MODEL), lambda i: (i, 0)),
            pl.BlockSpec((OUTPROJ_TM, D_MODEL), lambda i: (i, 0)),
            pl.BlockSpec((D_MODEL, D_MODEL), lambda i: (0, 0)),
            pl.BlockSpec((1, D_MODEL), lambda i: (0, 0)),
        ],
        out_specs=(
            pl.BlockSpec((OUTPROJ_TM, D_MODEL), lambda i: (i, 0)),
            pl.BlockSpec((OUTPROJ_TM, D_MODEL), lambda i: (i, 0)),
        ),
        compiler_params=_cparams(("parallel",)),
        name="outproj",
    )(x2d, merged, w_out, norm_w)


FFN_TM = 1024
FFN_TF = 512
DOWN_TM = 512
DOWN_TN = 1024


def _ffn_up_kernel(h_ref, wg_ref, wu_ref, a_ref):
    h = h_ref[...]
    g = jnp.dot(h, wg_ref[...], preferred_element_type=F32)
    u = jnp.dot(h, wu_ref[...], preferred_element_type=F32)
    a_ref[...] = (g * jax.nn.sigmoid(g) * u).astype(a_ref.dtype)


def _ffn_up(h2, w_gate, w_up):
    m = h2.shape[0]
    return pl.pallas_call(
        _ffn_up_kernel,
        out_shape=jax.ShapeDtypeStruct((m, FFN_HIDDEN), BF16),
        grid=(m // FFN_TM, FFN_HIDDEN // FFN_TF),
        in_specs=[
            pl.BlockSpec((FFN_TM, D_MODEL), lambda i, j: (i, 0)),
            pl.BlockSpec((D_MODEL, FFN_TF), lambda i, j: (0, j)),
            pl.BlockSpec((D_MODEL, FFN_TF), lambda i, j: (0, j)),
        ],
        out_specs=pl.BlockSpec((FFN_TM, FFN_TF), lambda i, j: (i, j)),
        compiler_params=_cparams(("parallel", "parallel")),
        name="ffn_up",
    )(h2, w_gate, w_up)


def _ffn_down_kernel(a_ref, w_ref, x1_ref, o_ref):
    o_ref[...] = x1_ref[...] + jnp.dot(a_ref[...], w_ref[...], preferred_element_type=F32)


def _ffn_down(a, w_down, x1):
    m = a.shape[0]
    return pl.pallas_call(
        _ffn_down_kernel,
        out_shape=jax.ShapeDtypeStruct((m, D_MODEL), F32),
        grid=(m // DOWN_TM, D_MODEL // DOWN_TN),
        in_specs=[
            pl.BlockSpec((DOWN_TM, FFN_HIDDEN), lambda i, j: (i, 0)),
            pl.BlockSpec((FFN_HIDDEN, DOWN_TN), lambda i, j: (0, j)),
            pl.BlockSpec((DOWN_TM, DOWN_TN), lambda i, j: (i, j)),
        ],
        out_specs=pl.BlockSpec((DOWN_TM, DOWN_TN), lambda i, j: (i, j)),
        compiler_params=_cparams(("parallel", "parallel")),
        name="ffn_down",
    )(a, w_down, x1)


def _rope_tables(seq):
    half = ATT_HEAD_DIM // 2
    inv_freq = ROPE_THETA ** (-jnp.arange(half, dtype=F32) / half)
    ang = jnp.arange(seq, dtype=jnp.int32).astype(F32)[:, None] * inv_freq[None, :]
    cos = jnp.cos(ang)
    sin = jnp.sin(ang)
    reps = ATT_PAIR // ATT_HEAD_DIM
    cos_t = jnp.tile(jnp.concatenate([cos, cos], axis=-1), (1, reps))
    sin_t = jnp.tile(jnp.concatenate([-sin, sin], axis=-1), (1, reps))
    return cos_t, sin_t


def _layer(x, mix_norm_w, w_in, conv_w, conv_b, dt_bias, a_log, d_skip, ssm_norm_w, w_ssm_proj,
           q_norm_w, k_norm_w, lambda_q1, lambda_k1, lambda_q2, lambda_k2, subln_w, w_attn_proj,
           w_out, ffn_norm_w, w_ffn_gate, w_ffn_up, w_ffn_down):
    b, s, d = x.shape
    m = b * s
    x2d = x.reshape(m, d)

    dt0 = SSM_COLS
    dt1 = dt0 + SSM_HEADS
    norm_w = mix_norm_w.reshape(1, d)
    w_dt = jnp.pad(w_in[:, dt0:dt1], ((0, 0), (0, DT_PAD - SSM_HEADS))).astype(BF16)
    proj_ssm, dt_raw = _inproj(x2d, norm_w, w_in, SSM_COLS, "inproj_ssm", w_dt=w_dt)
    (proj_att,) = _inproj(x2d, norm_w, w_in[:, dt1:].astype(BF16), ATT_COLS, "inproj_att")
    proj3 = proj_ssm.reshape(b, s, SSM_COLS)
    proj_att3 = proj_att.reshape(b, s, ATT_COLS)

    xact = _xconv(proj3, conv_w, conv_b.reshape(1, SSM_CONV_CH))
    dt_g = dt_raw[:, :SSM_HEADS].reshape(b, s, SSM_GROUPS, HPG)
    dt_col = jnp.transpose(dt_g, (0, 2, 1, 3))
    dt_row = jnp.transpose(dt_g, (0, 2, 3, 1))
    per_head_c = lambda v: v.astype(F32).reshape(SSM_GROUPS, 1, HPG)
    per_head_r = lambda v: v.astype(F32).reshape(SSM_GROUPS, HPG, 1)
    dskip_row = jnp.repeat(d_skip.astype(F32), SSM_HEAD_DIM).reshape(1, SSM_D_INNER)
    y_ssm = _ssd(xact, proj3, dt_col, dt_row,
                 per_head_c(dt_bias), per_head_c(a_log), per_head_r(dt_bias), per_head_r(a_log),
                 dskip_row, ssm_norm_w.reshape(1, SSM_D_INNER))

    cos_t, sin_t = _rope_tables(s)
    pair_w = lambda v: jnp.tile(v.astype(F32), ATT_PAIR // ATT_HEAD_DIM).reshape(1, ATT_PAIR)
    lam_w = lambda v: v.astype(F32).reshape(1, ATT_HEAD_DIM)
    y_att = _attn(proj_att3, cos_t, sin_t, pair_w(q_norm_w), pair_w(k_norm_w),
                  lam_w(lambda_q1), lam_w(lambda_k1), lam_w(lambda_q2), lam_w(lambda_k2),
                  subln_w.astype(F32).reshape(1, ATT_PAIR))

    merged = _merge(y_ssm.reshape(m, SSM_D_INNER), w_ssm_proj.astype(BF16),
                    y_att.reshape(m, ATT_V), w_attn_proj.astype(BF16), proj_att)
    x1, h2 = _outproj(x2d, merged, w_out.astype(BF16), ffn_norm_w.reshape(1, d))
    a = _ffn_up(h2, w_ffn_gate.astype(BF16), w_ffn_up.astype(BF16))
    out = _ffn_down(a, w_ffn_down.astype(BF16), x1)
    return out.reshape(b, s, d)


def kernel(x, mix_norm_w, w_in, conv_w, conv_b, dt_bias, a_log, d_skip, ssm_norm_w, w_ssm_proj, q_norm_w, k_norm_w, lambda_q1, lambda_k1, lambda_q2, lambda_k2, subln_w, w_attn_proj, w_out, ffn_norm_w, w_ffn_gate, w_ffn_up, w_ffn_down):
    depth = w_in.shape[0]
    assert depth == 1, "LAM_INIT is derived for a single layer"
    layer = lambda v: v[0]
    return _layer(x, *(layer(v) for v in (
        mix_norm_w, w_in, conv_w, conv_b, dt_bias, a_log, d_skip, ssm_norm_w, w_ssm_proj, q_norm_w, k_norm_w,
        lambda_q1, lambda_k1, lambda_q2, lambda_k2, subln_w, w_attn_proj, w_out, ffn_norm_w, w_ffn_gate,
        w_ffn_up, w_ffn_down)))
```

```python
import math

import jax
import jax.numpy as jnp
from jax import lax
from jax.experimental import pallas as pl
from jax.experimental.pallas import tpu as pltpu

F32 = jnp.float32
BF16 = jnp.bfloat16

D_MODEL = 2048
SSM_D_INNER = 2 * D_MODEL
SSM_HEAD_DIM = 64
SSM_HEADS = SSM_D_INNER // SSM_HEAD_DIM
SSM_GROUPS = 8
SSM_HEADS_PER_GROUP = SSM_HEADS // SSM_GROUPS
SSM_GROUP_WIDTH = SSM_D_INNER // SSM_GROUPS
SSM_STATE = 128
SSM_CONV = 4
SSM_CHUNK = 256
SSM_CONV_CH = SSM_D_INNER + 2 * SSM_GROUPS * SSM_STATE
ATT_HEAD_DIM = 64
ATT_HEADS = D_MODEL // (2 * ATT_HEAD_DIM)
ATT_PAIR = 2 * ATT_HEAD_DIM
ATT_QK = 2 * ATT_HEADS * ATT_HEAD_DIM
ATT_V = ATT_HEADS * 2 * ATT_HEAD_DIM
ROPE_THETA = 10000.0
FFN_HIDDEN = (((8 * D_MODEL + 2) // 3 + 255) // 256) * 256
NORM_EPS = 1e-6
SUBLN_EPS = 1e-5
LAM_INIT = 0.8 - 0.6 * math.exp(-0.3 * 0)
LOG2E = math.log2(math.e)

COL_Z = 0
COL_X = COL_Z + SSM_D_INNER
SSM_COLS = COL_X + SSM_CONV_CH
COL_Q = 0
COL_K = COL_Q + ATT_QK
COL_V = COL_K + ATT_QK
COL_GS = COL_V + ATT_V
COL_GA = COL_GS + D_MODEL
ATT_COLS = COL_GA + D_MODEL
LANE = 128
SUBLANES = 8
DT_PAD = LANE

VMEM_LIMIT = 56 * 1024 * 1024


def _cparams(sem):
    return pltpu.CompilerParams(dimension_semantics=sem, vmem_limit_bytes=VMEM_LIMIT)


def _silu(u):
    h = 0.5 * u
    return h * jnp.tanh(h) + h


INPROJ_TM = 1024
INPROJ_TN = 1024
NORM_ROWS = 256


def _rms_rows(x, w, eps):
    ms = jnp.mean(x * x, axis=-1, keepdims=True)
    return x * lax.rsqrt(ms + eps) * w


def _norm_tile(x_ref, nw_ref, h_ref):
    def body(i, carry):
        rows = pl.ds(pl.multiple_of(i * NORM_ROWS, NORM_ROWS), NORM_ROWS)
        h_ref[rows, :] = _rms_rows(x_ref[rows, :], nw_ref[...], NORM_EPS).astype(BF16)
        return carry
    lax.fori_loop(0, INPROJ_TM // NORM_ROWS, body, 0)


def _project_rows(h_ref, wt_ref):
    wt = wt_ref[...].astype(BF16)
    return lax.dot_general(h_ref[...], wt, (((1,), (1,)), ((), ())), preferred_element_type=F32)


def _inproj_dt_kernel(x_ref, nw_ref, wt_ref, wdt_ref, out_ref, dt_ref, h_ref):
    @pl.when(pl.program_id(1) == 0)
    def _():
        _norm_tile(x_ref, nw_ref, h_ref)
        dt_ref[...] = _project_rows(h_ref, wdt_ref)

    out_ref[...] = _project_rows(h_ref, wt_ref).astype(out_ref.dtype)


def _inproj_kernel(x_ref, nw_ref, wt_ref, out_ref, h_ref):
    @pl.when(pl.program_id(1) == 0)
    def _():
        _norm_tile(x_ref, nw_ref, h_ref)

    out_ref[...] = _project_rows(h_ref, wt_ref).astype(out_ref.dtype)


def _inproj(x2d, norm_w, w_t, row0, n_cols, name, dt_row0=None):
    m = x2d.shape[0]
    in_specs = [
        pl.BlockSpec((INPROJ_TM, D_MODEL), lambda i, j: (i, 0)),
        pl.BlockSpec((1, D_MODEL), lambda i, j: (0, 0)),
        pl.BlockSpec((pl.Element(INPROJ_TN), pl.Element(D_MODEL)),
                     lambda i, j: (pl.multiple_of(row0 + j * INPROJ_TN, SUBLANES), 0)),
    ]
    out_shape = [jax.ShapeDtypeStruct((m, n_cols), BF16)]
    out_specs = [pl.BlockSpec((INPROJ_TM, INPROJ_TN), lambda i, j: (i, j))]
    operands = [x2d, norm_w, w_t]
    if dt_row0 is not None:
        in_specs.append(pl.BlockSpec((pl.Element(DT_PAD), pl.Element(D_MODEL)), lambda i, j: (dt_row0, 0)))
        out_shape.append(jax.ShapeDtypeStruct((m, DT_PAD), F32))
        out_specs.append(pl.BlockSpec((INPROJ_TM, DT_PAD), lambda i, j: (i, 0)))
        operands.append(w_t)
    return pl.pallas_call(
        _inproj_kernel if dt_row0 is None else _inproj_dt_kernel,
        out_shape=tuple(out_shape),
        grid=(m // INPROJ_TM, n_cols // INPROJ_TN),
        in_specs=in_specs,
        out_specs=tuple(out_specs),
        scratch_shapes=[pltpu.VMEM((INPROJ_TM, D_MODEL), BF16)],
        compiler_params=_cparams(("parallel", "arbitrary")),
        name=name,
    )(*operands)


L = SSM_CHUNK
TAIL = 16
TAPS_BACK = SSM_CONV - 1
CONV_TW = 2048
CONV_SLAB = 512


def _xconv_kernel(cur_ref, prev_ref, w_ref, b_ref, shift_ref, tshift_ref, o_ref):
    has_prev = pl.program_id(1) > 0
    for s0 in range(0, CONV_TW, CONV_SLAB):
        cols = slice(s0, s0 + CONV_SLAB)
        cur_bf = cur_ref[:, cols]
        prev_bf = jnp.where(has_prev, prev_ref[:, cols], jnp.zeros((TAIL, CONV_SLAB), BF16))
        cur = cur_bf.astype(F32)
        shifted = jnp.dot(shift_ref[...], cur_bf, preferred_element_type=F32)
        carried = jnp.dot(tshift_ref[...], prev_bf, preferred_element_type=F32)
        w = w_ref[:, cols]
        u = b_ref[:, cols] + w[TAPS_BACK:SSM_CONV, :] * cur
        top = jnp.zeros((TAIL, CONV_SLAB), F32)
        for k in range(TAPS_BACK):
            u = u + w[k:k + 1, :] * shifted[k * L:(k + 1) * L, :]
            top = top + w[k:k + 1, :] * carried[k * TAIL:(k + 1) * TAIL, :]
        u = jnp.concatenate([u[0:TAIL, :] + top, u[TAIL:L, :]], axis=0)
        o_ref[:, cols] = _silu(u).astype(o_ref.dtype)


def _xconv(proj3, conv_w, conv_b):
    b, s, _ = proj3.shape
    nc = s // L
    xblk = COL_X // CONV_TW
    tail_per_block = L // TAIL
    shift = jnp.concatenate([jnp.eye(L, L, k=-(TAPS_BACK - k), dtype=BF16) for k in range(TAPS_BACK)], axis=0)
    tshift = jnp.concatenate(
        [jnp.eye(TAIL, TAIL, k=TAIL - (TAPS_BACK - k), dtype=BF16) for k in range(TAPS_BACK)], axis=0)
    return pl.pallas_call(
        _xconv_kernel,
        out_shape=jax.ShapeDtypeStruct((b, s, SSM_CONV_CH), BF16),
        grid=(b, nc, SSM_CONV_CH // CONV_TW),
        in_specs=[
            pl.BlockSpec((None, L, CONV_TW), lambda bi, c, j: (bi, c, xblk + j)),
            pl.BlockSpec((None, TAIL, CONV_TW),
                         lambda bi, c, j: (bi, jnp.maximum(c * tail_per_block - 1, 0), xblk + j)),
            pl.BlockSpec((SSM_CONV, CONV_TW), lambda bi, c, j: (0, j)),
            pl.BlockSpec((1, CONV_TW), lambda bi, c, j: (0, j)),
            pl.BlockSpec((TAPS_BACK * L, L), lambda bi, c, j: (0, 0)),
            pl.BlockSpec((TAPS_BACK * TAIL, TAIL), lambda bi, c, j: (0, 0)),
        ],
        out_specs=pl.BlockSpec((None, L, CONV_TW), lambda bi, c, j: (bi, c, j)),
        compiler_params=_cparams(("parallel", "parallel", "parallel")),
        name="xconv",
    )(proj3, proj3, conv_w, conv_b, shift, tshift)


HALF = L // 2
HPG = SSM_HEADS_PER_GROUP
GW = SSM_GROUP_WIDTH
PAIRS = HPG // 2


def _split3(x):
    hi = x.astype(BF16)
    rest = x - hi.astype(F32)
    mid = rest.astype(BF16)
    lo = (rest - mid.astype(F32)).astype(BF16)
    return hi, mid, lo


def _ssd_kernel(x_ref, b_ref, c_ref, z_ref, dtc_ref, dtr_ref,
                biasc_ref, alogc_ref, biasr_ref, alogr_ref, dskip_ref, nw_ref, tri_ref, trit_ref,
                y_ref, state_ref, xde_ref, ycat_ref):
    @pl.when(pl.program_id(2) == 0)
    def _():
        state_ref[...] = jnp.zeros_like(state_ref)

    bm_bf = b_ref[...]
    cm_bf = c_ref[...]
    bt_bf = bm_bf.astype(F32).T.astype(BF16)

    dtc = jax.nn.softplus(dtc_ref[...] + biasc_ref[...])
    dtr = jax.nn.softplus(dtr_ref[...] + biasr_ref[...])
    dac = dtc * (-LOG2E * jnp.exp(alogc_ref[...]))
    dar = dtr * (-LOG2E * jnp.exp(alogr_ref[...]))
    acs_c = sum(jnp.dot(tri_ref[...], part, preferred_element_type=F32) for part in _split3(dac))
    acs_r = sum(jnp.dot(part, trit_ref[...], preferred_element_type=F32) for part in _split3(dar))
    a_last = acs_c[L - 1:L, :]
    arow = acs_r - jnp.log2(dtr)
    g_end = jnp.log2(dtc) - acs_c + a_last
    dec_chunk = jnp.exp2(a_last)

    cb = lax.dot_general(cm_bf, bm_bf, (((1,), (1,)), ((), ())), preferred_element_type=F32)
    y_off = jnp.dot(cm_bf, state_ref[...].astype(BF16), preferred_element_type=F32)
    cb00 = cb[0:HALF, 0:HALF]
    cb10 = cb[HALF:L, 0:HALF]
    cb11 = cb[HALF:L, HALF:L]
    causal = (lax.broadcasted_iota(jnp.int32, (HALF, HALF), 0)
              >= lax.broadcasted_iota(jnp.int32, (HALF, HALF), 1))
    lane = lax.broadcasted_iota(jnp.int32, (1, LANE), 1)
    first = lane < SSM_HEAD_DIM

    for p in range(PAIRS):
        lanes = slice(p * LANE, (p + 1) * LANE)
        xs_bf = x_ref[:, lanes]
        xs_p = xs_bf.astype(F32)
        a_cols = [jnp.broadcast_to(acs_c[:, r:r + 1], (L, LANE)) for r in (2 * p, 2 * p + 1)]
        a_pair = jnp.where(first, a_cols[0], a_cols[1])
        g_pair = jnp.where(first, g_end[:, 2 * p:2 * p + 1], g_end[:, 2 * p + 1:2 * p + 2])
        xde_ref[:, lanes] = (xs_p * jnp.exp2(g_pair)).astype(BF16)
        y_heads = []
        for h in range(2):
            a_col = a_cols[h]
            a_row = arow[2 * p + h:2 * p + h + 1, :]
            d00 = jnp.exp2(jnp.where(causal, a_col[0:HALF, :] - a_row[:, 0:HALF], -jnp.inf))
            d10 = jnp.exp2(a_col[HALF:L, :] - a_row[:, 0:HALF])
            d11 = jnp.exp2(jnp.where(causal, a_col[HALF:L, :] - a_row[:, HALF:L], -jnp.inf))
            l_top = (cb00 * d00).astype(BF16)
            l_bot = jnp.concatenate([cb10 * d10, cb11 * d11], axis=1).astype(BF16)
            y_heads.append(jnp.concatenate(
                [jnp.dot(l_top, xs_bf[0:HALF, :], preferred_element_type=F32),
                 jnp.dot(l_bot, xs_bf, preferred_element_type=F32)], axis=0))
        y_diag = jnp.where(first, y_heads[0], y_heads[1])
        ycat_ref[:, lanes] = y_diag + y_off[:, lanes] * jnp.exp2(a_pair) + dskip_ref[:, lanes] * xs_p

    upd = jnp.dot(bt_bf, xde_ref[...], preferred_element_type=F32)
    for p in range(PAIRS):
        lanes = slice(p * LANE, (p + 1) * LANE)
        dec_p = jnp.where(first, dec_chunk[:, 2 * p:2 * p + 1], dec_chunk[:, 2 * p + 1:2 * p + 2])
        state_ref[:, lanes] = state_ref[:, lanes] * dec_p + upd[:, lanes]

    y = ycat_ref[...] * _silu(z_ref[...].astype(F32))
    y = y * lax.rsqrt(jnp.mean(y * y, axis=-1, keepdims=True) + SUBLN_EPS)
    y_ref[...] = (y * nw_ref[...]).astype(y_ref.dtype)


def _ssd(xact, proj3, dt_col, dt_row, bias_c, alog_c, bias_r, alog_r, dskip_row, norm_w):
    b, s, _ = proj3.shape
    nc = s // L
    bblk = SSM_D_INNER // SSM_STATE
    cblk = bblk + SSM_GROUPS
    tri = jnp.tril(jnp.ones((L, L), BF16))
    const = lambda shape: pl.BlockSpec(shape, lambda bi, g, c: (0, 0))
    in_specs = [
        pl.BlockSpec((None, L, GW), lambda bi, g, c: (bi, c, g)),
        pl.BlockSpec((None, L, SSM_STATE), lambda bi, g, c: (bi, c, bblk + g)),
        pl.BlockSpec((None, L, SSM_STATE), lambda bi, g, c: (bi, c, cblk + g)),
        pl.BlockSpec((None, L, GW), lambda bi, g, c: (bi, c, g)),
        pl.BlockSpec((None, None, L, HPG), lambda bi, g, c: (bi, g, c, 0)),
        pl.BlockSpec((None, None, HPG, L), lambda bi, g, c: (bi, g, 0, c)),
        pl.BlockSpec((None, 1, HPG), lambda bi, g, c: (g, 0, 0)),
        pl.BlockSpec((None, 1, HPG), lambda bi, g, c: (g, 0, 0)),
        pl.BlockSpec((None, HPG, 1), lambda bi, g, c: (g, 0, 0)),
        pl.BlockSpec((None, HPG, 1), lambda bi, g, c: (g, 0, 0)),
        pl.BlockSpec((1, GW), lambda bi, g, c: (0, g)),
        pl.BlockSpec((1, GW), lambda bi, g, c: (0, g)),
        const((L, L)),
        const((L, L)),
    ]
    return pl.pallas_call(
        _ssd_kernel,
        out_shape=jax.ShapeDtypeStruct((b, s, SSM_D_INNER), BF16),
        grid=(b, SSM_GROUPS, nc),
        in_specs=in_specs,
        out_specs=pl.BlockSpec((None, L, GW), lambda bi, g, c: (bi, c, g)),
        scratch_shapes=[
            pltpu.VMEM((SSM_STATE, GW), F32),
            pltpu.VMEM((L, GW), BF16),
            pltpu.VMEM((L, GW), F32),
        ],
        compiler_params=_cparams(("parallel", "parallel", "arbitrary")),
        name="ssd",
    )(xact, xact, xact, proj3, dt_col, dt_row, bias_c, alog_c, bias_r, alog_r, dskip_row, norm_w, tri, tri.T)


ATT_TQ = 512
ATT_TK = 512
ROPE_ROWS = 1024
ATT_HEADS_PER_PAIR = 2


def _norm_rope(x, w, cos, sin_signed, group_ones):
    ss = jnp.dot((x * x).astype(BF16), group_ones, preferred_element_type=F32)
    xn = x * lax.rsqrt(ss * (1.0 / ATT_HEAD_DIM) + NORM_EPS) * w
    lane = lax.broadcasted_iota(jnp.int32, (1, ATT_PAIR), 1)
    low_half = (lane % ATT_HEAD_DIM) < (ATT_HEAD_DIM // 2)
    half = ATT_HEAD_DIM // 2
    rot = jnp.where(low_half, pltpu.roll(xn, ATT_PAIR - half, 1), pltpu.roll(xn, half, 1))
    return xn * cos + rot * sin_signed


def _fold_lanes(x, op):
    out = x[:, 0:LANE]
    for t in range(1, x.shape[1] // LANE):
        out = op(out, x[:, t * LANE:(t + 1) * LANE])
    return out


def _attn_kernel(q_ref, k_ref, v_ref, cos_ref, sin_ref, qw_ref, kw_ref,
                 lq1_ref, lk1_ref, lq2_ref, lk2_ref, sw_ref, o_ref,
                 krot_ref, qm_ref, s_ref, stat_ref, acc_ref):
    qi = pl.program_id(2)
    ri = lax.broadcasted_iota(jnp.int32, (ATT_PAIR, ATT_PAIR), 0) // ATT_HEAD_DIM
    ci = lax.broadcasted_iota(jnp.int32, (ATT_PAIR, ATT_PAIR), 1) // ATT_HEAD_DIM
    group_ones = (ri == ci).astype(BF16)

    @pl.when(qi == 0)
    def _():
        def body(i, carry):
            rows = pl.ds(pl.multiple_of(i * ROPE_ROWS, ROPE_ROWS), ROPE_ROWS)
            kr = _norm_rope(k_ref[rows, :].astype(F32), kw_ref[...], cos_ref[rows, :], sin_ref[rows, :], group_ones)
            krot_ref[rows, :] = kr.astype(BF16)
            return carry
        lax.fori_loop(0, k_ref.shape[0] // ROPE_ROWS, body, 0)

    qrows = pl.ds(pl.multiple_of(qi * ATT_TQ, ATT_TQ), ATT_TQ)
    q = _norm_rope(q_ref[...].astype(F32), qw_ref[...], cos_ref[qrows, :], sin_ref[qrows, :], group_ones)
    q = q * (LOG2E * ATT_HEAD_DIM ** -0.5)
    lane = lax.broadcasted_iota(jnp.int32, (1, ATT_PAIR), 1)
    qm_ref[0] = jnp.where(lane < ATT_HEAD_DIM, q, 0.0).astype(BF16)
    qm_ref[1] = jnp.where(lane >= ATT_HEAD_DIM, q, 0.0).astype(BF16)

    def kv_rows(j):
        return pl.ds(pl.multiple_of(j * ATT_TK, ATT_TK), ATT_TK)

    def score_block(j, masked):
        kj = krot_ref[kv_rows(j), :]
        for h in range(ATT_HEADS_PER_PAIR):
            s = lax.dot_general(qm_ref[h], kj, (((1,), (1,)), ((), ())), preferred_element_type=F32)
            if masked:
                r = lax.broadcasted_iota(jnp.int32, (ATT_TQ, ATT_TK), 0)
                c = lax.broadcasted_iota(jnp.int32, (ATT_TQ, ATT_TK), 1)
                s = jnp.where(r >= c, s, -jnp.inf)
            s_ref[h, j] = s
            stat_ref[h] = jnp.maximum(stat_ref[h], _fold_lanes(s, jnp.maximum))

    stat_ref[0:ATT_HEADS_PER_PAIR] = jnp.full((ATT_HEADS_PER_PAIR, ATT_TQ, LANE), -jnp.inf, F32)

    def pass1(j, carry):
        score_block(j, False)
        return carry
    lax.fori_loop(0, qi, pass1, 0)
    score_block(qi, True)

    for h in range(ATT_HEADS_PER_PAIR):
        row_max = jnp.max(stat_ref[h], axis=-1, keepdims=True)
        stat_ref[ATT_HEADS_PER_PAIR + h] = jnp.broadcast_to(row_max, (ATT_TQ, LANE))
        stat_ref[h] = jnp.zeros((ATT_TQ, LANE), F32)
    acc_ref[...] = jnp.zeros_like(acc_ref)

    def pass2(j, carry):
        vj = v_ref[kv_rows(j), :]
        for h in range(ATT_HEADS_PER_PAIR):
            m = stat_ref[ATT_HEADS_PER_PAIR + h]
            s = s_ref[h, j]
            p = jnp.concatenate(
                [jnp.exp2(s[:, t * LANE:(t + 1) * LANE] - m) for t in range(ATT_TK // LANE)], axis=1)
            stat_ref[h] = stat_ref[h] + _fold_lanes(p, jnp.add)
            acc_ref[h] = acc_ref[h] + jnp.dot(p.astype(BF16), vj, preferred_element_type=F32)
        return carry
    lax.fori_loop(0, qi + 1, pass2, 0)

    l1 = jnp.sum(stat_ref[0], axis=-1, keepdims=True)
    l2 = jnp.sum(stat_ref[1], axis=-1, keepdims=True)
    lam = (jnp.exp(jnp.sum(lq1_ref[...] * lk1_ref[...], axis=-1, keepdims=True))
           - jnp.exp(jnp.sum(lq2_ref[...] * lk2_ref[...], axis=-1, keepdims=True)) + LAM_INIT)
    o = acc_ref[0] / l1 - lam * (acc_ref[1] / l2)
    o = o * lax.rsqrt(jnp.mean(o * o, axis=-1, keepdims=True) + SUBLN_EPS) * sw_ref[...]
    o_ref[...] = (o * (1.0 - LAM_INIT)).astype(o_ref.dtype)


def _attn(proj3, cos_t, sin_t, qw, kw, lq1, lk1, lq2, lk2, sw):
    b, s, _ = proj3.shape
    qblk = COL_Q // ATT_PAIR
    kblk = COL_K // ATT_PAIR
    vblk = COL_V // ATT_PAIR
    small = lambda width: pl.BlockSpec((1, width), lambda bi, h, qi: (0, 0))
    return pl.pallas_call(
        _attn_kernel,
        out_shape=jax.ShapeDtypeStruct((b, s, ATT_V), BF16),
        grid=(b, ATT_HEADS, s // ATT_TQ),
        in_specs=[
            pl.BlockSpec((None, ATT_TQ, ATT_PAIR), lambda bi, h, qi: (bi, qi, qblk + h)),
            pl.BlockSpec((None, s, ATT_PAIR), lambda bi, h, qi: (bi, 0, kblk + h)),
            pl.BlockSpec((None, s, ATT_PAIR), lambda bi, h, qi: (bi, 0, vblk + h)),
            pl.BlockSpec((s, ATT_PAIR), lambda bi, h, qi: (0, 0)),
            pl.BlockSpec((s, ATT_PAIR), lambda bi, h, qi: (0, 0)),
            small(ATT_PAIR), small(ATT_PAIR),
            small(ATT_HEAD_DIM), small(ATT_HEAD_DIM), small(ATT_HEAD_DIM), small(ATT_HEAD_DIM),
            small(ATT_PAIR),
        ],
        out_specs=pl.BlockSpec((None, ATT_TQ, ATT_PAIR), lambda bi, h, qi: (bi, qi, h)),
        scratch_shapes=[
            pltpu.VMEM((s, ATT_PAIR), BF16),
            pltpu.VMEM((ATT_HEADS_PER_PAIR, ATT_TQ, ATT_PAIR), BF16),
            pltpu.VMEM((ATT_HEADS_PER_PAIR, s // ATT_TK, ATT_TQ, ATT_TK), F32),
            pltpu.VMEM((2 * ATT_HEADS_PER_PAIR, ATT_TQ, LANE), F32),
            pltpu.VMEM((ATT_HEADS_PER_PAIR, ATT_TQ, ATT_PAIR), F32),
        ],
        compiler_params=_cparams(("parallel", "parallel", "arbitrary")),
        name="attn",
    )(proj3, proj3, proj3, cos_t, sin_t, qw, kw, lq1, lk1, lq2, lk2, sw)


MERGE_TM = 512
MERGE_TN = 1024


def _merge_kernel(ys_ref, ws_ref, ya_ref, wa_ref, gs_ref, ga_ref, o_ref):
    bs = jnp.dot(ys_ref[...], ws_ref[...], preferred_element_type=F32)
    ba = jnp.dot(ya_ref[...], wa_ref[...], preferred_element_type=F32)
    gs = jax.nn.sigmoid(gs_ref[...].astype(F32))
    ga = jax.nn.sigmoid(ga_ref[...].astype(F32))
    o_ref[...] = (gs * bs + ga * ba).astype(o_ref.dtype)


def _merge(y_ssm, w_ssm, y_att, w_att, proj):
    m = y_ssm.shape[0]
    gsblk = COL_GS // MERGE_TN
    gablk = COL_GA // MERGE_TN
    return pl.pallas_call(
        _merge_kernel,
        out_shape=jax.ShapeDtypeStruct((m, D_MODEL), BF16),
        grid=(m // MERGE_TM, D_MODEL // MERGE_TN),
        in_specs=[
            pl.BlockSpec((MERGE_TM, SSM_D_INNER), lambda i, j: (i, 0)),
            pl.BlockSpec((SSM_D_INNER, MERGE_TN), lambda i, j: (0, j)),
            pl.BlockSpec((MERGE_TM, ATT_V), lambda i, j: (i, 0)),
            pl.BlockSpec((ATT_V, MERGE_TN), lambda i, j: (0, j)),
            pl.BlockSpec((MERGE_TM, MERGE_TN), lambda i, j: (i, gsblk + j)),
            pl.BlockSpec((MERGE_TM, MERGE_TN), lambda i, j: (i, gablk + j)),
        ],
        out_specs=pl.BlockSpec((MERGE_TM, MERGE_TN), lambda i, j: (i, j)),
        compiler_params=_cparams(("parallel", "parallel")),
        name="merge",
    )(y_ssm, w_ssm, y_att, w_att, proj, proj)


OUTPROJ_TM = 512


def _outproj_kernel(x_ref, m_ref, w_ref, nw_ref, x1_ref, h2_ref):
    x1 = x_ref[...] + jnp.dot(m_ref[...], w_ref[...], preferred_element_type=F32)
    x1_ref[...] = x1
    h2_ref[...] = _rms_rows(x1, nw_ref[...], NORM_EPS).astype(h2_ref.dtype)


def _outproj(x2d, merged, w_out, norm_w):
    m = x2d.shape[0]
    return pl.pallas_call(
        _outproj_kernel,
        out_shape=(jax.ShapeDtypeStruct((m, D_MODEL), F32), jax.ShapeDtypeStruct((m, D_MODEL), BF16)),
        grid=(m // OUTPROJ_TM,),
        in_specs=[
            pl.BlockSpec((OUTPROJ_TM, D_MODEL), lambda i: (i, 0)),
            pl.BlockSpec((OUTPROJ_TM, D_MODEL), lambda i: (i, 0)),
            pl.BlockSpec((D_MODEL, D_MODEL), lambda i: (0, 0)),
            pl.BlockSpec((1, D_MODEL), lambda i: (0, 0)),
        ],
        out_specs=(
            pl.BlockSpec((OUTPROJ_TM, D_MODEL), lambda i: (i, 0)),
            pl.BlockSpec((OUTPROJ_TM, D_MODEL), lambda i: (i, 0)),
        ),
        compiler_params=_cparams(("parallel",)),
        name="outproj",
    )(x2d, merged, w_out, norm_w)


FFN_TM = 1024
FFN_TF = 512
DOWN_TM = 512
DOWN_TN = 512


def _ffn_up_kernel(h_ref, wg_ref, wu_ref, a_ref):
    h = h_ref[...]
    g = jnp.dot(h, wg_ref[...].astype(BF16), preferred_element_type=F32)
    u = jnp.dot(h, wu_ref[...].astype(BF16), preferred_element_type=F32)
    a_ref[...] = (g * jax.nn.sigmoid(g) * u).astype(a_ref.dtype)


def _ffn_up(h2, w_gate, w_up):
    m = h2.shape[0]
    return pl.pallas_call(
        _ffn_up_kernel,
        out_shape=jax.ShapeDtypeStruct((m, FFN_HIDDEN), BF16),
        grid=(m // FFN_TM, FFN_HIDDEN // FFN_TF),
        in_specs=[
            pl.BlockSpec((FFN_TM, D_MODEL), lambda i, j: (i, 0)),
            pl.BlockSpec((D_MODEL, FFN_TF), lambda i, j: (0, j)),
            pl.BlockSpec((D_MODEL, FFN_TF), lambda i, j: (0, j)),
        ],
        out_specs=pl.BlockSpec((FFN_TM, FFN_TF), lambda i, j: (i, j)),
        compiler_params=_cparams(("parallel", "parallel")),
        name="ffn_up",
    )(h2, w_gate, w_up)


def _ffn_down_kernel(a_ref, w_ref, x1_ref, o_ref, wbf_ref):
    @pl.when(pl.program_id(1) == 0)
    def _():
        wbf_ref[...] = w_ref[...].astype(BF16)

    o_ref[...] = x1_ref[...] + jnp.dot(a_ref[...], wbf_ref[...], preferred_element_type=F32)


def _ffn_down(a, w_down, x1):
    m = a.shape[0]
    return pl.pallas_call(
        _ffn_down_kernel,
        out_shape=jax.ShapeDtypeStruct((m, D_MODEL), F32),
        grid=(D_MODEL // DOWN_TN, m // DOWN_TM),
        in_specs=[
            pl.BlockSpec((DOWN_TM, FFN_HIDDEN), lambda j, i: (i, 0)),
            pl.BlockSpec((FFN_HIDDEN, DOWN_TN), lambda j, i: (0, j)),
            pl.BlockSpec((DOWN_TM, DOWN_TN), lambda j, i: (i, j)),
        ],
        out_specs=pl.BlockSpec((DOWN_TM, DOWN_TN), lambda j, i: (i, j)),
        scratch_shapes=[pltpu.VMEM((FFN_HIDDEN, DOWN_TN), BF16)],
        compiler_params=_cparams(("parallel", "arbitrary")),
        name="ffn_down",
    )(a, w_down, x1)


def _rope_tables(seq):
    half = ATT_HEAD_DIM // 2
    inv_freq = ROPE_THETA ** (-jnp.arange(half, dtype=F32) / half)
    ang = jnp.arange(seq, dtype=jnp.int32).astype(F32)[:, None] * inv_freq[None, :]
    cos = jnp.cos(ang)
    sin = jnp.sin(ang)
    reps = ATT_PAIR // ATT_HEAD_DIM
    cos_t = jnp.tile(jnp.concatenate([cos, cos], axis=-1), (1, reps))
    sin_t = jnp.tile(jnp.concatenate([-sin, sin], axis=-1), (1, reps))
    return cos_t, sin_t


def _layer(x, mix_norm_w, w_in, conv_w, conv_b, dt_bias, a_log, d_skip, ssm_norm_w, w_ssm_proj,
           q_norm_w, k_norm_w, lambda_q1, lambda_k1, lambda_q2, lambda_k2, subln_w, w_attn_proj,
           w_out, ffn_norm_w, w_ffn_gate, w_ffn_up, w_ffn_down):
    b, s, d = x.shape
    m = b * s
    x2d = x.reshape(m, d)

    dt0 = SSM_COLS
    dt1 = dt0 + SSM_HEADS
    norm_w = mix_norm_w.reshape(1, d)
    w_t = w_in.T
    proj_ssm, dt_raw = _inproj(x2d, norm_w, w_t, 0, SSM_COLS, "inproj_ssm", dt_row0=dt0)
    (proj_att,) = _inproj(x2d, norm_w, w_t, dt1, ATT_COLS, "inproj_att")
    proj3 = proj_ssm.reshape(b, s, SSM_COLS)
    proj_att3 = proj_att.reshape(b, s, ATT_COLS)

    xact = _xconv(proj3, conv_w, conv_b.reshape(1, SSM_CONV_CH))
    dt_g = dt_raw[:, :SSM_HEADS].reshape(b, s, SSM_GROUPS, HPG)
    dt_col = jnp.transpose(dt_g, (0, 2, 1, 3))
    dt_row = jnp.transpose(dt_g, (0, 2, 3, 1))
    per_head_c = lambda v: v.astype(F32).reshape(SSM_GROUPS, 1, HPG)
    per_head_r = lambda v: v.astype(F32).reshape(SSM_GROUPS, HPG, 1)
    dskip_row = jnp.repeat(d_skip.astype(F32), SSM_HEAD_DIM).reshape(1, SSM_D_INNER)
    y_ssm = _ssd(xact, proj3, dt_col, dt_row,
                 per_head_c(dt_bias), per_head_c(a_log), per_head_r(dt_bias), per_head_r(a_log),
                 dskip_row, ssm_norm_w.reshape(1, SSM_D_INNER))

    cos_t, sin_t = _rope_tables(s)
    pair_w = lambda v: jnp.tile(v.astype(F32), ATT_PAIR // ATT_HEAD_DIM).reshape(1, ATT_PAIR)
    lam_w = lambda v: v.astype(F32).reshape(1, ATT_HEAD_DIM)
    y_att = _attn(proj_att3, cos_t, sin_t, pair_w(q_norm_w), pair_w(k_norm_w),
                  lam_w(lambda_q1), lam_w(lambda_k1), lam_w(lambda_q2), lam_w(lambda_k2),
                  subln_w.astype(F32).reshape(1, ATT_PAIR))

    merged = _merge(y_ssm.reshape(m, SSM_D_INNER), w_ssm_proj.astype(BF16),
                    y_att.reshape(m, ATT_V), w_attn_proj.astype(BF16), proj_att)
    x1, h2 = _outproj(x2d, merged, w_out.astype(BF16), ffn_norm_w.reshape(1, d))
    a = _ffn_up(h2, w_ffn_gate, w_ffn_up)
    out = _ffn_down(a, w_ffn_down, x1)
    return out.reshape(b, s, d)


def kernel(x, mix_norm_w, w_in, conv_w, conv_b, dt_bias, a_log, d_skip, ssm_norm_w, w_ssm_proj, q_norm_w, k_norm_w, lambda_q1, lambda_k1, lambda_q2, lambda_k2, subln_w, w_attn_proj, w_out, ffn_norm_w, w_ffn_gate, w_ffn_up, w_ffn_down):
    depth = w_in.shape[0]
    assert depth == 1, "LAM_INIT is derived for a single layer"
    layer = lambda v: v[0]
    return _layer(x, *(layer(v) for v in (
        mix_norm_w, w_in, conv_w, conv_b, dt_bias, a_log, d_skip, ssm_norm_w, w_ssm_proj, q_norm_w, k_norm_w,
        lambda_q1, lambda_k1, lambda_q2, lambda_k2, subln_w, w_attn_proj, w_out, ffn_norm_w, w_ffn_gate,
        w_ffn_up, w_ffn_down)))
```

```python
import math

import jax
import jax.numpy as jnp
from jax import lax
from jax.experimental import pallas as pl
from jax.experimental.pallas import tpu as pltpu

F32 = jnp.float32
BF16 = jnp.bfloat16

D_MODEL = 2048
SSM_D_INNER = 2 * D_MODEL
SSM_HEAD_DIM = 64
SSM_HEADS = SSM_D_INNER // SSM_HEAD_DIM
SSM_GROUPS = 8
SSM_HEADS_PER_GROUP = SSM_HEADS // SSM_GROUPS
SSM_GROUP_WIDTH = SSM_D_INNER // SSM_GROUPS
SSM_STATE = 128
SSM_CONV = 4
SSM_CHUNK = 256
SSM_CONV_CH = SSM_D_INNER + 2 * SSM_GROUPS * SSM_STATE
ATT_HEAD_DIM = 64
ATT_HEADS = D_MODEL // (2 * ATT_HEAD_DIM)
ATT_PAIR = 2 * ATT_HEAD_DIM
ATT_QK = 2 * ATT_HEADS * ATT_HEAD_DIM
ATT_V = ATT_HEADS * 2 * ATT_HEAD_DIM
ROPE_THETA = 10000.0
FFN_HIDDEN = (((8 * D_MODEL + 2) // 3 + 255) // 256) * 256
NORM_EPS = 1e-6
SUBLN_EPS = 1e-5
LAM_INIT = 0.8 - 0.6 * math.exp(-0.3 * 0)
LOG2E = math.log2(math.e)

COL_Z = 0
COL_X = COL_Z + SSM_D_INNER
SSM_COLS = COL_X + SSM_CONV_CH
COL_Q = 0
COL_K = COL_Q + ATT_QK
COL_V = COL_K + ATT_QK
COL_GS = COL_V + ATT_V
COL_GA = COL_GS + D_MODEL
ATT_COLS = COL_GA + D_MODEL
LANE = 128
SUBLANES = 8
DT_PAD = LANE

VMEM_LIMIT = 56 * 1024 * 1024


def _cparams(sem):
    return pltpu.CompilerParams(dimension_semantics=sem, vmem_limit_bytes=VMEM_LIMIT)


def _silu(u):
    h = 0.5 * u
    return h * jnp.tanh(h) + h


INPROJ_TM = 1024
INPROJ_TN = 1024
NORM_ROWS = 256


def _rms_rows(x, w, eps):
    ms = jnp.mean(x * x, axis=-1, keepdims=True)
    return x * lax.rsqrt(ms + eps) * w


def _norm_tile(x_ref, nw_ref, h_ref):
    def body(i, carry):
        rows = pl.ds(pl.multiple_of(i * NORM_ROWS, NORM_ROWS), NORM_ROWS)
        h_ref[rows, :] = _rms_rows(x_ref[rows, :], nw_ref[...], NORM_EPS).astype(BF16)
        return carry
    lax.fori_loop(0, INPROJ_TM // NORM_ROWS, body, 0)


def _project_rows(h_ref, wt_ref):
    wt = wt_ref[...].astype(BF16)
    return lax.dot_general(h_ref[...], wt, (((1,), (1,)), ((), ())), preferred_element_type=F32)


def _inproj_dt_kernel(x_ref, nw_ref, wt_ref, wdt_ref, out_ref, dt_ref, h_ref):
    @pl.when(pl.program_id(1) == 0)
    def _():
        _norm_tile(x_ref, nw_ref, h_ref)
        dt_ref[...] = _project_rows(h_ref, wdt_ref)

    out_ref[...] = _project_rows(h_ref, wt_ref).astype(out_ref.dtype)


def _inproj_kernel(x_ref, nw_ref, wt_ref, out_ref, h_ref):
    @pl.when(pl.program_id(1) == 0)
    def _():
        _norm_tile(x_ref, nw_ref, h_ref)

    out_ref[...] = _project_rows(h_ref, wt_ref).astype(out_ref.dtype)


def _inproj(x2d, norm_w, w_t, row0, n_cols, name, dt_row0=None):
    m = x2d.shape[0]
    in_specs = [
        pl.BlockSpec((INPROJ_TM, D_MODEL), lambda i, j: (i, 0)),
        pl.BlockSpec((1, D_MODEL), lambda i, j: (0, 0)),
        pl.BlockSpec((pl.Element(INPROJ_TN), pl.Element(D_MODEL)),
                     lambda i, j: (pl.multiple_of(row0 + j * INPROJ_TN, SUBLANES), 0)),
    ]
    out_shape = [jax.ShapeDtypeStruct((m, n_cols), BF16)]
    out_specs = [pl.BlockSpec((INPROJ_TM, INPROJ_TN), lambda i, j: (i, j))]
    operands = [x2d, norm_w, w_t]
    if dt_row0 is not None:
        in_specs.append(pl.BlockSpec((pl.Element(DT_PAD), pl.Element(D_MODEL)), lambda i, j: (dt_row0, 0)))
        out_shape.append(jax.ShapeDtypeStruct((m, DT_PAD), F32))
        out_specs.append(pl.BlockSpec((INPROJ_TM, DT_PAD), lambda i, j: (i, 0)))
        operands.append(w_t)
    return pl.pallas_call(
        _inproj_kernel if dt_row0 is None else _inproj_dt_kernel,
        out_shape=tuple(out_shape),
        grid=(m // INPROJ_TM, n_cols // INPROJ_TN),
        in_specs=in_specs,
        out_specs=tuple(out_specs),
        scratch_shapes=[pltpu.VMEM((INPROJ_TM, D_MODEL), BF16)],
        compiler_params=_cparams(("parallel", "arbitrary")),
        name=name,
    )(*operands)


L = SSM_CHUNK
TAIL = 16
TAPS_BACK = SSM_CONV - 1
CONV_TW = 2048
CONV_SLAB = 512


def _xconv_kernel(cur_ref, prev_ref, w_ref, b_ref, shift_ref, tshift_ref, o_ref):
    has_prev = pl.program_id(1) > 0
    for s0 in range(0, CONV_TW, CONV_SLAB):
        cols = slice(s0, s0 + CONV_SLAB)
        cur_bf = cur_ref[:, cols]
        prev_bf = jnp.where(has_prev, prev_ref[:, cols], jnp.zeros((TAIL, CONV_SLAB), BF16))
        cur = cur_bf.astype(F32)
        shifted = jnp.dot(shift_ref[...], cur_bf, preferred_element_type=F32)
        carried = jnp.dot(tshift_ref[...], prev_bf, preferred_element_type=F32)
        w = w_ref[:, cols]
        u = b_ref[:, cols] + w[TAPS_BACK:SSM_CONV, :] * cur
        top = jnp.zeros((TAIL, CONV_SLAB), F32)
        for k in range(TAPS_BACK):
            u = u + w[k:k + 1, :] * shifted[k * L:(k + 1) * L, :]
            top = top + w[k:k + 1, :] * carried[k * TAIL:(k + 1) * TAIL, :]
        u = jnp.concatenate([u[0:TAIL, :] + top, u[TAIL:L, :]], axis=0)
        o_ref[:, cols] = _silu(u).astype(o_ref.dtype)


def _xconv(proj3, conv_w, conv_b):
    b, s, _ = proj3.shape
    nc = s // L
    xblk = COL_X // CONV_TW
    tail_per_block = L // TAIL
    shift = jnp.concatenate([jnp.eye(L, L, k=-(TAPS_BACK - k), dtype=BF16) for k in range(TAPS_BACK)], axis=0)
    tshift = jnp.concatenate(
        [jnp.eye(TAIL, TAIL, k=TAIL - (TAPS_BACK - k), dtype=BF16) for k in range(TAPS_BACK)], axis=0)
    return pl.pallas_call(
        _xconv_kernel,
        out_shape=jax.ShapeDtypeStruct((b, s, SSM_CONV_CH), BF16),
        grid=(b, nc, SSM_CONV_CH // CONV_TW),
        in_specs=[
            pl.BlockSpec((None, L, CONV_TW), lambda bi, c, j: (bi, c, xblk + j)),
            pl.BlockSpec((None, TAIL, CONV_TW),
                         lambda bi, c, j: (bi, jnp.maximum(c * tail_per_block - 1, 0), xblk + j)),
            pl.BlockSpec((SSM_CONV, CONV_TW), lambda bi, c, j: (0, j)),
            pl.BlockSpec((1, CONV_TW), lambda bi, c, j: (0, j)),
            pl.BlockSpec((TAPS_BACK * L, L), lambda bi, c, j: (0, 0)),
            pl.BlockSpec((TAPS_BACK * TAIL, TAIL), lambda bi, c, j: (0, 0)),
        ],
        out_specs=pl.BlockSpec((None, L, CONV_TW), lambda bi, c, j: (bi, c, j)),
        compiler_params=_cparams(("parallel", "parallel", "parallel")),
        name="xconv",
    )(proj3, proj3, conv_w, conv_b, shift, tshift)


HALF = L // 2
HPG = SSM_HEADS_PER_GROUP
GW = SSM_GROUP_WIDTH
PAIRS = HPG // 2


def _split3(x):
    hi = x.astype(BF16)
    rest = x - hi.astype(F32)
    mid = rest.astype(BF16)
    lo = (rest - mid.astype(F32)).astype(BF16)
    return hi, mid, lo


def _ssd_kernel(x_ref, b_ref, c_ref, z_ref, dtc_ref, dtr_ref,
                biasc_ref, alogc_ref, biasr_ref, alogr_ref, dskip_ref, nw_ref, tri_ref, trit_ref,
                y_ref, state_ref, xde_ref, ycat_ref):
    @pl.when(pl.program_id(2) == 0)
    def _():
        state_ref[...] = jnp.zeros_like(state_ref)

    bm_bf = b_ref[...]
    cm_bf = c_ref[...]
    bt_bf = bm_bf.astype(F32).T.astype(BF16)

    dtc = jax.nn.softplus(dtc_ref[...] + biasc_ref[...])
    dtr = jax.nn.softplus(dtr_ref[...] + biasr_ref[...])
    dac = dtc * (-LOG2E * jnp.exp(alogc_ref[...]))
    dar = dtr * (-LOG2E * jnp.exp(alogr_ref[...]))
    acs_c = sum(jnp.dot(tri_ref[...], part, preferred_element_type=F32) for part in _split3(dac))
    acs_r = sum(jnp.dot(part, trit_ref[...], preferred_element_type=F32) for part in _split3(dar))
    a_last = acs_c[L - 1:L, :]
    arow = acs_r - jnp.log2(dtr)
    g_end = jnp.log2(dtc) - acs_c + a_last
    dec_chunk = jnp.exp2(a_last)

    cb = lax.dot_general(cm_bf, bm_bf, (((1,), (1,)), ((), ())), preferred_element_type=F32)
    y_off = jnp.dot(cm_bf, state_ref[...].astype(BF16), preferred_element_type=F32)
    cb00 = cb[0:HALF, 0:HALF]
    cb10 = cb[HALF:L, 0:HALF]
    cb11 = cb[HALF:L, HALF:L]
    causal = (lax.broadcasted_iota(jnp.int32, (HALF, HALF), 0)
              >= lax.broadcasted_iota(jnp.int32, (HALF, HALF), 1))
    lane = lax.broadcasted_iota(jnp.int32, (1, LANE), 1)
    first = lane < SSM_HEAD_DIM

    for p in range(PAIRS):
        lanes = slice(p * LANE, (p + 1) * LANE)
        xs_bf = x_ref[:, lanes]
        xs_p = xs_bf.astype(F32)
        a_cols = [jnp.broadcast_to(acs_c[:, r:r + 1], (L, LANE)) for r in (2 * p, 2 * p + 1)]
        a_pair = jnp.where(first, a_cols[0], a_cols[1])
        g_pair = jnp.where(first, g_end[:, 2 * p:2 * p + 1], g_end[:, 2 * p + 1:2 * p + 2])
        xde_ref[:, lanes] = (xs_p * jnp.exp2(g_pair)).astype(BF16)
        y_heads = []
        for h in range(2):
            a_col = a_cols[h]
            a_row = arow[2 * p + h:2 * p + h + 1, :]
            d00 = jnp.exp2(jnp.where(causal, a_col[0:HALF, :] - a_row[:, 0:HALF], -jnp.inf))
            d10 = jnp.exp2(a_col[HALF:L, :] - a_row[:, 0:HALF])
            d11 = jnp.exp2(jnp.where(causal, a_col[HALF:L, :] - a_row[:, HALF:L], -jnp.inf))
            l_top = (cb00 * d00).astype(BF16)
            l_bot = jnp.concatenate([cb10 * d10, cb11 * d11], axis=1).astype(BF16)
            y_heads.append(jnp.concatenate(
                [jnp.dot(l_top, xs_bf[0:HALF, :], preferred_element_type=F32),
                 jnp.dot(l_bot, xs_bf, preferred_element_type=F32)], axis=0))
        y_diag = jnp.where(first, y_heads[0], y_heads[1])
        ycat_ref[:, lanes] = y_diag + y_off[:, lanes] * jnp.exp2(a_pair) + dskip_ref[:, lanes] * xs_p

    upd = jnp.dot(bt_bf, xde_ref[...], preferred_element_type=F32)
    for p in range(PAIRS):
        lanes = slice(p * LANE, (p + 1) * LANE)
        dec_p = jnp.where(first, dec_chunk[:, 2 * p:2 * p + 1], dec_chunk[:, 2 * p + 1:2 * p + 2])
        state_ref[:, lanes] = state_ref[:, lanes] * dec_p + upd[:, lanes]

    y = ycat_ref[...] * _silu(z_ref[...].astype(F32))
    y = y * lax.rsqrt(jnp.mean(y * y, axis=-1, keepdims=True) + SUBLN_EPS)
    y_ref[...] = (y * nw_ref[...]).astype(y_ref.dtype)


def _ssd(xact, proj3, dt_col, dt_row, bias_c, alog_c, bias_r, alog_r, dskip_row, norm_w):
    b, s, _ = proj3.shape
    nc = s // L
    bblk = SSM_D_INNER // SSM_STATE
    cblk = bblk + SSM_GROUPS
    tri = jnp.tril(jnp.ones((L, L), BF16))
    const = lambda shape: pl.BlockSpec(shape, lambda bi, g, c: (0, 0))
    in_specs = [
        pl.BlockSpec((None, L, GW), lambda bi, g, c: (bi, c, g)),
        pl.BlockSpec((None, L, SSM_STATE), lambda bi, g, c: (bi, c, bblk + g)),
        pl.BlockSpec((None, L, SSM_STATE), lambda bi, g, c: (bi, c, cblk + g)),
        pl.BlockSpec((None, L, GW), lambda bi, g, c: (bi, c, g)),
        pl.BlockSpec((None, None, L, HPG), lambda bi, g, c: (bi, g, c, 0)),
        pl.BlockSpec((None, None, HPG, L), lambda bi, g, c: (bi, g, 0, c)),
        pl.BlockSpec((None, 1, HPG), lambda bi, g, c: (g, 0, 0)),
        pl.BlockSpec((None, 1, HPG), lambda bi, g, c: (g, 0, 0)),
        pl.BlockSpec((None, HPG, 1), lambda bi, g, c: (g, 0, 0)),
        pl.BlockSpec((None, HPG, 1), lambda bi, g, c: (g, 0, 0)),
        pl.BlockSpec((1, GW), lambda bi, g, c: (0, g)),
        pl.BlockSpec((1, GW), lambda bi, g, c: (0, g)),
        const((L, L)),
        const((L, L)),
    ]
    return pl.pallas_call(
        _ssd_kernel,
        out_shape=jax.ShapeDtypeStruct((b, s, SSM_D_INNER), BF16),
        grid=(b, SSM_GROUPS, nc),
        in_specs=in_specs,
        out_specs=pl.BlockSpec((None, L, GW), lambda bi, g, c: (bi, c, g)),
        scratch_shapes=[
            pltpu.VMEM((SSM_STATE, GW), F32),
            pltpu.VMEM((L, GW), BF16),
            pltpu.VMEM((L, GW), F32),
        ],
        compiler_params=_cparams(("parallel", "parallel", "arbitrary")),
        name="ssd",
    )(xact, xact, xact, proj3, dt_col, dt_row, bias_c, alog_c, bias_r, alog_r, dskip_row, norm_w, tri, tri.T)


ATT_TQ = 512
ATT_TK = 512
ROPE_ROWS = 1024
ATT_HEADS_PER_PAIR = 2
ATT_HQ = ATT_TQ // 2


def _norm_rope(x, w, cos, sin_signed, group_ones):
    ss = jnp.dot((x * x).astype(BF16), group_ones, preferred_element_type=F32)
    xn = x * lax.rsqrt(ss * (1.0 / ATT_HEAD_DIM) + NORM_EPS) * w
    lane = lax.broadcasted_iota(jnp.int32, (1, ATT_PAIR), 1)
    low_half = (lane % ATT_HEAD_DIM) < (ATT_HEAD_DIM // 2)
    half = ATT_HEAD_DIM // 2
    rot = jnp.where(low_half, pltpu.roll(xn, ATT_PAIR - half, 1), pltpu.roll(xn, half, 1))
    return xn * cos + rot * sin_signed


def _fold_lanes(x, op):
    out = x[:, 0:LANE]
    for t in range(1, x.shape[1] // LANE):
        out = op(out, x[:, t * LANE:(t + 1) * LANE])
    return out


def _nt_dot(a, b):
    return lax.dot_general(a, b, (((1,), (1,)), ((), ())), preferred_element_type=F32)


def _attn_kernel(q_ref, k_ref, v_ref, cos_ref, sin_ref, qw_ref, kw_ref,
                  lq1_ref, lk1_ref, lq2_ref, lk2_ref, sw_ref, o_ref,
                  krot_ref, qrot_ref, s_ref, sd_ref, stat_ref, acc_ref):
    seq = k_ref.shape[0]
    ri = lax.broadcasted_iota(jnp.int32, (ATT_PAIR, ATT_PAIR), 0) // ATT_HEAD_DIM
    ci = lax.broadcasted_iota(jnp.int32, (ATT_PAIR, ATT_PAIR), 1) // ATT_HEAD_DIM
    group_ones = (ri == ci).astype(BF16)
    lane = lax.broadcasted_iota(jnp.int32, (1, ATT_PAIR), 1)

    def prep(i, carry):
        rows = pl.ds(pl.multiple_of(i * ROPE_ROWS, ROPE_ROWS), ROPE_ROWS)
        cos = cos_ref[rows, :]
        sin = sin_ref[rows, :]
        kr = _norm_rope(k_ref[rows, :].astype(F32), kw_ref[...], cos, sin, group_ones)
        krot_ref[rows, :] = kr.astype(BF16)
        qr = _norm_rope(q_ref[rows, :].astype(F32), qw_ref[...], cos, sin, group_ones)
        qr = qr * (LOG2E * ATT_HEAD_DIM ** -0.5)
        qrot_ref[0, rows, :] = jnp.where(lane < ATT_HEAD_DIM, qr, 0.0).astype(BF16)
        qrot_ref[1, rows, :] = jnp.where(lane >= ATT_HEAD_DIM, qr, 0.0).astype(BF16)
        return carry
    lax.fori_loop(0, seq // ROPE_ROWS, prep, 0)

    lam = (jnp.exp(jnp.sum(lq1_ref[...] * lk1_ref[...], axis=-1, keepdims=True))
           - jnp.exp(jnp.sum(lq2_ref[...] * lk2_ref[...], axis=-1, keepdims=True)) + LAM_INIT)
    tri_top = (lax.broadcasted_iota(jnp.int32, (ATT_HQ, ATT_HQ), 0)
               >= lax.broadcasted_iota(jnp.int32, (ATT_HQ, ATT_HQ), 1))
    tri_bot = (lax.broadcasted_iota(jnp.int32, (ATT_HQ, ATT_TK), 0) + ATT_HQ
               >= lax.broadcasted_iota(jnp.int32, (ATT_HQ, ATT_TK), 1))
    top = slice(0, ATT_HQ)
    bot = slice(ATT_HQ, ATT_TQ)

    for qi in range(seq // ATT_TQ):
        q0 = qi * ATT_TQ
        kd = krot_ref[q0:q0 + ATT_TQ, :]
        for h in range(ATT_HEADS_PER_PAIR):
            qh = qrot_ref[h, q0:q0 + ATT_TQ, :]
            s_top = jnp.where(tri_top, _nt_dot(qh[top, :], kd[top, :]), -jnp.inf)
            s_bot = jnp.where(tri_bot, _nt_dot(qh[bot, :], kd), -jnp.inf)
            sd_ref[h, top, 0:ATT_HQ] = s_top
            sd_ref[h, bot, :] = s_bot
            stat_ref[h, top, :] = _fold_lanes(s_top, jnp.maximum)
            stat_ref[h, bot, :] = _fold_lanes(s_bot, jnp.maximum)

        def pass1(j, carry):
            kj = krot_ref[pl.ds(pl.multiple_of(j * ATT_TK, ATT_TK), ATT_TK), :]
            for h in range(ATT_HEADS_PER_PAIR):
                s = _nt_dot(qrot_ref[h, q0:q0 + ATT_TQ, :], kj)
                s_ref[h, j] = s
                stat_ref[h] = jnp.maximum(stat_ref[h], _fold_lanes(s, jnp.maximum))
            return carry
        if qi > 0:
            lax.fori_loop(0, qi, pass1, 0)

        for h in range(ATT_HEADS_PER_PAIR):
            row_max = jnp.max(stat_ref[h], axis=-1, keepdims=True)
            stat_ref[ATT_HEADS_PER_PAIR + h] = jnp.broadcast_to(row_max, (ATT_TQ, LANE))

        vd = v_ref[q0:q0 + ATT_TQ, :]
        for h in range(ATT_HEADS_PER_PAIR):
            m = stat_ref[ATT_HEADS_PER_PAIR + h]
            p_top = jnp.concatenate(
                [jnp.exp2(sd_ref[h, top, t * LANE:(t + 1) * LANE] - m[top, :]) for t in range(ATT_HQ // LANE)],
                axis=1)
            p_bot = jnp.concatenate(
                [jnp.exp2(sd_ref[h, bot, t * LANE:(t + 1) * LANE] - m[bot, :]) for t in range(ATT_TK // LANE)],
                axis=1)
            stat_ref[h, top, :] = _fold_lanes(p_top, jnp.add)
            stat_ref[h, bot, :] = _fold_lanes(p_bot, jnp.add)
            acc_ref[h, top, :] = jnp.dot(p_top.astype(BF16), vd[top, :], preferred_element_type=F32)
            acc_ref[h, bot, :] = jnp.dot(p_bot.astype(BF16), vd, preferred_element_type=F32)

        def pass2(j, carry):
            vj = v_ref[pl.ds(pl.multiple_of(j * ATT_TK, ATT_TK), ATT_TK), :]
            for h in range(ATT_HEADS_PER_PAIR):
                m = stat_ref[ATT_HEADS_PER_PAIR + h]
                s = s_ref[h, j]
                p = jnp.concatenate(
                    [jnp.exp2(s[:, t * LANE:(t + 1) * LANE] - m) for t in range(ATT_TK // LANE)], axis=1)
                stat_ref[h] = stat_ref[h] + _fold_lanes(p, jnp.add)
                acc_ref[h] = acc_ref[h] + jnp.dot(p.astype(BF16), vj, preferred_element_type=F32)
            return carry
        if qi > 0:
            lax.fori_loop(0, qi, pass2, 0)

        l1 = jnp.sum(stat_ref[0], axis=-1, keepdims=True)
        l2 = jnp.sum(stat_ref[1], axis=-1, keepdims=True)
        o = acc_ref[0] / l1 - lam * (acc_ref[1] / l2)
        o = o * lax.rsqrt(jnp.mean(o * o, axis=-1, keepdims=True) + SUBLN_EPS) * sw_ref[...]
        o_ref[q0:q0 + ATT_TQ, :] = (o * (1.0 - LAM_INIT)).astype(o_ref.dtype)


def _attn(proj3, cos_t, sin_t, qw, kw, lq1, lk1, lq2, lk2, sw):
    b, s, _ = proj3.shape
    qblk = COL_Q // ATT_PAIR
    kblk = COL_K // ATT_PAIR
    vblk = COL_V // ATT_PAIR
    small = lambda width: pl.BlockSpec((1, width), lambda bi, h: (0, 0))
    return pl.pallas_call(
        _attn_kernel,
        out_shape=jax.ShapeDtypeStruct((b, s, ATT_V), BF16),
        grid=(b, ATT_HEADS),
        in_specs=[
            pl.BlockSpec((None, s, ATT_PAIR), lambda bi, h: (bi, 0, qblk + h)),
            pl.BlockSpec((None, s, ATT_PAIR), lambda bi, h: (bi, 0, kblk + h)),
            pl.BlockSpec((None, s, ATT_PAIR), lambda bi, h: (bi, 0, vblk + h)),
            pl.BlockSpec((s, ATT_PAIR), lambda bi, h: (0, 0)),
            pl.BlockSpec((s, ATT_PAIR), lambda bi, h: (0, 0)),
            small(ATT_PAIR), small(ATT_PAIR),
            small(ATT_HEAD_DIM), small(ATT_HEAD_DIM), small(ATT_HEAD_DIM), small(ATT_HEAD_DIM),
            small(ATT_PAIR),
        ],
        out_specs=pl.BlockSpec((None, s, ATT_PAIR), lambda bi, h: (bi, 0, h)),
        scratch_shapes=[
            pltpu.VMEM((s, ATT_PAIR), BF16),
            pltpu.VMEM((ATT_HEADS_PER_PAIR, s, ATT_PAIR), BF16),
            pltpu.VMEM((ATT_HEADS_PER_PAIR, s // ATT_TK - 1, ATT_TQ, ATT_TK), F32),
            pltpu.VMEM((ATT_HEADS_PER_PAIR, ATT_TQ, ATT_TK), F32),
            pltpu.VMEM((2 * ATT_HEADS_PER_PAIR, ATT_TQ, LANE), F32),
            pltpu.VMEM((ATT_HEADS_PER_PAIR, ATT_TQ, ATT_PAIR), F32),
        ],
        compiler_params=_cparams(("parallel", "parallel")),
        name="attn",
    )(proj3, proj3, proj3, cos_t, sin_t, qw, kw, lq1, lk1, lq2, lk2, sw)


MERGE_TM = 512
MERGE_TN = 1024


def _merge_kernel(ys_ref, ws_ref, ya_ref, wa_ref, gs_ref, ga_ref, o_ref):
    bs = jnp.dot(ys_ref[...], ws_ref[...], preferred_element_type=F32)
    ba = jnp.dot(ya_ref[...], wa_ref[...], preferred_element_type=F32)
    gs = jax.nn.sigmoid(gs_ref[...].astype(F32))
    ga = jax.nn.sigmoid(ga_ref[...].astype(F32))
    o_ref[...] = (gs * bs + ga * ba).astype(o_ref.dtype)


def _merge(y_ssm, w_ssm, y_att, w_att, proj):
    m = y_ssm.shape[0]
    gsblk = COL_GS // MERGE_TN
    gablk = COL_GA // MERGE_TN
    return pl.pallas_call(
        _merge_kernel,
        out_shape=jax.ShapeDtypeStruct((m, D_MODEL), BF16),
        grid=(m // MERGE_TM, D_MODEL // MERGE_TN),
        in_specs=[
            pl.BlockSpec((MERGE_TM, SSM_D_INNER), lambda i, j: (i, 0)),
            pl.BlockSpec((SSM_D_INNER, MERGE_TN), lambda i, j: (0, j)),
            pl.BlockSpec((MERGE_TM, ATT_V), lambda i, j: (i, 0)),
            pl.BlockSpec((ATT_V, MERGE_TN), lambda i, j: (0, j)),
            pl.BlockSpec((MERGE_TM, MERGE_TN), lambda i, j: (i, gsblk + j)),
            pl.BlockSpec((MERGE_TM, MERGE_TN), lambda i, j: (i, gablk + j)),
        ],
        out_specs=pl.BlockSpec((MERGE_TM, MERGE_TN), lambda i, j: (i, j)),
        compiler_params=_cparams(("parallel", "parallel")),
        name="merge",
    )(y_ssm, w_ssm, y_att, w_att, proj, proj)


OUTPROJ_TM = 512


def _outproj_kernel(x_ref, m_ref, w_ref, nw_ref, x1_ref, h2_ref):
    x1 = x_ref[...] + jnp.dot(m_ref[...], w_ref[...], preferred_element_type=F32)
    x1_ref[...] = x1
    h2_ref[...] = _rms_rows(x1, nw_ref[...], NORM_EPS).astype(h2_ref.dtype)


def _outproj(x2d, merged, w_out, norm_w):
    m = x2d.shape[0]
    return pl.pallas_call(
        _outproj_kernel,
        out_shape=(jax.ShapeDtypeStruct((m, D_MODEL), F32), jax.ShapeDtypeStruct((m, D_MODEL), BF16)),
        grid=(m // OUTPROJ_TM,),
        in_specs=[
            pl.BlockSpec((OUTPROJ_TM, D_MODEL), lambda i: (i, 0)),
            pl.BlockSpec((OUTPROJ_TM, D_MODEL), lambda i: (i, 0)),
            pl.BlockSpec((D_MODEL, D_MODEL), lambda i: (0, 0)),
            pl.BlockSpec((1, D_MODEL), lambda i: (0, 0)),
        ],
        out_specs=(
            pl.BlockSpec((OUTPROJ_TM, D_MODEL), lambda i: (i, 0)),
            pl.BlockSpec((OUTPROJ_TM, D_MODEL), lambda i: (i, 0)),
        ),
        compiler_params=_cparams(("parallel",)),
        name="outproj",
    )(x2d, merged, w_out, norm_w)


FFN_TM = 1024
FFN_TF = 512
DOWN_TM = 512
DOWN_TN = 512


def _ffn_up_kernel(h_ref, wg_ref, wu_ref, a_ref):
    h = h_ref[...]
    g = jnp.dot(h, wg_ref[...].astype(BF16), preferred_element_type=F32)
    u = jnp.dot(h, wu_ref[...].astype(BF16), preferred_element_type=F32)
    a_ref[...] = (g * jax.nn.sigmoid(g) * u).astype(a_ref.dtype)


def _ffn_up(h2, w_gate, w_up):
    m = h2.shape[0]
    return pl.pallas_call(
        _ffn_up_kernel,
        out_shape=jax.ShapeDtypeStruct((m, FFN_HIDDEN), BF16),
        grid=(m // FFN_TM, FFN_HIDDEN // FFN_TF),
        in_specs=[
            pl.BlockSpec((FFN_TM, D_MODEL), lambda i, j: (i, 0)),
            pl.BlockSpec((D_MODEL, FFN_TF), lambda i, j: (0, j)),
            pl.BlockSpec((D_MODEL, FFN_TF), lambda i, j: (0, j)),
        ],
        out_specs=pl.BlockSpec((FFN_TM, FFN_TF), lambda i, j: (i, j)),
        compiler_params=_cparams(("parallel", "parallel")),
        name="ffn_up",
    )(h2, w_gate, w_up)


def _ffn_down_kernel(a_ref, w_ref, x1_ref, o_ref, wbf_ref):
    @pl.when(pl.program_id(1) == 0)
    def _():
        wbf_ref[...] = w_ref[...].astype(BF16)

    o_ref[...] = x1_ref[...] + jnp.dot(a_ref[...], wbf_ref[...], preferred_element_type=F32)


def _ffn_down(a, w_down, x1):
    m = a.shape[0]
    return pl.pallas_call(
        _ffn_down_kernel,
        out_shape=jax.ShapeDtypeStruct((m, D_MODEL), F32),
        grid=(D_MODEL // DOWN_TN, m // DOWN_TM),
        in_specs=[
            pl.BlockSpec((DOWN_TM, FFN_HIDDEN), lambda j, i: (i, 0)),
            pl.BlockSpec((FFN_HIDDEN, DOWN_TN), lambda j, i: (0, j)),
            pl.BlockSpec((DOWN_TM, DOWN_TN), lambda j, i: (i, j)),
        ],
        out_specs=pl.BlockSpec((DOWN_TM, DOWN_TN), lambda j, i: (i, j)),
        scratch_shapes=[pltpu.VMEM((FFN_HIDDEN, DOWN_TN), BF16)],
        compiler_params=_cparams(("parallel", "arbitrary")),
        name="ffn_down",
    )(a, w_down, x1)


def _rope_tables(seq):
    half = ATT_HEAD_DIM // 2
    inv_freq = ROPE_THETA ** (-jnp.arange(half, dtype=F32) / half)
    ang = jnp.arange(seq, dtype=jnp.int32).astype(F32)[:, None] * inv_freq[None, :]
    cos = jnp.cos(ang)
    sin = jnp.sin(ang)
    reps = ATT_PAIR // ATT_HEAD_DIM
    cos_t = jnp.tile(jnp.concatenate([cos, cos], axis=-1), (1, reps))
    sin_t = jnp.tile(jnp.concatenate([-sin, sin], axis=-1), (1, reps))
    return cos_t, sin_t


def _layer(x, mix_norm_w, w_in, conv_w, conv_b, dt_bias, a_log, d_skip, ssm_norm_w, w_ssm_proj,
           q_norm_w, k_norm_w, lambda_q1, lambda_k1, lambda_q2, lambda_k2, subln_w, w_attn_proj,
           w_out, ffn_norm_w, w_ffn_gate, w_ffn_up, w_ffn_down):
    b, s, d = x.shape
    m = b * s
    x2d = x.reshape(m, d)

    dt0 = SSM_COLS
    dt1 = dt0 + SSM_HEADS
    norm_w = mix_norm_w.reshape(1, d)
    w_t = w_in.T
    proj_ssm, dt_raw = _inproj(x2d, norm_w, w_t, 0, SSM_COLS, "inproj_ssm", dt_row0=dt0)
    (proj_att,) = _inproj(x2d, norm_w, w_t, dt1, ATT_COLS, "inproj_att")
    proj3 = proj_ssm.reshape(b, s, SSM_COLS)
    proj_att3 = proj_att.reshape(b, s, ATT_COLS)

    xact = _xconv(proj3, conv_w, conv_b.reshape(1, SSM_CONV_CH))
    dt_g = dt_raw[:, :SSM_HEADS].reshape(b, s, SSM_GROUPS, HPG)
    dt_col = jnp.transpose(dt_g, (0, 2, 1, 3))
    dt_row = jnp.transpose(dt_g, (0, 2, 3, 1))
    per_head_c = lambda v: v.astype(F32).reshape(SSM_GROUPS, 1, HPG)
    per_head_r = lambda v: v.astype(F32).reshape(SSM_GROUPS, HPG, 1)
    dskip_row = jnp.repeat(d_skip.astype(F32), SSM_HEAD_DIM).reshape(1, SSM_D_INNER)
    y_ssm = _ssd(xact, proj3, dt_col, dt_row,
                 per_head_c(dt_bias), per_head_c(a_log), per_head_r(dt_bias), per_head_r(a_log),
                 dskip_row, ssm_norm_w.reshape(1, SSM_D_INNER))

    cos_t, sin_t = _rope_tables(s)
    pair_w = lambda v: jnp.tile(v.astype(F32), ATT_PAIR // ATT_HEAD_DIM).reshape(1, ATT_PAIR)
    lam_w = lambda v: v.astype(F32).reshape(1, ATT_HEAD_DIM)
    y_att = _attn(proj_att3, cos_t, sin_t, pair_w(q_norm_w), pair_w(k_norm_w),
                  lam_w(lambda_q1), lam_w(lambda_k1), lam_w(lambda_q2), lam_w(lambda_k2),
                  subln_w.astype(F32).reshape(1, ATT_PAIR))

    merged = _merge(y_ssm.reshape(m, SSM_D_INNER), w_ssm_proj.astype(BF16),
                    y_att.reshape(m, ATT_V), w_attn_proj.astype(BF16), proj_att)
    x1, h2 = _outproj(x2d, merged, w_out.astype(BF16), ffn_norm_w.reshape(1, d))
    a = _ffn_up(h2, w_ffn_gate, w_ffn_up)
    out = _ffn_down(a, w_ffn_down, x1)
    return out.reshape(b, s, d)


def kernel(x, mix_norm_w, w_in, conv_w, conv_b, dt_bias, a_log, d_skip, ssm_norm_w, w_ssm_proj, q_norm_w, k_norm_w, lambda_q1, lambda_k1, lambda_q2, lambda_k2, subln_w, w_attn_proj, w_out, ffn_norm_w, w_ffn_gate, w_ffn_up, w_ffn_down):
    depth = w_in.shape[0]
    assert depth == 1, "LAM_INIT is derived for a single layer"
    layer = lambda v: v[0]
    return _layer(x, *(layer(v) for v in (
        mix_norm_w, w_in, conv_w, conv_b, dt_bias, a_log, d_skip, ssm_norm_w, w_ssm_proj, q_norm_w, k_norm_w,
        lambda_q1, lambda_k1, lambda_q2, lambda_k2, subln_w, w_attn_proj, w_out, ffn_norm_w, w_ffn_gate,
        w_ffn_up, w_ffn_down)))
```

```python
import math

import jax
import jax.numpy as jnp
from jax import lax
from jax.experimental import pallas as pl
from jax.experimental.pallas import tpu as pltpu

F32 = jnp.float32
BF16 = jnp.bfloat16

D_MODEL = 2048
SSM_D_INNER = 2 * D_MODEL
SSM_HEAD_DIM = 64
SSM_HEADS = SSM_D_INNER // SSM_HEAD_DIM
SSM_GROUPS = 8
SSM_HEADS_PER_GROUP = SSM_HEADS // SSM_GROUPS
SSM_GROUP_WIDTH = SSM_D_INNER // SSM_GROUPS
SSM_STATE = 128
SSM_CONV = 4
SSM_CHUNK = 256
SSM_CONV_CH = SSM_D_INNER + 2 * SSM_GROUPS * SSM_STATE
ATT_HEAD_DIM = 64
ATT_HEADS = D_MODEL // (2 * ATT_HEAD_DIM)
ATT_PAIR = 2 * ATT_HEAD_DIM
ATT_QK = 2 * ATT_HEADS * ATT_HEAD_DIM
ATT_V = ATT_HEADS * 2 * ATT_HEAD_DIM
ROPE_THETA = 10000.0
FFN_HIDDEN = (((8 * D_MODEL + 2) // 3 + 255) // 256) * 256
NORM_EPS = 1e-6
SUBLN_EPS = 1e-5
LAM_INIT = 0.8 - 0.6 * math.exp(-0.3 * 0)
LOG2E = math.log2(math.e)

COL_Z = 0
COL_X = COL_Z + SSM_D_INNER
SSM_COLS = COL_X + SSM_CONV_CH
COL_Q = 0
COL_K = COL_Q + ATT_QK
COL_V = COL_K + ATT_QK
COL_GS = COL_V + ATT_V
COL_GA = COL_GS + D_MODEL
ATT_COLS = COL_GA + D_MODEL
LANE = 128
SUBLANES = 8
DT_PAD = LANE

VMEM_LIMIT = 56 * 1024 * 1024


def _cparams(sem):
    return pltpu.CompilerParams(dimension_semantics=sem, vmem_limit_bytes=VMEM_LIMIT)


def _silu(u):
    h = 0.5 * u
    return h * jnp.tanh(h) + h


INPROJ_TM = 1024
INPROJ_TN = 1024
NORM_ROWS = 256


def _rms_rows(x, w, eps):
    ms = jnp.mean(x * x, axis=-1, keepdims=True)
    return x * lax.rsqrt(ms + eps) * w


def _norm_tile(x_ref, nw_ref, h_ref):
    def body(i, carry):
        rows = pl.ds(pl.multiple_of(i * NORM_ROWS, NORM_ROWS), NORM_ROWS)
        h_ref[rows, :] = _rms_rows(x_ref[rows, :], nw_ref[...], NORM_EPS).astype(BF16)
        return carry
    lax.fori_loop(0, INPROJ_TM // NORM_ROWS, body, 0)


def _project_rows(h_ref, wt_ref):
    wt = wt_ref[...].astype(BF16)
    return lax.dot_general(h_ref[...], wt, (((1,), (1,)), ((), ())), preferred_element_type=F32)


def _inproj_dt_kernel(x_ref, nw_ref, wt_ref, wdt_ref, out_ref, dt_ref, h_ref):
    @pl.when(pl.program_id(1) == 0)
    def _():
        _norm_tile(x_ref, nw_ref, h_ref)
        dt_ref[...] = _project_rows(h_ref, wdt_ref)

    out_ref[...] = _project_rows(h_ref, wt_ref).astype(out_ref.dtype)


def _inproj_kernel(x_ref, nw_ref, wt_ref, out_ref, h_ref):
    @pl.when(pl.program_id(1) == 0)
    def _():
        _norm_tile(x_ref, nw_ref, h_ref)

    out_ref[...] = _project_rows(h_ref, wt_ref).astype(out_ref.dtype)


def _inproj(x2d, norm_w, w_t, row0, n_cols, name, dt_row0=None):
    m = x2d.shape[0]
    in_specs = [
        pl.BlockSpec((INPROJ_TM, D_MODEL), lambda i, j: (i, 0)),
        pl.BlockSpec((1, D_MODEL), lambda i, j: (0, 0)),
        pl.BlockSpec((pl.Element(INPROJ_TN), pl.Element(D_MODEL)),
                     lambda i, j: (pl.multiple_of(row0 + j * INPROJ_TN, SUBLANES), 0)),
    ]
    out_shape = [jax.ShapeDtypeStruct((m, n_cols), BF16)]
    out_specs = [pl.BlockSpec((INPROJ_TM, INPROJ_TN), lambda i, j: (i, j))]
    operands = [x2d, norm_w, w_t]
    if dt_row0 is not None:
        in_specs.append(pl.BlockSpec((pl.Element(DT_PAD), pl.Element(D_MODEL)), lambda i, j: (dt_row0, 0)))
        out_shape.append(jax.ShapeDtypeStruct((m, DT_PAD), F32))
        out_specs.append(pl.BlockSpec((INPROJ_TM, DT_PAD), lambda i, j: (i, 0)))
        operands.append(w_t)
    return pl.pallas_call(
        _inproj_kernel if dt_row0 is None else _inproj_dt_kernel,
        out_shape=tuple(out_shape),
        grid=(m // INPROJ_TM, n_cols // INPROJ_TN),
        in_specs=in_specs,
        out_specs=tuple(out_specs),
        scratch_shapes=[pltpu.VMEM((INPROJ_TM, D_MODEL), BF16)],
        compiler_params=_cparams(("parallel", "arbitrary")),
        name=name,
    )(*operands)


L = SSM_CHUNK
TAIL = 16
TAPS_BACK = SSM_CONV - 1
CONV_TW = 2048
CONV_SLAB = 512


def _xconv_kernel(cur_ref, prev_ref, w_ref, b_ref, shift_ref, tshift_ref, o_ref):
    has_prev = pl.program_id(1) > 0
    for s0 in range(0, CONV_TW, CONV_SLAB):
        cols = slice(s0, s0 + CONV_SLAB)
        cur_bf = cur_ref[:, cols]
        prev_bf = jnp.where(has_prev, prev_ref[:, cols], jnp.zeros((TAIL, CONV_SLAB), BF16))
        cur = cur_bf.astype(F32)
        shifted = jnp.dot(shift_ref[...], cur_bf, preferred_element_type=F32)
        carried = jnp.dot(tshift_ref[...], prev_bf, preferred_element_type=F32)
        w = w_ref[:, cols]
        u = b_ref[:, cols] + w[TAPS_BACK:SSM_CONV, :] * cur
        top = jnp.zeros((TAIL, CONV_SLAB), F32)
        for k in range(TAPS_BACK):
            u = u + w[k:k + 1, :] * shifted[k * L:(k + 1) * L, :]
            top = top + w[k:k + 1, :] * carried[k * TAIL:(k + 1) * TAIL, :]
        u = jnp.concatenate([u[0:TAIL, :] + top, u[TAIL:L, :]], axis=0)
        o_ref[:, cols] = _silu(u).astype(o_ref.dtype)


def _xconv(proj3, conv_w, conv_b):
    b, s, _ = proj3.shape
    nc = s // L
    xblk = COL_X // CONV_TW
    tail_per_block = L // TAIL
    shift = jnp.concatenate([jnp.eye(L, L, k=-(TAPS_BACK - k), dtype=BF16) for k in range(TAPS_BACK)], axis=0)
    tshift = jnp.concatenate(
        [jnp.eye(TAIL, TAIL, k=TAIL - (TAPS_BACK - k), dtype=BF16) for k in range(TAPS_BACK)], axis=0)
    return pl.pallas_call(
        _xconv_kernel,
        out_shape=jax.ShapeDtypeStruct((b, s, SSM_CONV_CH), BF16),
        grid=(b, nc, SSM_CONV_CH // CONV_TW),
        in_specs=[
            pl.BlockSpec((None, L, CONV_TW), lambda bi, c, j: (bi, c, xblk + j)),
            pl.BlockSpec((None, TAIL, CONV_TW),
                         lambda bi, c, j: (bi, jnp.maximum(c * tail_per_block - 1, 0), xblk + j)),
            pl.BlockSpec((SSM_CONV, CONV_TW), lambda bi, c, j: (0, j)),
            pl.BlockSpec((1, CONV_TW), lambda bi, c, j: (0, j)),
            pl.BlockSpec((TAPS_BACK * L, L), lambda bi, c, j: (0, 0)),
            pl.BlockSpec((TAPS_BACK * TAIL, TAIL), lambda bi, c, j: (0, 0)),
        ],
        out_specs=pl.BlockSpec((None, L, CONV_TW), lambda bi, c, j: (bi, c, j)),
        compiler_params=_cparams(("parallel", "parallel", "parallel")),
        name="xconv",
    )(proj3, proj3, conv_w, conv_b, shift, tshift)


HALF = L // 2
HPG = SSM_HEADS_PER_GROUP
GW = SSM_GROUP_WIDTH
PAIRS = HPG // 2


def _split3(x):
    hi = x.astype(BF16)
    rest = x - hi.astype(F32)
    mid = rest.astype(BF16)
    lo = (rest - mid.astype(F32)).astype(BF16)
    return hi, mid, lo


def _ssd_kernel(x_ref, b_ref, c_ref, z_ref, dtc_ref, dtr_ref,
                biasc_ref, alogc_ref, biasr_ref, alogr_ref, dskip_ref, nw_ref, tri_ref, trit_ref,
                y_ref, state_ref, xde_ref, ycat_ref):
    @pl.when(pl.program_id(2) == 0)
    def _():
        state_ref[...] = jnp.zeros_like(state_ref)

    bm_bf = b_ref[...]
    cm_bf = c_ref[...]
    bt_bf = bm_bf.astype(F32).T.astype(BF16)

    dtc = jax.nn.softplus(dtc_ref[...] + biasc_ref[...])
    dtr = jax.nn.softplus(dtr_ref[...] + biasr_ref[...])
    dac = dtc * (-LOG2E * jnp.exp(alogc_ref[...]))
    dar = dtr * (-LOG2E * jnp.exp(alogr_ref[...]))
    acs_c = sum(jnp.dot(tri_ref[...], part, preferred_element_type=F32) for part in _split3(dac))
    acs_r = sum(jnp.dot(part, trit_ref[...], preferred_element_type=F32) for part in _split3(dar))
    a_last = acs_c[L - 1:L, :]
    arow = acs_r - jnp.log2(dtr)
    g_end = jnp.log2(dtc) - acs_c + a_last
    dec_chunk = jnp.exp2(a_last)

    cb = lax.dot_general(cm_bf, bm_bf, (((1,), (1,)), ((), ())), preferred_element_type=F32)
    y_off = jnp.dot(cm_bf, state_ref[...].astype(BF16), preferred_element_type=F32)
    cb00 = cb[0:HALF, 0:HALF]
    cb10 = cb[HALF:L, 0:HALF]
    cb11 = cb[HALF:L, HALF:L]
    causal = (lax.broadcasted_iota(jnp.int32, (HALF, HALF), 0)
              >= lax.broadcasted_iota(jnp.int32, (HALF, HALF), 1))
    lane = lax.broadcasted_iota(jnp.int32, (1, LANE), 1)
    first = lane < SSM_HEAD_DIM

    for p in range(PAIRS):
        lanes = slice(p * LANE, (p + 1) * LANE)
        xs_bf = x_ref[:, lanes]
        xs_p = xs_bf.astype(F32)
        a_cols = [jnp.broadcast_to(acs_c[:, r:r + 1], (L, LANE)) for r in (2 * p, 2 * p + 1)]
        a_pair = jnp.where(first, a_cols[0], a_cols[1])
        g_pair = jnp.where(first, g_end[:, 2 * p:2 * p + 1], g_end[:, 2 * p + 1:2 * p + 2])
        xde_ref[:, lanes] = (xs_p * jnp.exp2(g_pair)).astype(BF16)
        y_heads = []
        for h in range(2):
            a_col = a_cols[h]
            a_row = arow[2 * p + h:2 * p + h + 1, :]
            d00 = jnp.exp2(jnp.where(causal, a_col[0:HALF, :] - a_row[:, 0:HALF], -jnp.inf))
            d10 = jnp.exp2(a_col[HALF:L, :] - a_row[:, 0:HALF])
            d11 = jnp.exp2(jnp.where(causal, a_col[HALF:L, :] - a_row[:, HALF:L], -jnp.inf))
            l_top = (cb00 * d00).astype(BF16)
            l_bot = jnp.concatenate([cb10 * d10, cb11 * d11], axis=1).astype(BF16)
            y_heads.append(jnp.concatenate(
                [jnp.dot(l_top, xs_bf[0:HALF, :], preferred_element_type=F32),
                 jnp.dot(l_bot, xs_bf, preferred_element_type=F32)], axis=0))
        y_diag = jnp.where(first, y_heads[0], y_heads[1])
        ycat_ref[:, lanes] = y_diag + y_off[:, lanes] * jnp.exp2(a_pair) + dskip_ref[:, lanes] * xs_p

    upd = jnp.dot(bt_bf, xde_ref[...], preferred_element_type=F32)
    for p in range(PAIRS):
        lanes = slice(p * LANE, (p + 1) * LANE)
        dec_p = jnp.where(first, dec_chunk[:, 2 * p:2 * p + 1], dec_chunk[:, 2 * p + 1:2 * p + 2])
        state_ref[:, lanes] = state_ref[:, lanes] * dec_p + upd[:, lanes]

    y = ycat_ref[...] * _silu(z_ref[...].astype(F32))
    y = y * lax.rsqrt(jnp.mean(y * y, axis=-1, keepdims=True) + SUBLN_EPS)
    y_ref[...] = (y * nw_ref[...]).astype(y_ref.dtype)


def _ssd(xact, proj3, dt_col, dt_row, bias_c, alog_c, bias_r, alog_r, dskip_row, norm_w):
    b, s, _ = proj3.shape
    nc = s // L
    bblk = SSM_D_INNER // SSM_STATE
    cblk = bblk + SSM_GROUPS
    tri = jnp.tril(jnp.ones((L, L), BF16))
    const = lambda shape: pl.BlockSpec(shape, lambda bi, g, c: (0, 0))
    in_specs = [
        pl.BlockSpec((None, L, GW), lambda bi, g, c: (bi, c, g)),
        pl.BlockSpec((None, L, SSM_STATE), lambda bi, g, c: (bi, c, bblk + g)),
        pl.BlockSpec((None, L, SSM_STATE), lambda bi, g, c: (bi, c, cblk + g)),
        pl.BlockSpec((None, L, GW), lambda bi, g, c: (bi, c, g)),
        pl.BlockSpec((None, None, L, HPG), lambda bi, g, c: (bi, g, c, 0)),
        pl.BlockSpec((None, None, HPG, L), lambda bi, g, c: (bi, g, 0, c)),
        pl.BlockSpec((None, 1, HPG), lambda bi, g, c: (g, 0, 0)),
        pl.BlockSpec((None, 1, HPG), lambda bi, g, c: (g, 0, 0)),
        pl.BlockSpec((None, HPG, 1), lambda bi, g, c: (g, 0, 0)),
        pl.BlockSpec((None, HPG, 1), lambda bi, g, c: (g, 0, 0)),
        pl.BlockSpec((1, GW), lambda bi, g, c: (0, g)),
        pl.BlockSpec((1, GW), lambda bi, g, c: (0, g)),
        const((L, L)),
        const((L, L)),
    ]
    return pl.pallas_call(
        _ssd_kernel,
        out_shape=jax.ShapeDtypeStruct((b, s, SSM_D_INNER), BF16),
        grid=(b, SSM_GROUPS, nc),
        in_specs=in_specs,
        out_specs=pl.BlockSpec((None, L, GW), lambda bi, g, c: (bi, c, g)),
        scratch_shapes=[
            pltpu.VMEM((SSM_STATE, GW), F32),
            pltpu.VMEM((L, GW), BF16),
            pltpu.VMEM((L, GW), F32),
        ],
        compiler_params=_cparams(("parallel", "parallel", "arbitrary")),
        name="ssd",
    )(xact, xact, xact, proj3, dt_col, dt_row, bias_c, alog_c, bias_r, alog_r, dskip_row, norm_w, tri, tri.T)


ATT_TQ = 512
ATT_TK = 512
ROPE_ROWS = 1024
ATT_HEADS_PER_PAIR = 2
ATT_HQ = ATT_TQ // 2


def _norm_rope(x, w, cos, sin_signed, group_ones):
    ss = jnp.dot((x * x).astype(BF16), group_ones, preferred_element_type=F32)
    xn = x * lax.rsqrt(ss * (1.0 / ATT_HEAD_DIM) + NORM_EPS) * w
    lane = lax.broadcasted_iota(jnp.int32, (1, ATT_PAIR), 1)
    low_half = (lane % ATT_HEAD_DIM) < (ATT_HEAD_DIM // 2)
    half = ATT_HEAD_DIM // 2
    rot = jnp.where(low_half, pltpu.roll(xn, ATT_PAIR - half, 1), pltpu.roll(xn, half, 1))
    return xn * cos + rot * sin_signed


def _fold_lanes(x, op):
    out = x[:, 0:LANE]
    for t in range(1, x.shape[1] // LANE):
        out = op(out, x[:, t * LANE:(t + 1) * LANE])
    return out


def _nt_dot(a, b):
    return lax.dot_general(a, b, (((1,), (1,)), ((), ())), preferred_element_type=F32)


def _attn_kernel(q_ref, k_ref, v_ref, cos_ref, sin_ref, qw_ref, kw_ref,
                  lq1_ref, lk1_ref, lq2_ref, lk2_ref, sw_ref, o_ref,
                  krot_ref, qrot_ref, s_ref, sd_ref, stat_ref, acc_ref):
    seq = k_ref.shape[0]
    ri = lax.broadcasted_iota(jnp.int32, (ATT_PAIR, ATT_PAIR), 0) // ATT_HEAD_DIM
    ci = lax.broadcasted_iota(jnp.int32, (ATT_PAIR, ATT_PAIR), 1) // ATT_HEAD_DIM
    group_ones = (ri == ci).astype(BF16)
    lane = lax.broadcasted_iota(jnp.int32, (1, ATT_PAIR), 1)

    def prep(i, carry):
        rows = pl.ds(pl.multiple_of(i * ROPE_ROWS, ROPE_ROWS), ROPE_ROWS)
        cos = cos_ref[rows, :]
        sin = sin_ref[rows, :]
        kr = _norm_rope(k_ref[rows, :].astype(F32), kw_ref[...], cos, sin, group_ones)
        krot_ref[rows, :] = kr.astype(BF16)
        qr = _norm_rope(q_ref[rows, :].astype(F32), qw_ref[...], cos, sin, group_ones)
        qr = qr * (LOG2E * ATT_HEAD_DIM ** -0.5)
        qrot_ref[0, rows, :] = jnp.where(lane < ATT_HEAD_DIM, qr, 0.0).astype(BF16)
        qrot_ref[1, rows, :] = jnp.where(lane >= ATT_HEAD_DIM, qr, 0.0).astype(BF16)
        return carry
    lax.fori_loop(0, seq // ROPE_ROWS, prep, 0, unroll=True)

    lam =(jnp.exp(jnp.sum(lq1_ref[...] * lk1_ref[...], axis=-1, keepdims=True))
           - jnp.exp(jnp.sum(lq2_ref[...] * lk2_ref[...], axis=-1, keepdims=True)) + LAM_INIT)
    tri_top = (lax.broadcasted_iota(jnp.int32, (ATT_HQ, ATT_HQ), 0)
               >= lax.broadcasted_iota(jnp.int32, (ATT_HQ, ATT_HQ), 1))
    tri_bot = (lax.broadcasted_iota(jnp.int32, (ATT_HQ, ATT_TK), 0) + ATT_HQ
               >= lax.broadcasted_iota(jnp.int32, (ATT_HQ, ATT_TK), 1))
    top = slice(0, ATT_HQ)
    bot = slice(ATT_HQ, ATT_TQ)

    for qi in range(seq // ATT_TQ):
        q0 = qi * ATT_TQ
        kd = krot_ref[q0:q0 + ATT_TQ, :]
        for h in range(ATT_HEADS_PER_PAIR):
            qh = qrot_ref[h, q0:q0 + ATT_TQ, :]
            s_top = jnp.where(tri_top, _nt_dot(qh[top, :], kd[top, :]), -jnp.inf)
            s_bot = jnp.where(tri_bot, _nt_dot(qh[bot, :], kd), -jnp.inf)
            sd_ref[h, top, 0:ATT_HQ] = s_top
            sd_ref[h, bot, :] = s_bot
            stat_ref[h, top, :] = _fold_lanes(s_top, jnp.maximum)
            stat_ref[h, bot, :] = _fold_lanes(s_bot, jnp.maximum)

        def pass1(j, carry):
            kj = krot_ref[pl.ds(pl.multiple_of(j * ATT_TK, ATT_TK), ATT_TK), :]
            for h in range(ATT_HEADS_PER_PAIR):
                s = _nt_dot(qrot_ref[h, q0:q0 + ATT_TQ, :], kj)
                s_ref[h, j] = s
                stat_ref[h] = jnp.maximum(stat_ref[h], _fold_lanes(s, jnp.maximum))
            return carry
        if qi > 0:
            lax.fori_loop(0, qi, pass1, 0, unroll=True)

        for h in range(ATT_HEADS_PER_PAIR):
            row_max = jnp.max(stat_ref[h], axis=-1, keepdims=True)
            stat_ref[ATT_HEADS_PER_PAIR + h] = jnp.broadcast_to(row_max, (ATT_TQ, LANE))

        vd = v_ref[q0:q0 + ATT_TQ, :]
        for h in range(ATT_HEADS_PER_PAIR):
            m = stat_ref[ATT_HEADS_PER_PAIR + h]
            p_top = jnp.concatenate(
                [jnp.exp2(sd_ref[h, top, t * LANE:(t + 1) * LANE] - m[top, :]) for t in range(ATT_HQ // LANE)],
                axis=1)
            p_bot = jnp.concatenate(
                [jnp.exp2(sd_ref[h, bot, t * LANE:(t + 1) * LANE] - m[bot, :]) for t in range(ATT_TK // LANE)],
                axis=1)
            stat_ref[h, top, :] = _fold_lanes(p_top, jnp.add)
            stat_ref[h, bot, :] = _fold_lanes(p_bot, jnp.add)
            acc_ref[h, top, :] = jnp.dot(p_top.astype(BF16), vd[top, :], preferred_element_type=F32)
            acc_ref[h, bot, :] = jnp.dot(p_bot.astype(BF16), vd, preferred_element_type=F32)

        def pass2(j, carry):
            vj = v_ref[pl.ds(pl.multiple_of(j * ATT_TK, ATT_TK), ATT_TK), :]
            for h in range(ATT_HEADS_PER_PAIR):
                m = stat_ref[ATT_HEADS_PER_PAIR + h]
                s = s_ref[h, j]
                p = jnp.concatenate(
                    [jnp.exp2(s[:, t * LANE:(t + 1) * LANE] - m) for t in range(ATT_TK // LANE)], axis=1)
                stat_ref[h] = stat_ref[h] + _fold_lanes(p, jnp.add)
                acc_ref[h] = acc_ref[h] + jnp.dot(p.astype(BF16), vj, preferred_element_type=F32)
            return carry
        if qi > 0:
            lax.fori_loop(0, qi, pass2, 0, unroll=True)

        l1 = jnp.sum(stat_ref[0], axis=-1, keepdims=True)
        l2 = jnp.sum(stat_ref[1], axis=-1, keepdims=True)
        o = acc_ref[0] / l1 - lam * (acc_ref[1] / l2)
        o = o * lax.rsqrt(jnp.mean(o * o, axis=-1, keepdims=True) + SUBLN_EPS) * sw_ref[...]
        o_ref[q0:q0 + ATT_TQ, :] = (o * (1.0 - LAM_INIT)).astype(o_ref.dtype)


def _attn(proj3, cos_t, sin_t, qw, kw, lq1, lk1, lq2, lk2, sw):
    b, s, _ = proj3.shape
    qblk = COL_Q // ATT_PAIR
    kblk = COL_K // ATT_PAIR
    vblk = COL_V // ATT_PAIR
    small = lambda width: pl.BlockSpec((1, width), lambda bi, h: (0, 0))
    return pl.pallas_call(
        _attn_kernel,
        out_shape=jax.ShapeDtypeStruct((b, s, ATT_V), BF16),
        grid=(b, ATT_HEADS),
        in_specs=[
            pl.BlockSpec((None, s, ATT_PAIR), lambda bi, h: (bi, 0, qblk + h)),
            pl.BlockSpec((None, s, ATT_PAIR), lambda bi, h: (bi, 0, kblk + h)),
            pl.BlockSpec((None, s, ATT_PAIR), lambda bi, h: (bi, 0, vblk + h)),
            pl.BlockSpec((s, ATT_PAIR), lambda bi, h: (0, 0)),
            pl.BlockSpec((s, ATT_PAIR), lambda bi, h: (0, 0)),
            small(ATT_PAIR), small(ATT_PAIR),
            small(ATT_HEAD_DIM), small(ATT_HEAD_DIM), small(ATT_HEAD_DIM), small(ATT_HEAD_DIM),
            small(ATT_PAIR),
        ],
        out_specs=pl.BlockSpec((None, s, ATT_PAIR), lambda bi, h: (bi, 0, h)),
        scratch_shapes=[
            pltpu.VMEM((s, ATT_PAIR), BF16),
            pltpu.VMEM((ATT_HEADS_PER_PAIR, s, ATT_PAIR), BF16),
            pltpu.VMEM((ATT_HEADS_PER_PAIR, s // ATT_TK - 1, ATT_TQ, ATT_TK), F32),
            pltpu.VMEM((ATT_HEADS_PER_PAIR, ATT_TQ, ATT_TK), F32),
            pltpu.VMEM((2 * ATT_HEADS_PER_PAIR, ATT_TQ, LANE), F32),
            pltpu.VMEM((ATT_HEADS_PER_PAIR, ATT_TQ, ATT_PAIR), F32),
        ],
        compiler_params=_cparams(("parallel", "parallel")),
        name="attn",
    )(proj3, proj3, proj3, cos_t, sin_t, qw, kw, lq1, lk1, lq2, lk2, sw)


MERGE_TM = 512
MERGE_TN = 1024


def _merge_kernel(ys_ref, ws_ref, ya_ref, wa_ref, gs_ref, ga_ref, o_ref):
    bs = jnp.dot(ys_ref[...], ws_ref[...], preferred_element_type=F32)
    ba = jnp.dot(ya_ref[...], wa_ref[...], preferred_element_type=F32)
    gs = jax.nn.sigmoid(gs_ref[...].astype(F32))
    ga = jax.nn.sigmoid(ga_ref[...].astype(F32))
    o_ref[...] = (gs * bs + ga * ba).astype(o_ref.dtype)


def _merge(y_ssm, w_ssm, y_att, w_att, proj):
    m = y_ssm.shape[0]
    gsblk = COL_GS // MERGE_TN
    gablk = COL_GA // MERGE_TN
    return pl.pallas_call(
        _merge_kernel,
        out_shape=jax.ShapeDtypeStruct((m, D_MODEL), BF16),
        grid=(m // MERGE_TM, D_MODEL // MERGE_TN),
        in_specs=[
            pl.BlockSpec((MERGE_TM, SSM_D_INNER), lambda i, j: (i, 0)),
            pl.BlockSpec((SSM_D_INNER, MERGE_TN), lambda i, j: (0, j)),
            pl.BlockSpec((MERGE_TM, ATT_V), lambda i, j: (i, 0)),
            pl.BlockSpec((ATT_V, MERGE_TN), lambda i, j: (0, j)),
            pl.BlockSpec((MERGE_TM, MERGE_TN), lambda i, j: (i, gsblk + j)),
            pl.BlockSpec((MERGE_TM, MERGE_TN), lambda i, j: (i, gablk + j)),
        ],
        out_specs=pl.BlockSpec((MERGE_TM, MERGE_TN), lambda i, j: (i, j)),
        compiler_params=_cparams(("parallel", "parallel")),
        name="merge",
    )(y_ssm, w_ssm, y_att, w_att, proj, proj)


OUTPROJ_TM = 512


def _outproj_kernel(x_ref, m_ref, w_ref, nw_ref, x1_ref, h2_ref):
    x1 = x_ref[...] + jnp.dot(m_ref[...], w_ref[...], preferred_element_type=F32)
    x1_ref[...] = x1
    h2_ref[...] = _rms_rows(x1, nw_ref[...], NORM_EPS).astype(h2_ref.dtype)


def _outproj(x2d, merged, w_out, norm_w):
    m = x2d.shape[0]
    return pl.pallas_call(
        _outproj_kernel,
        out_shape=(jax.ShapeDtypeStruct((m, D_MODEL), F32), jax.ShapeDtypeStruct((m, D_MODEL), BF16)),
        grid=(m // OUTPROJ_TM,),
        in_specs=[
            pl.BlockSpec((OUTPROJ_TM, D_MODEL), lambda i: (i, 0)),
            pl.BlockSpec((OUTPROJ_TM, D_MODEL), lambda i: (i, 0)),
            pl.BlockSpec((D_MODEL, D_MODEL), lambda i: (0, 0)),
            pl.BlockSpec((1, D_MODEL), lambda i: (0, 0)),
        ],
        out_specs=(
            pl.BlockSpec((OUTPROJ_TM, D_MODEL), lambda i: (i, 0)),
            pl.BlockSpec((OUTPROJ_TM, D_MODEL), lambda i: (i, 0)),
        ),
        compiler_params=_cparams(("parallel",)),
        name="outproj",
    )(x2d, merged, w_out, norm_w)


FFN_TM = 1024
FFN_TF = 512
DOWN_TM = 512
DOWN_TN = 512


def _ffn_up_kernel(h_ref, wg_ref, wu_ref, a_ref):
    h = h_ref[...]
    g = jnp.dot(h, wg_ref[...].astype(BF16), preferred_element_type=F32)
    u = jnp.dot(h, wu_ref[...].astype(BF16), preferred_element_type=F32)
    a_ref[...] = (g * jax.nn.sigmoid(g) * u).astype(a_ref.dtype)


def _ffn_up(h2, w_gate, w_up):
    m = h2.shape[0]
    return pl.pallas_call(
        _ffn_up_kernel,
        out_shape=jax.ShapeDtypeStruct((m, FFN_HIDDEN), BF16),
        grid=(m // FFN_TM, FFN_HIDDEN // FFN_TF),
        in_specs=[
            pl.BlockSpec((FFN_TM, D_MODEL), lambda i, j: (i, 0)),
            pl.BlockSpec((D_MODEL, FFN_TF), lambda i, j: (0, j)),
            pl.BlockSpec((D_MODEL, FFN_TF), lambda i, j: (0, j)),
        ],
        out_specs=pl.BlockSpec((FFN_TM, FFN_TF), lambda i, j: (i, j)),
        compiler_params=_cparams(("parallel", "parallel")),
        name="ffn_up",
    )(h2, w_gate, w_up)


def _ffn_down_kernel(a_ref, w_ref, x1_ref, o_ref, wbf_ref):
    @pl.when(pl.program_id(1) == 0)
    def _():
        wbf_ref[...] = w_ref[...].astype(BF16)

    o_ref[...] = x1_ref[...] + jnp.dot(a_ref[...], wbf_ref[...], preferred_element_type=F32)


def _ffn_down(a, w_down, x1):
    m = a.shape[0]
    return pl.pallas_call(
        _ffn_down_kernel,
        out_shape=jax.ShapeDtypeStruct((m, D_MODEL), F32),
        grid=(D_MODEL // DOWN_TN, m // DOWN_TM),
        in_specs=[
            pl.BlockSpec((DOWN_TM, FFN_HIDDEN), lambda j, i: (i, 0)),
            pl.BlockSpec((FFN_HIDDEN, DOWN_TN), lambda j, i: (0, j)),
            pl.BlockSpec((DOWN_TM, DOWN_TN), lambda j, i: (i, j)),
        ],
        out_specs=pl.BlockSpec((DOWN_TM, DOWN_TN), lambda j, i: (i, j)),
        scratch_shapes=[pltpu.VMEM((FFN_HIDDEN, DOWN_TN), BF16)],
        compiler_params=_cparams(("parallel", "arbitrary")),
        name="ffn_down",
    )(a, w_down, x1)


def _rope_tables(seq):
    half = ATT_HEAD_DIM // 2
    inv_freq = ROPE_THETA ** (-jnp.arange(half, dtype=F32) / half)
    ang = jnp.arange(seq, dtype=jnp.int32).astype(F32)[:, None] * inv_freq[None, :]
    cos = jnp.cos(ang)
    sin = jnp.sin(ang)
    reps = ATT_PAIR // ATT_HEAD_DIM
    cos_t = jnp.tile(jnp.concatenate([cos, cos], axis=-1), (1, reps))
    sin_t = jnp.tile(jnp.concatenate([-sin, sin], axis=-1), (1, reps))
    return cos_t, sin_t


def _layer(x, mix_norm_w, w_in, conv_w, conv_b, dt_bias, a_log, d_skip, ssm_norm_w, w_ssm_proj,
           q_norm_w, k_norm_w, lambda_q1, lambda_k1, lambda_q2, lambda_k2, subln_w, w_attn_proj,
           w_out, ffn_norm_w, w_ffn_gate, w_ffn_up, w_ffn_down):
    b, s, d = x.shape
    m = b * s
    x2d = x.reshape(m, d)

    dt0 = SSM_COLS
    dt1 = dt0 + SSM_HEADS
    norm_w = mix_norm_w.reshape(1, d)
    w_t = w_in.T
    proj_ssm, dt_raw = _inproj(x2d, norm_w, w_t, 0, SSM_COLS, "inproj_ssm", dt_row0=dt0)
    (proj_att,) = _inproj(x2d, norm_w, w_t, dt1, ATT_COLS, "inproj_att")
    proj3 = proj_ssm.reshape(b, s, SSM_COLS)
    proj_att3 = proj_att.reshape(b, s, ATT_COLS)

    xact = _xconv(proj3, conv_w, conv_b.reshape(1, SSM_CONV_CH))
    dt_g = dt_raw[:, :SSM_HEADS].reshape(b, s, SSM_GROUPS, HPG)
    dt_col = jnp.transpose(dt_g, (0, 2, 1, 3))
    dt_row = jnp.transpose(dt_g, (0, 2, 3, 1))
    per_head_c = lambda v: v.astype(F32).reshape(SSM_GROUPS, 1, HPG)
    per_head_r = lambda v: v.astype(F32).reshape(SSM_GROUPS, HPG, 1)
    dskip_row = jnp.repeat(d_skip.astype(F32), SSM_HEAD_DIM).reshape(1, SSM_D_INNER)
    y_ssm = _ssd(xact, proj3, dt_col, dt_row,
                 per_head_c(dt_bias), per_head_c(a_log), per_head_r(dt_bias), per_head_r(a_log),
                 dskip_row, ssm_norm_w.reshape(1, SSM_D_INNER))

    cos_t, sin_t = _rope_tables(s)
    pair_w = lambda v: jnp.tile(v.astype(F32), ATT_PAIR // ATT_HEAD_DIM).reshape(1, ATT_PAIR)
    lam_w = lambda v: v.astype(F32).reshape(1, ATT_HEAD_DIM)
    y_att = _attn(proj_att3, cos_t, sin_t, pair_w(q_norm_w), pair_w(k_norm_w),
                  lam_w(lambda_q1), lam_w(lambda_k1), lam_w(lambda_q2), lam_w(lambda_k2),
                  subln_w.astype(F32).reshape(1, ATT_PAIR))

    merged = _merge(y_ssm.reshape(m, SSM_D_INNER), w_ssm_proj.astype(BF16),
                    y_att.reshape(m, ATT_V), w_attn_proj.astype(BF16), proj_att)
    x1, h2 = _outproj(x2d, merged, w_out.astype(BF16), ffn_norm_w.reshape(1, d))
    a = _ffn_up(h2, w_ffn_gate, w_ffn_up)
    out = _ffn_down(a, w_ffn_down, x1)
    return out.reshape(b, s, d)


def kernel(x, mix_norm_w, w_in, conv_w, conv_b, dt_bias, a_log, d_skip, ssm_norm_w, w_ssm_proj, q_norm_w, k_norm_w, lambda_q1, lambda_k1, lambda_q2, lambda_k2, subln_w, w_attn_proj, w_out, ffn_norm_w, w_ffn_gate, w_ffn_up, w_ffn_down):
    depth = w_in.shape[0]
    assert depth == 1, "LAM_INIT is derived for a single layer"
    layer = lambda v: v[0]
    return _layer(x, *(layer(v) for v in (
        mix_norm_w, w_in, conv_w, conv_b, dt_bias, a_log, d_skip, ssm_norm_w, w_ssm_proj, q_norm_w, k_norm_w,
        lambda_q1, lambda_k1, lambda_q2, lambda_k2, subln_w, w_attn_proj, w_out, ffn_norm_w, w_ffn_gate,
        w_ffn_up, w_ffn_down)))
```

```python
import math

import jax
import jax.numpy as jnp
from jax import lax
from jax.experimental import pallas as pl
from jax.experimental.pallas import tpu as pltpu

F32 = jnp.float32
BF16 = jnp.bfloat16

D_MODEL = 2048
SSM_D_INNER = 2 * D_MODEL
SSM_HEAD_DIM = 64
SSM_HEADS = SSM_D_INNER // SSM_HEAD_DIM
SSM_GROUPS = 8
SSM_HEADS_PER_GROUP = SSM_HEADS // SSM_GROUPS
SSM_GROUP_WIDTH = SSM_D_INNER // SSM_GROUPS
SSM_STATE = 128
SSM_CONV = 4
SSM_CHUNK = 256
SSM_CONV_CH = SSM_D_INNER + 2 * SSM_GROUPS * SSM_STATE
ATT_HEAD_DIM = 64
ATT_HEADS = D_MODEL // (2 * ATT_HEAD_DIM)
ATT_PAIR = 2 * ATT_HEAD_DIM
ATT_QK = 2 * ATT_HEADS * ATT_HEAD_DIM
ATT_V = ATT_HEADS * 2 * ATT_HEAD_DIM
ROPE_THETA = 10000.0
FFN_HIDDEN = (((8 * D_MODEL + 2) // 3 + 255) // 256) * 256
NORM_EPS = 1e-6
SUBLN_EPS = 1e-5
LAM_INIT = 0.8 - 0.6 * math.exp(-0.3 * 0)
LOG2E = math.log2(math.e)

COL_Z = 0
COL_X = COL_Z + SSM_D_INNER
SSM_COLS = COL_X + SSM_CONV_CH
COL_Q = 0
COL_K = COL_Q + ATT_QK
COL_V = COL_K + ATT_QK
COL_GS = COL_V + ATT_V
COL_GA = COL_GS + D_MODEL
ATT_COLS = COL_GA + D_MODEL
LANE = 128
SUBLANES = 8
DT_PAD = LANE

VMEM_LIMIT = 56 * 1024 * 1024


def _cparams(sem):
    return pltpu.CompilerParams(dimension_semantics=sem, vmem_limit_bytes=VMEM_LIMIT)


def _silu(u):
    h = 0.5 * u
    return h * jnp.tanh(h) + h


INPROJ_TM = 1024
INPROJ_TN = 1024
NORM_ROWS = 256


def _rms_rows(x, w, eps):
    ms = jnp.mean(x * x, axis=-1, keepdims=True)
    return x * lax.rsqrt(ms + eps) * w


def _norm_tile(x_ref, nw_ref, h_ref):
    def body(i, carry):
        rows = pl.ds(pl.multiple_of(i * NORM_ROWS, NORM_ROWS), NORM_ROWS)
        h_ref[rows, :] = _rms_rows(x_ref[rows, :], nw_ref[...], NORM_EPS).astype(BF16)
        return carry
    lax.fori_loop(0, INPROJ_TM // NORM_ROWS, body, 0)


def _project_rows(h_ref, wt_ref):
    wt = wt_ref[...].astype(BF16)
    return lax.dot_general(h_ref[...], wt, (((1,), (1,)), ((), ())), preferred_element_type=F32)


def _inproj_dt_kernel(x_ref, nw_ref, wt_ref, wdt_ref, out_ref, dt_ref, h_ref):
    @pl.when(pl.program_id(1) == 0)
    def _():
        _norm_tile(x_ref, nw_ref, h_ref)
        dt_ref[...] = _project_rows(h_ref, wdt_ref)

    out_ref[...] = _project_rows(h_ref, wt_ref).astype(out_ref.dtype)


def _inproj_kernel(x_ref, nw_ref, wt_ref, out_ref, h_ref):
    @pl.when(pl.program_id(1) == 0)
    def _():
        _norm_tile(x_ref, nw_ref, h_ref)

    out_ref[...] = _project_rows(h_ref, wt_ref).astype(out_ref.dtype)


def _inproj(x2d, norm_w, w_t, row0, n_cols, name, dt_row0=None):
    m = x2d.shape[0]
    in_specs = [
        pl.BlockSpec((INPROJ_TM, D_MODEL), lambda i, j: (i, 0)),
        pl.BlockSpec((1, D_MODEL), lambda i, j: (0, 0)),
        pl.BlockSpec((pl.Element(INPROJ_TN), pl.Element(D_MODEL)),
                     lambda i, j: (pl.multiple_of(row0 + j * INPROJ_TN, SUBLANES), 0)),
    ]
    out_shape = [jax.ShapeDtypeStruct((m, n_cols), BF16)]
    out_specs = [pl.BlockSpec((INPROJ_TM, INPROJ_TN), lambda i, j: (i, j))]
    operands = [x2d, norm_w, w_t]
    if dt_row0 is not None:
        in_specs.append(pl.BlockSpec((pl.Element(DT_PAD), pl.Element(D_MODEL)), lambda i, j: (dt_row0, 0)))
        out_shape.append(jax.ShapeDtypeStruct((m, DT_PAD), F32))
        out_specs.append(pl.BlockSpec((INPROJ_TM, DT_PAD), lambda i, j: (i, 0)))
        operands.append(w_t)
    return pl.pallas_call(
        _inproj_kernel if dt_row0 is None else _inproj_dt_kernel,
        out_shape=tuple(out_shape),
        grid=(m // INPROJ_TM, n_cols // INPROJ_TN),
        in_specs=in_specs,
        out_specs=tuple(out_specs),
        scratch_shapes=[pltpu.VMEM((INPROJ_TM, D_MODEL), BF16)],
        compiler_params=_cparams(("parallel", "arbitrary")),
        name=name,
    )(*operands)


L = SSM_CHUNK
TAIL = 16
TAPS_BACK = SSM_CONV - 1
CONV_TW = 2048
CONV_SLAB = 512


def _xconv_kernel(cur_ref, prev_ref, w_ref, b_ref, shift_ref, tshift_ref, o_ref):
    has_prev = pl.program_id(1) > 0
    for s0 in range(0, CONV_TW, CONV_SLAB):
        cols = slice(s0, s0 + CONV_SLAB)
        cur_bf = cur_ref[:, cols]
        prev_bf = jnp.where(has_prev, prev_ref[:, cols], jnp.zeros((TAIL, CONV_SLAB), BF16))
        cur = cur_bf.astype(F32)
        shifted = jnp.dot(shift_ref[...], cur_bf, preferred_element_type=F32)
        carried = jnp.dot(tshift_ref[...], prev_bf, preferred_element_type=F32)
        w = w_ref[:, cols]
        u = b_ref[:, cols] + w[TAPS_BACK:SSM_CONV, :] * cur
        top = jnp.zeros((TAIL, CONV_SLAB), F32)
        for k in range(TAPS_BACK):
            u = u + w[k:k + 1, :] * shifted[k * L:(k + 1) * L, :]
            top = top + w[k:k + 1, :] * carried[k * TAIL:(k + 1) * TAIL, :]
        u = jnp.concatenate([u[0:TAIL, :] + top, u[TAIL:L, :]], axis=0)
        o_ref[:, cols] = _silu(u).astype(o_ref.dtype)


def _xconv(proj3, conv_w, conv_b):
    b, s, _ = proj3.shape
    nc = s // L
    xblk = COL_X // CONV_TW
    tail_per_block = L // TAIL
    shift = jnp.concatenate([jnp.eye(L, L, k=-(TAPS_BACK - k), dtype=BF16) for k in range(TAPS_BACK)], axis=0)
    tshift = jnp.concatenate(
        [jnp.eye(TAIL, TAIL, k=TAIL - (TAPS_BACK - k), dtype=BF16) for k in range(TAPS_BACK)], axis=0)
    return pl.pallas_call(
        _xconv_kernel,
        out_shape=jax.ShapeDtypeStruct((b, s, SSM_CONV_CH), BF16),
        grid=(b, nc, SSM_CONV_CH // CONV_TW),
        in_specs=[
            pl.BlockSpec((None, L, CONV_TW), lambda bi, c, j: (bi, c, xblk + j)),
            pl.BlockSpec((None, TAIL, CONV_TW),
                         lambda bi, c, j: (bi, jnp.maximum(c * tail_per_block - 1, 0), xblk + j)),
            pl.BlockSpec((SSM_CONV, CONV_TW), lambda bi, c, j: (0, j)),
            pl.BlockSpec((1, CONV_TW), lambda bi, c, j: (0, j)),
            pl.BlockSpec((TAPS_BACK * L, L), lambda bi, c, j: (0, 0)),
            pl.BlockSpec((TAPS_BACK * TAIL, TAIL), lambda bi, c, j: (0, 0)),
        ],
        out_specs=pl.BlockSpec((None, L, CONV_TW), lambda bi, c, j: (bi, c, j)),
        compiler_params=_cparams(("parallel", "parallel", "parallel")),
        name="xconv",
    )(proj3, proj3, conv_w, conv_b, shift, tshift)


HALF = L // 2
HPG = SSM_HEADS_PER_GROUP
GW = SSM_GROUP_WIDTH
PAIRS = HPG // 2
SSD_SUB = 2


def _split3(x):
    hi = x.astype(BF16)
    rest = x - hi.astype(F32)
    mid = rest.astype(BF16)
    lo = (rest - mid.astype(F32)).astype(BF16)
    return hi, mid, lo


def _ssd_kernel(x_ref, b_ref, c_ref, z_ref, dtc_ref, dtr_ref,
                biasc_ref, alogc_ref, biasr_ref, alogr_ref, dskip_ref, nw_ref, tri_ref, trit_ref,
                y_ref, state_ref, xde_ref, ycat_ref):
    @pl.when(pl.program_id(2) == 0)
    def _():
        state_ref[...] = jnp.zeros_like(state_ref)

    for sub in range(SSD_SUB):
        _ssd_chunk(slice(sub * L, (sub + 1) * L), x_ref, b_ref, c_ref, z_ref, dtc_ref, dtr_ref,
                   biasc_ref, alogc_ref, biasr_ref, alogr_ref, dskip_ref, nw_ref, tri_ref, trit_ref,
                   y_ref, state_ref, xde_ref.at[sub], ycat_ref.at[sub])


def _ssd_chunk(rows, x_ref, b_ref, c_ref, z_ref, dtc_ref, dtr_ref,
               biasc_ref, alogc_ref, biasr_ref, alogr_ref, dskip_ref, nw_ref, tri_ref, trit_ref,
               y_ref, state_ref, xde_ref, ycat_ref):
    bm_bf = b_ref[rows, :]
    cm_bf = c_ref[rows, :]
    bt_bf = bm_bf.astype(F32).T.astype(BF16)

    dtc = jax.nn.softplus(dtc_ref[rows, :] + biasc_ref[...])
    dtr = jax.nn.softplus(dtr_ref[:, rows] + biasr_ref[...])
    dac = dtc * (-LOG2E * jnp.exp(alogc_ref[...]))
    dar = dtr * (-LOG2E * jnp.exp(alogr_ref[...]))
    acs_c = sum(jnp.dot(tri_ref[...], part, preferred_element_type=F32) for part in _split3(dac))
    acs_r = sum(jnp.dot(part, trit_ref[...], preferred_element_type=F32) for part in _split3(dar))
    a_last = acs_c[L - 1:L, :]
    arow = acs_r - jnp.log2(dtr)
    g_end = jnp.log2(dtc) - acs_c + a_last
    dec_chunk = jnp.exp2(a_last)

    cb = lax.dot_general(cm_bf, bm_bf, (((1,), (1,)), ((), ())), preferred_element_type=F32)
    y_off = jnp.dot(cm_bf, state_ref[...].astype(BF16), preferred_element_type=F32)
    cb00 = cb[0:HALF, 0:HALF]
    cb10 = cb[HALF:L, 0:HALF]
    cb11 = cb[HALF:L, HALF:L]
    causal = (lax.broadcasted_iota(jnp.int32, (HALF, HALF), 0)
              >= lax.broadcasted_iota(jnp.int32, (HALF, HALF), 1))
    lane = lax.broadcasted_iota(jnp.int32, (1, LANE), 1)
    first = lane < SSM_HEAD_DIM

    for p in range(PAIRS):
        lanes = slice(p * LANE, (p + 1) * LANE)
        xs_bf = x_ref[rows, lanes]
        xs_p = xs_bf.astype(F32)
        a_cols = [jnp.broadcast_to(acs_c[:, r:r + 1], (L, LANE)) for r in (2 * p, 2 * p + 1)]
        a_pair = jnp.where(first, a_cols[0], a_cols[1])
        g_pair = jnp.where(first, g_end[:, 2 * p:2 * p + 1], g_end[:, 2 * p + 1:2 * p + 2])
        xde_ref[:, lanes] = (xs_p * jnp.exp2(g_pair)).astype(BF16)
        y_heads = []
        for h in range(2):
            a_col = a_cols[h]
            a_row = arow[2 * p + h:2 * p + h + 1, :]
            d00 = jnp.exp2(jnp.where(causal, a_col[0:HALF, :] - a_row[:, 0:HALF], -jnp.inf))
            d10 = jnp.exp2(a_col[HALF:L, :] - a_row[:, 0:HALF])
            d11 = jnp.exp2(jnp.where(causal, a_col[HALF:L, :] - a_row[:, HALF:L], -jnp.inf))
            l_top = (cb00 * d00).astype(BF16)
            l_bot = jnp.concatenate([cb10 * d10, cb11 * d11], axis=1).astype(BF16)
            y_heads.append(jnp.concatenate(
                [jnp.dot(l_top, xs_bf[0:HALF, :], preferred_element_type=F32),
                 jnp.dot(l_bot, xs_bf, preferred_element_type=F32)], axis=0))
        y_diag = jnp.where(first, y_heads[0], y_heads[1])
        ycat_ref[:, lanes] = y_diag + y_off[:, lanes] * jnp.exp2(a_pair) + dskip_ref[:, lanes] * xs_p

    upd = jnp.dot(bt_bf, xde_ref[...], preferred_element_type=F32)
    for p in range(PAIRS):
        lanes = slice(p * LANE, (p + 1) * LANE)
        dec_p = jnp.where(first, dec_chunk[:, 2 * p:2 * p + 1], dec_chunk[:, 2 * p + 1:2 * p + 2])
        state_ref[:, lanes] = state_ref[:, lanes] * dec_p + upd[:, lanes]

    y = ycat_ref[...] * _silu(z_ref[rows, :].astype(F32))
    y = y * lax.rsqrt(jnp.mean(y * y, axis=-1, keepdims=True) + SUBLN_EPS)
    y_ref[rows, :] = (y * nw_ref[...]).astype(y_ref.dtype)


def _ssd(xact, proj3, dt_col, dt_row, bias_c, alog_c, bias_r, alog_r, dskip_row, norm_w):
    b, s, _ = proj3.shape
    nc = s // L
    bblk = SSM_D_INNER // SSM_STATE
    cblk = bblk + SSM_GROUPS
    tri = jnp.tril(jnp.ones((L, L), BF16))
    const = lambda shape: pl.BlockSpec(shape, lambda bi, g, c: (0, 0))
    rows = SSD_SUB * L
    in_specs = [
        pl.BlockSpec((None, rows, GW), lambda bi, g, c: (bi, c, g)),
        pl.BlockSpec((None, rows, SSM_STATE), lambda bi, g, c: (bi, c, bblk + g)),
        pl.BlockSpec((None, rows, SSM_STATE), lambda bi, g, c: (bi, c, cblk + g)),
        pl.BlockSpec((None, rows, GW), lambda bi, g, c: (bi, c, g)),
        pl.BlockSpec((None, None, rows, HPG), lambda bi, g, c: (bi, g, c, 0)),
        pl.BlockSpec((None, None, HPG, rows), lambda bi, g, c: (bi, g, 0, c)),
        pl.BlockSpec((None, 1, HPG), lambda bi, g, c: (g, 0, 0)),
        pl.BlockSpec((None, 1, HPG), lambda bi, g, c: (g, 0, 0)),
        pl.BlockSpec((None, HPG, 1), lambda bi, g, c: (g, 0, 0)),
        pl.BlockSpec((None, HPG, 1), lambda bi, g, c: (g, 0, 0)),
        pl.BlockSpec((1, GW), lambda bi, g, c: (0, g)),
        pl.BlockSpec((1, GW), lambda bi, g, c: (0, g)),
        const((L, L)),
        const((L, L)),
    ]
    return pl.pallas_call(
        _ssd_kernel,
        out_shape=jax.ShapeDtypeStruct((b, s, SSM_D_INNER), BF16),
        grid=(b, SSM_GROUPS, nc // SSD_SUB),
        in_specs=in_specs,
        out_specs=pl.BlockSpec((None, rows, GW), lambda bi, g, c: (bi, c, g)),
        scratch_shapes=[
            pltpu.VMEM((SSM_STATE, GW), F32),
            pltpu.VMEM((SSD_SUB, L, GW), BF16),
            pltpu.VMEM((SSD_SUB, L, GW), F32),
        ],
        compiler_params=_cparams(("parallel", "parallel", "arbitrary")),
        name="ssd",
    )(xact, xact, xact, proj3, dt_col, dt_row, bias_c, alog_c, bias_r, alog_r, dskip_row, norm_w, tri, tri.T)


ATT_TQ = 512
ATT_TK = 512
ROPE_ROWS = 1024
ATT_HEADS_PER_PAIR = 2
ATT_HQ = ATT_TQ // 2


def _norm_rope(x, w, cos, sin_signed, group_ones):
    ss = jnp.dot((x * x).astype(BF16), group_ones, preferred_element_type=F32)
    xn = x * lax.rsqrt(ss * (1.0 / ATT_HEAD_DIM) + NORM_EPS) * w
    lane = lax.broadcasted_iota(jnp.int32, (1, ATT_PAIR), 1)
    low_half = (lane % ATT_HEAD_DIM) < (ATT_HEAD_DIM // 2)
    half = ATT_HEAD_DIM // 2
    rot = jnp.where(low_half, pltpu.roll(xn, ATT_PAIR - half, 1), pltpu.roll(xn, half, 1))
    return xn * cos + rot * sin_signed


def _fold_lanes(x, op):
    out = x[:, 0:LANE]
    for t in range(1, x.shape[1] // LANE):
        out = op(out, x[:, t * LANE:(t + 1) * LANE])
    return out


def _nt_dot(a, b):
    return lax.dot_general(a, b, (((1,), (1,)), ((), ())), preferred_element_type=F32)


def _attn_kernel(q_ref, k_ref, v_ref, cos_ref, sin_ref, qw_ref, kw_ref,
                  lq1_ref, lk1_ref, lq2_ref, lk2_ref, sw_ref, o_ref,
                  krot_ref, qrot_ref, s_ref, sd_ref, stat_ref, acc_ref):
    seq = k_ref.shape[0]
    ri = lax.broadcasted_iota(jnp.int32, (ATT_PAIR, ATT_PAIR), 0) // ATT_HEAD_DIM
    ci = lax.broadcasted_iota(jnp.int32, (ATT_PAIR, ATT_PAIR), 1) // ATT_HEAD_DIM
    group_ones = (ri == ci).astype(BF16)
    lane = lax.broadcasted_iota(jnp.int32, (1, ATT_PAIR), 1)

    def prep(i, carry):
        rows = pl.ds(pl.multiple_of(i * ROPE_ROWS, ROPE_ROWS), ROPE_ROWS)
        cos = cos_ref[rows, :]
        sin = sin_ref[rows, :]
        kr = _norm_rope(k_ref[rows, :].astype(F32), kw_ref[...], cos, sin, group_ones)
        krot_ref[rows, :] = kr.astype(BF16)
        qr = _norm_rope(q_ref[rows, :].astype(F32), qw_ref[...], cos, sin, group_ones)
        qr = qr * (LOG2E * ATT_HEAD_DIM ** -0.5)
        qrot_ref[0, rows, :] = jnp.where(lane < ATT_HEAD_DIM, qr, 0.0).astype(BF16)
        qrot_ref[1, rows, :] = jnp.where(lane >= ATT_HEAD_DIM, qr, 0.0).astype(BF16)
        return carry
    lax.fori_loop(0, seq // ROPE_ROWS, prep, 0, unroll=True)

    lam =(jnp.exp(jnp.sum(lq1_ref[...] * lk1_ref[...], axis=-1, keepdims=True))
           - jnp.exp(jnp.sum(lq2_ref[...] * lk2_ref[...], axis=-1, keepdims=True)) + LAM_INIT)
    tri_top = (lax.broadcasted_iota(jnp.int32, (ATT_HQ, ATT_HQ), 0)
               >= lax.broadcasted_iota(jnp.int32, (ATT_HQ, ATT_HQ), 1))
    tri_bot = (lax.broadcasted_iota(jnp.int32, (ATT_HQ, ATT_TK), 0) + ATT_HQ
               >= lax.broadcasted_iota(jnp.int32, (ATT_HQ, ATT_TK), 1))
    top = slice(0, ATT_HQ)
    bot = slice(ATT_HQ, ATT_TQ)

    for qi in range(seq // ATT_TQ):
        q0 = qi * ATT_TQ
        kd = krot_ref[q0:q0 + ATT_TQ, :]
        for h in range(ATT_HEADS_PER_PAIR):
            qh = qrot_ref[h, q0:q0 + ATT_TQ, :]
            s_top = jnp.where(tri_top, _nt_dot(qh[top, :], kd[top, :]), -jnp.inf)
            s_bot = jnp.where(tri_bot, _nt_dot(qh[bot, :], kd), -jnp.inf)
            sd_ref[h, top, 0:ATT_HQ] = s_top
            sd_ref[h, bot, :] = s_bot
            stat_ref[h, top, :] = _fold_lanes(s_top, jnp.maximum)
            stat_ref[h, bot, :] = _fold_lanes(s_bot, jnp.maximum)

        def pass1(j, carry):
            kj = krot_ref[pl.ds(pl.multiple_of(j * ATT_TK, ATT_TK), ATT_TK), :]
            for h in range(ATT_HEADS_PER_PAIR):
                s = _nt_dot(qrot_ref[h, q0:q0 + ATT_TQ, :], kj)
                s_ref[h, j] = s
                stat_ref[h] = jnp.maximum(stat_ref[h], _fold_lanes(s, jnp.maximum))
            return carry
        if qi > 0:
            lax.fori_loop(0, qi, pass1, 0, unroll=True)

        for h in range(ATT_HEADS_PER_PAIR):
            row_max = jnp.max(stat_ref[h], axis=-1, keepdims=True)
            stat_ref[ATT_HEADS_PER_PAIR + h] = jnp.broadcast_to(row_max, (ATT_TQ, LANE))

        vd = v_ref[q0:q0 + ATT_TQ, :]
        for h in range(ATT_HEADS_PER_PAIR):
            m = stat_ref[ATT_HEADS_PER_PAIR + h]
            p_top = jnp.concatenate(
                [jnp.exp2(sd_ref[h, top, t * LANE:(t + 1) * LANE] - m[top, :]) for t in range(ATT_HQ // LANE)],
                axis=1)
            p_bot = jnp.concatenate(
                [jnp.exp2(sd_ref[h, bot, t * LANE:(t + 1) * LANE] - m[bot, :]) for t in range(ATT_TK // LANE)],
                axis=1)
            stat_ref[h, top, :] = _fold_lanes(p_top, jnp.add)
            stat_ref[h, bot, :] = _fold_lanes(p_bot, jnp.add)
            acc_ref[h, top, :] = jnp.dot(p_top.astype(BF16), vd[top, :], preferred_element_type=F32)
            acc_ref[h, bot, :] = jnp.dot(p_bot.astype(BF16), vd, preferred_element_type=F32)

        def pass2(j, carry):
            vj = v_ref[pl.ds(pl.multiple_of(j * ATT_TK, ATT_TK), ATT_TK), :]
            for h in range(ATT_HEADS_PER_PAIR):
                m = stat_ref[ATT_HEADS_PER_PAIR + h]
                s = s_ref[h, j]
                p = jnp.concatenate(
                    [jnp.exp2(s[:, t * LANE:(t + 1) * LANE] - m) for t in range(ATT_TK // LANE)], axis=1)
                stat_ref[h] = stat_ref[h] + _fold_lanes(p, jnp.add)
                acc_ref[h] = acc_ref[h] + jnp.dot(p.astype(BF16), vj, preferred_element_type=F32)
            return carry
        if qi > 0:
            lax.fori_loop(0, qi, pass2, 0, unroll=True)

        l1 = jnp.sum(stat_ref[0], axis=-1, keepdims=True)
        l2 = jnp.sum(stat_ref[1], axis=-1, keepdims=True)
        o = acc_ref[0] / l1 - lam * (acc_ref[1] / l2)
        o = o * lax.rsqrt(jnp.mean(o * o, axis=-1, keepdims=True) + SUBLN_EPS) * sw_ref[...]
        o_ref[q0:q0 + ATT_TQ, :] = (o * (1.0 - LAM_INIT)).astype(o_ref.dtype)


def _attn(proj3, cos_t, sin_t, qw, kw, lq1, lk1, lq2, lk2, sw):
    b, s, _ = proj3.shape
    qblk = COL_Q // ATT_PAIR
    kblk = COL_K // ATT_PAIR
    vblk = COL_V // ATT_PAIR
    small = lambda width: pl.BlockSpec((1, width), lambda bi, h: (0, 0))
    return pl.pallas_call(
        _attn_kernel,
        out_shape=jax.ShapeDtypeStruct((b, s, ATT_V), BF16),
        grid=(b, ATT_HEADS),
        in_specs=[
            pl.BlockSpec((None, s, ATT_PAIR), lambda bi, h: (bi, 0, qblk + h)),
            pl.BlockSpec((None, s, ATT_PAIR), lambda bi, h: (bi, 0, kblk + h)),
            pl.BlockSpec((None, s, ATT_PAIR), lambda bi, h: (bi, 0, vblk + h)),
            pl.BlockSpec((s, ATT_PAIR), lambda bi, h: (0, 0)),
            pl.BlockSpec((s, ATT_PAIR), lambda bi, h: (0, 0)),
            small(ATT_PAIR), small(ATT_PAIR),
            small(ATT_HEAD_DIM), small(ATT_HEAD_DIM), small(ATT_HEAD_DIM), small(ATT_HEAD_DIM),
            small(ATT_PAIR),
        ],
        out_specs=pl.BlockSpec((None, s, ATT_PAIR), lambda bi, h: (bi, 0, h)),
        scratch_shapes=[
            pltpu.VMEM((s, ATT_PAIR), BF16),
            pltpu.VMEM((ATT_HEADS_PER_PAIR, s, ATT_PAIR), BF16),
            pltpu.VMEM((ATT_HEADS_PER_PAIR, s // ATT_TK - 1, ATT_TQ, ATT_TK), F32),
            pltpu.VMEM((ATT_HEADS_PER_PAIR, ATT_TQ, ATT_TK), F32),
            pltpu.VMEM((2 * ATT_HEADS_PER_PAIR, ATT_TQ, LANE), F32),
            pltpu.VMEM((ATT_HEADS_PER_PAIR, ATT_TQ, ATT_PAIR), F32),
        ],
        compiler_params=_cparams(("parallel", "parallel")),
        name="attn",
    )(proj3, proj3, proj3, cos_t, sin_t, qw, kw, lq1, lk1, lq2, lk2, sw)


MERGE_TM = 512
MERGE_TN = 512


def _merge_kernel(ys_ref, ws_ref, ya_ref, wa_ref, gs_ref, ga_ref, o_ref, wsbf_ref, wabf_ref):
    @pl.when(pl.program_id(1) == 0)
    def _():
        wsbf_ref[...] = ws_ref[...].astype(BF16)
        wabf_ref[...] = wa_ref[...].astype(BF16)

    bs = jnp.dot(ys_ref[...], wsbf_ref[...], preferred_element_type=F32)
    ba = jnp.dot(ya_ref[...], wabf_ref[...], preferred_element_type=F32)
    gs = jax.nn.sigmoid(gs_ref[...].astype(F32))
    ga = jax.nn.sigmoid(ga_ref[...].astype(F32))
    o_ref[...] = (gs * bs + ga * ba).astype(o_ref.dtype)


def _merge(y_ssm, w_ssm, y_att, w_att, proj):
    m = y_ssm.shape[0]
    gsblk = COL_GS // MERGE_TN
    gablk = COL_GA // MERGE_TN
    return pl.pallas_call(
        _merge_kernel,
        out_shape=jax.ShapeDtypeStruct((m, D_MODEL), BF16),
        grid=(D_MODEL // MERGE_TN, m // MERGE_TM),
        in_specs=[
            pl.BlockSpec((MERGE_TM, SSM_D_INNER), lambda j, i: (i, 0)),
            pl.BlockSpec((SSM_D_INNER, MERGE_TN), lambda j, i: (0, j)),
            pl.BlockSpec((MERGE_TM, ATT_V), lambda j, i: (i, 0)),
            pl.BlockSpec((ATT_V, MERGE_TN), lambda j, i: (0, j)),
            pl.BlockSpec((MERGE_TM, MERGE_TN), lambda j, i: (i, gsblk + j)),
            pl.BlockSpec((MERGE_TM, MERGE_TN), lambda j, i: (i, gablk + j)),
        ],
        out_specs=pl.BlockSpec((MERGE_TM, MERGE_TN), lambda j, i: (i, j)),
        scratch_shapes=[
            pltpu.VMEM((SSM_D_INNER, MERGE_TN), BF16),
            pltpu.VMEM((ATT_V, MERGE_TN), BF16),
        ],
        compiler_params=_cparams(("parallel", "arbitrary")),
        name="merge",
    )(y_ssm, w_ssm, y_att, w_att, proj, proj)


OUTPROJ_TM = 512


def _outproj_kernel(x_ref, m_ref, w_ref, nw_ref, x1_ref, h2_ref, wbf_ref):
    @pl.when(pl.program_id(0) == 0)
    def _():
        wbf_ref[...] = w_ref[...].astype(BF16)

    x1 = x_ref[...] + jnp.dot(m_ref[...], wbf_ref[...], preferred_element_type=F32)
    x1_ref[...] = x1
    h2_ref[...] = _rms_rows(x1, nw_ref[...], NORM_EPS).astype(h2_ref.dtype)


def _outproj(x2d, merged, w_out, norm_w):
    m = x2d.shape[0]
    return pl.pallas_call(
        _outproj_kernel,
        out_shape=(jax.ShapeDtypeStruct((m, D_MODEL), F32), jax.ShapeDtypeStruct((m, D_MODEL), BF16)),
        grid=(m // OUTPROJ_TM,),
        in_specs=[
            pl.BlockSpec((OUTPROJ_TM, D_MODEL), lambda i: (i, 0)),
            pl.BlockSpec((OUTPROJ_TM, D_MODEL), lambda i: (i, 0)),
            pl.BlockSpec((D_MODEL, D_MODEL), lambda i: (0, 0), pipeline_mode=pl.Buffered(1)),
            pl.BlockSpec((1, D_MODEL), lambda i: (0, 0)),
        ],
        out_specs=(
            pl.BlockSpec((OUTPROJ_TM, D_MODEL), lambda i: (i, 0)),
            pl.BlockSpec((OUTPROJ_TM, D_MODEL), lambda i: (i, 0)),
        ),
        scratch_shapes=[pltpu.VMEM((D_MODEL, D_MODEL), BF16)],
        compiler_params=_cparams(("arbitrary",)),
        name="outproj",
    )(x2d, merged, w_out, norm_w)


FFN_TM = 1024
FFN_TF = 512
DOWN_TM = 512
DOWN_TN = 512


def _ffn_up_kernel(h_ref, wg_ref, wu_ref, a_ref):
    h = h_ref[...]
    g = jnp.dot(h, wg_ref[...].astype(BF16), preferred_element_type=F32)
    u = jnp.dot(h, wu_ref[...].astype(BF16), preferred_element_type=F32)
    a_ref[...] = (g * jax.nn.sigmoid(g) * u).astype(a_ref.dtype)


def _ffn_up(h2, w_gate, w_up):
    m = h2.shape[0]
    return pl.pallas_call(
        _ffn_up_kernel,
        out_shape=jax.ShapeDtypeStruct((m, FFN_HIDDEN), BF16),
        grid=(m // FFN_TM, FFN_HIDDEN // FFN_TF),
        in_specs=[
            pl.BlockSpec((FFN_TM, D_MODEL), lambda i, j: (i, 0)),
            pl.BlockSpec((D_MODEL, FFN_TF), lambda i, j: (0, j)),
            pl.BlockSpec((D_MODEL, FFN_TF), lambda i, j: (0, j)),
        ],
        out_specs=pl.BlockSpec((FFN_TM, FFN_TF), lambda i, j: (i, j)),
        compiler_params=_cparams(("parallel", "parallel")),
        name="ffn_up",
    )(h2, w_gate, w_up)


def _ffn_down_kernel(a_ref, w_ref, x1_ref, o_ref, wbf_ref):
    @pl.when(pl.program_id(1) == 0)
    def _():
        wbf_ref[...] = w_ref[...].astype(BF16)

    o_ref[...] = x1_ref[...] + jnp.dot(a_ref[...], wbf_ref[...], preferred_element_type=F32)


def _ffn_down(a, w_down, x1):
    m = a.shape[0]
    return pl.pallas_call(
        _ffn_down_kernel,
        out_shape=jax.ShapeDtypeStruct((m, D_MODEL), F32),
        grid=(D_MODEL // DOWN_TN, m // DOWN_TM),
        in_specs=[
            pl.BlockSpec((DOWN_TM, FFN_HIDDEN), lambda j, i: (i, 0)),
            pl.BlockSpec((FFN_HIDDEN, DOWN_TN), lambda j, i: (0, j)),
            pl.BlockSpec((DOWN_TM, DOWN_TN), lambda j, i: (i, j)),
        ],
        out_specs=pl.BlockSpec((DOWN_TM, DOWN_TN), lambda j, i: (i, j)),
        scratch_shapes=[pltpu.VMEM((FFN_HIDDEN, DOWN_TN), BF16)],
        compiler_params=_cparams(("parallel", "arbitrary")),
        name="ffn_down",
    )(a, w_down, x1)


def _rope_tables(seq):
    half = ATT_HEAD_DIM // 2
    inv_freq = ROPE_THETA ** (-jnp.arange(half, dtype=F32) / half)
    ang = jnp.arange(seq, dtype=jnp.int32).astype(F32)[:, None] * inv_freq[None, :]
    cos = jnp.cos(ang)
    sin = jnp.sin(ang)
    reps = ATT_PAIR // ATT_HEAD_DIM
    cos_t = jnp.tile(jnp.concatenate([cos, cos], axis=-1), (1, reps))
    sin_t = jnp.tile(jnp.concatenate([-sin, sin], axis=-1), (1, reps))
    return cos_t, sin_t


def _layer(x, mix_norm_w, w_in, conv_w, conv_b, dt_bias, a_log, d_skip, ssm_norm_w, w_ssm_proj,
           q_norm_w, k_norm_w, lambda_q1, lambda_k1, lambda_q2, lambda_k2, subln_w, w_attn_proj,
           w_out, ffn_norm_w, w_ffn_gate, w_ffn_up, w_ffn_down):
    b, s, d = x.shape
    m = b * s
    x2d = x.reshape(m, d)

    dt0 = SSM_COLS
    dt1 = dt0 + SSM_HEADS
    norm_w = mix_norm_w.reshape(1, d)
    w_t = w_in.T
    proj_ssm, dt_raw = _inproj(x2d, norm_w, w_t, 0, SSM_COLS, "inproj_ssm", dt_row0=dt0)
    (proj_att,) = _inproj(x2d, norm_w, w_t, dt1, ATT_COLS, "inproj_att")
    proj3 = proj_ssm.reshape(b, s, SSM_COLS)
    proj_att3 = proj_att.reshape(b, s, ATT_COLS)

    xact = _xconv(proj3, conv_w, conv_b.reshape(1, SSM_CONV_CH))
    dt_g = dt_raw[:, :SSM_HEADS].reshape(b, s, SSM_GROUPS, HPG)
    dt_col = jnp.transpose(dt_g, (0, 2, 1, 3))
    dt_row = jnp.transpose(dt_g, (0, 2, 3, 1))
    per_head_c = lambda v: v.astype(F32).reshape(SSM_GROUPS, 1, HPG)
    per_head_r = lambda v: v.astype(F32).reshape(SSM_GROUPS, HPG, 1)
    dskip_row = jnp.repeat(d_skip.astype(F32), SSM_HEAD_DIM).reshape(1, SSM_D_INNER)
    y_ssm = _ssd(xact, proj3, dt_col, dt_row,
                 per_head_c(dt_bias), per_head_c(a_log), per_head_r(dt_bias), per_head_r(a_log),
                 dskip_row, ssm_norm_w.reshape(1, SSM_D_INNER))

    cos_t, sin_t = _rope_tables(s)
    pair_w = lambda v: jnp.tile(v.astype(F32), ATT_PAIR // ATT_HEAD_DIM).reshape(1, ATT_PAIR)
    lam_w = lambda v: v.astype(F32).reshape(1, ATT_HEAD_DIM)
    y_att = _attn(proj_att3, cos_t, sin_t, pair_w(q_norm_w), pair_w(k_norm_w),
                  lam_w(lambda_q1), lam_w(lambda_k1), lam_w(lambda_q2), lam_w(lambda_k2),
                  subln_w.astype(F32).reshape(1, ATT_PAIR))

    merged = _merge(y_ssm.reshape(m, SSM_D_INNER), w_ssm_proj, y_att.reshape(m, ATT_V), w_attn_proj, proj_att)
    x1, h2 = _outproj(x2d, merged, w_out, ffn_norm_w.reshape(1, d))
    a = _ffn_up(h2, w_ffn_gate, w_ffn_up)
    out = _ffn_down(a, w_ffn_down, x1)
    return out.reshape(b, s, d)


def kernel(x, mix_norm_w, w_in, conv_w, conv_b, dt_bias, a_log, d_skip, ssm_norm_w, w_ssm_proj, q_norm_w, k_norm_w, lambda_q1, lambda_k1, lambda_q2, lambda_k2, subln_w, w_attn_proj, w_out, ffn_norm_w, w_ffn_gate, w_ffn_up, w_ffn_down):
    depth = w_in.shape[0]
    assert depth == 1, "LAM_INIT is derived for a single layer"
    layer = lambda v: v[0]
    return _layer(x, *(layer(v) for v in (
        mix_norm_w, w_in, conv_w, conv_b, dt_bias, a_log, d_skip, ssm_norm_w, w_ssm_proj, q_norm_w, k_norm_w,
        lambda_q1, lambda_k1, lambda_q2, lambda_k2, subln_w, w_attn_proj, w_out, ffn_norm_w, w_ffn_gate,
        w_ffn_up, w_ffn_down)))
```

```python
import math

import jax
import jax.numpy as jnp
from jax import lax
from jax.experimental import pallas as pl
from jax.experimental.pallas import tpu as pltpu

F32 = jnp.float32
BF16 = jnp.bfloat16

D_MODEL = 2048
SSM_D_INNER = 2 * D_MODEL
SSM_HEAD_DIM = 64
SSM_HEADS = SSM_D_INNER // SSM_HEAD_DIM
SSM_GROUPS = 8
SSM_HEADS_PER_GROUP = SSM_HEADS // SSM_GROUPS
SSM_GROUP_WIDTH = SSM_D_INNER // SSM_GROUPS
SSM_STATE = 128
SSM_CONV = 4
SSM_CHUNK = 256
SSM_CONV_CH = SSM_D_INNER + 2 * SSM_GROUPS * SSM_STATE
ATT_HEAD_DIM = 64
ATT_HEADS = D_MODEL // (2 * ATT_HEAD_DIM)
ATT_PAIR = 2 * ATT_HEAD_DIM
ATT_QK = 2 * ATT_HEADS * ATT_HEAD_DIM
ATT_V = ATT_HEADS * 2 * ATT_HEAD_DIM
ROPE_THETA = 10000.0
FFN_HIDDEN = (((8 * D_MODEL + 2) // 3 + 255) // 256) * 256
NORM_EPS = 1e-6
SUBLN_EPS = 1e-5
LAM_INIT = 0.8 - 0.6 * math.exp(-0.3 * 0)
LOG2E = math.log2(math.e)

COL_Z = 0
COL_X = COL_Z + SSM_D_INNER
SSM_COLS = COL_X + SSM_CONV_CH
COL_Q = 0
COL_K = COL_Q + ATT_QK
COL_V = COL_K + ATT_QK
COL_GS = COL_V + ATT_V
COL_GA = COL_GS + D_MODEL
ATT_COLS = COL_GA + D_MODEL
LANE = 128
SUBLANES = 8
DT_PAD = LANE

VMEM_LIMIT = 56 * 1024 * 1024


def _cparams(sem):
    return pltpu.CompilerParams(dimension_semantics=sem, vmem_limit_bytes=VMEM_LIMIT)


def _silu(u):
    h = 0.5 * u
    return h * jnp.tanh(h) + h


INPROJ_TM = 1024
INPROJ_TN = 1024
NORM_ROWS = 256


def _rms_rows(x, w, eps):
    ms = jnp.mean(x * x, axis=-1, keepdims=True)
    return x * lax.rsqrt(ms + eps) * w


def _norm_tile(x_ref, nw_ref, h_ref):
    def body(i, carry):
        rows = pl.ds(pl.multiple_of(i * NORM_ROWS, NORM_ROWS), NORM_ROWS)
        h_ref[rows, :] = _rms_rows(x_ref[rows, :], nw_ref[...], NORM_EPS).astype(BF16)
        return carry
    lax.fori_loop(0, INPROJ_TM // NORM_ROWS, body, 0)


def _project_rows(h_ref, wt_ref):
    wt = wt_ref[...].astype(BF16)
    return lax.dot_general(h_ref[...], wt, (((1,), (1,)), ((), ())), preferred_element_type=F32)


def _inproj_dt_kernel(x_ref, nw_ref, wt_ref, wdt_ref, out_ref, dt_ref, h_ref):
    @pl.when(pl.program_id(1) == 0)
    def _():
        _norm_tile(x_ref, nw_ref, h_ref)
        dt_ref[...] = _project_rows(h_ref, wdt_ref)

    out_ref[...] = _project_rows(h_ref, wt_ref).astype(out_ref.dtype)


def _inproj_kernel(x_ref, nw_ref, wt_ref, out_ref, h_ref):
    @pl.when(pl.program_id(1) == 0)
    def _():
        _norm_tile(x_ref, nw_ref, h_ref)

    out_ref[...] = _project_rows(h_ref, wt_ref).astype(out_ref.dtype)


def _inproj(x2d, norm_w, w_t, row0, n_cols, name, dt_row0=None):
    m = x2d.shape[0]
    assert row0 % SUBLANES == 0 and (dt_row0 is None or dt_row0 % SUBLANES == 0)
    in_specs = [
        pl.BlockSpec((INPROJ_TM, D_MODEL), lambda i, j: (i, 0)),
        pl.BlockSpec((1, D_MODEL), lambda i, j: (0, 0)),
        pl.BlockSpec((pl.Element(INPROJ_TN), pl.Element(D_MODEL)),
                     lambda i, j: (pl.multiple_of(row0 + j * INPROJ_TN, SUBLANES), 0)),
    ]
    out_shape = [jax.ShapeDtypeStruct((m, n_cols), BF16)]
    out_specs = [pl.BlockSpec((INPROJ_TM, INPROJ_TN), lambda i, j: (i, j))]
    operands = [x2d, norm_w, w_t]
    if dt_row0 is not None:
        in_specs.append(pl.BlockSpec((pl.Element(DT_PAD), pl.Element(D_MODEL)), lambda i, j: (dt_row0, 0)))
        out_shape.append(jax.ShapeDtypeStruct((m, DT_PAD), F32))
        out_specs.append(pl.BlockSpec((INPROJ_TM, DT_PAD), lambda i, j: (i, 0)))
        operands.append(w_t)
    return pl.pallas_call(
        _inproj_kernel if dt_row0 is None else _inproj_dt_kernel,
        out_shape=tuple(out_shape),
        grid=(m // INPROJ_TM, n_cols // INPROJ_TN),
        in_specs=in_specs,
        out_specs=tuple(out_specs),
        scratch_shapes=[pltpu.VMEM((INPROJ_TM, D_MODEL), BF16)],
        compiler_params=_cparams(("parallel", "arbitrary")),
        name=name,
    )(*operands)


L = SSM_CHUNK
TAIL = 16
TAPS_BACK = SSM_CONV - 1
CONV_TW = 2048
CONV_SLAB = 512
CONV_SUB = 4


def _xconv_kernel(cur_ref, prev_ref, w_ref, b_ref, shift_ref, tshift_ref, o_ref):
    has_prev = pl.program_id(1) > 0
    for sub, s0 in [(sub, s0) for sub in range(CONV_SUB) for s0 in range(0, CONV_TW, CONV_SLAB)]:
        cols = slice(s0, s0 + CONV_SLAB)
        r0 = sub * L
        cur_bf = cur_ref[r0:r0 + L, cols]
        if sub == 0:
            prev_bf = jnp.where(has_prev, prev_ref[:, cols], jnp.zeros((TAIL, CONV_SLAB), BF16))
        else:
            prev_bf = cur_ref[r0 - TAIL:r0, cols]
        cur = cur_bf.astype(F32)
        shifted = jnp.dot(shift_ref[...], cur_bf, preferred_element_type=F32)
        carried = jnp.dot(tshift_ref[...], prev_bf, preferred_element_type=F32)
        w = 0.5 * w_ref[:, cols]
        h = 0.5 * b_ref[:, cols] + w[TAPS_BACK:SSM_CONV, :] * cur
        top = jnp.zeros((TAIL, CONV_SLAB), F32)
        for k in range(TAPS_BACK):
            h = h + w[k:k + 1, :] * shifted[k * L:(k + 1) * L, :]
            top = top + w[k:k + 1, :] * carried[k * TAIL:(k + 1) * TAIL, :]
        h = jnp.concatenate([h[0:TAIL, :] + top, h[TAIL:L, :]], axis=0)
        o_ref[r0:r0 + L, cols] = (h * jnp.tanh(h) + h).astype(o_ref.dtype)


def _xconv(proj3, conv_w, conv_b):
    b, s, _ = proj3.shape
    rows = CONV_SUB * L
    xblk = COL_X // CONV_TW
    tail_per_block = rows // TAIL
    shift = jnp.concatenate([jnp.eye(L, L, k=-(TAPS_BACK - k), dtype=BF16) for k in range(TAPS_BACK)], axis=0)
    tshift = jnp.concatenate(
        [jnp.eye(TAIL, TAIL, k=TAIL - (TAPS_BACK - k), dtype=BF16) for k in range(TAPS_BACK)], axis=0)
    return pl.pallas_call(
        _xconv_kernel,
        out_shape=jax.ShapeDtypeStruct((b, s, SSM_CONV_CH), BF16),
        grid=(b, s // rows, SSM_CONV_CH // CONV_TW),
        in_specs=[
            pl.BlockSpec((None, rows, CONV_TW), lambda bi, c, j: (bi, c, xblk + j)),
            pl.BlockSpec((None, TAIL, CONV_TW),
                         lambda bi, c, j: (bi, jnp.maximum(c * tail_per_block - 1, 0), xblk + j)),
            pl.BlockSpec((SSM_CONV, CONV_TW), lambda bi, c, j: (0, j)),
            pl.BlockSpec((1, CONV_TW), lambda bi, c, j: (0, j)),
            pl.BlockSpec((TAPS_BACK * L, L), lambda bi, c, j: (0, 0)),
            pl.BlockSpec((TAPS_BACK * TAIL, TAIL), lambda bi, c, j: (0, 0)),
        ],
        out_specs=pl.BlockSpec((None, rows, CONV_TW), lambda bi, c, j: (bi, c, j)),
        compiler_params=_cparams(("parallel", "parallel", "parallel")),
        name="xconv",
    )(proj3, proj3, conv_w, conv_b, shift, tshift)


HALF = L // 2
HPG = SSM_HEADS_PER_GROUP
GW = SSM_GROUP_WIDTH
PAIRS = HPG // 2
SSD_SUB = 2


def _split3(x):
    hi = x.astype(BF16)
    rest = x - hi.astype(F32)
    mid = rest.astype(BF16)
    lo = (rest - mid.astype(F32)).astype(BF16)
    return hi, mid, lo


def _ssd_kernel(x_ref, b_ref, c_ref, z_ref, dtc_ref, dtr_ref,
                biasc_ref, alogc_ref, biasr_ref, alogr_ref, dskip_ref, nw_ref, tri_ref, trit_ref,
                y_ref, state_ref, xde_ref, ycat_ref):
    @pl.when(pl.program_id(2) == 0)
    def _():
        state_ref[...] = jnp.zeros_like(state_ref)

    for sub in range(SSD_SUB):
        _ssd_chunk(slice(sub * L, (sub + 1) * L), x_ref, b_ref, c_ref, z_ref, dtc_ref, dtr_ref,
                   biasc_ref, alogc_ref, biasr_ref, alogr_ref, dskip_ref, nw_ref, tri_ref, trit_ref,
                   y_ref, state_ref, xde_ref.at[sub], ycat_ref.at[sub])


def _ssd_chunk(rows, x_ref, b_ref, c_ref, z_ref, dtc_ref, dtr_ref,
               biasc_ref, alogc_ref, biasr_ref, alogr_ref, dskip_ref, nw_ref, tri_ref, trit_ref,
               y_ref, state_ref, xde_ref, ycat_ref):
    bm_bf = b_ref[rows, :]
    cm_bf = c_ref[rows, :]
    bt_bf = bm_bf.astype(F32).T.astype(BF16)

    dtc = jax.nn.softplus(dtc_ref[rows, :] + biasc_ref[...])
    dtr = jax.nn.softplus(dtr_ref[:, rows] + biasr_ref[...])
    dac = dtc * (-LOG2E * jnp.exp(alogc_ref[...]))
    dar = dtr * (-LOG2E * jnp.exp(alogr_ref[...]))
    acs_c = sum(jnp.dot(tri_ref[...], part, preferred_element_type=F32) for part in _split3(dac))
    acs_r = sum(jnp.dot(part, trit_ref[...], preferred_element_type=F32) for part in _split3(dar))
    a_last = acs_c[L - 1:L, :]
    arow = acs_r - jnp.log2(dtr)
    g_end = jnp.log2(dtc) - acs_c + a_last
    dec_chunk = jnp.exp2(a_last)

    cb = lax.dot_general(cm_bf, bm_bf, (((1,), (1,)), ((), ())), preferred_element_type=F32)
    y_off = jnp.dot(cm_bf, state_ref[...].astype(BF16), preferred_element_type=F32)
    cb00 = cb[0:HALF, 0:HALF]
    cb10 = cb[HALF:L, 0:HALF]
    cb11 = cb[HALF:L, HALF:L]
    causal = (lax.broadcasted_iota(jnp.int32, (HALF, HALF), 0)
              >= lax.broadcasted_iota(jnp.int32, (HALF, HALF), 1))
    lane = lax.broadcasted_iota(jnp.int32, (1, LANE), 1)
    first = lane < SSM_HEAD_DIM

    for p in range(PAIRS):
        lanes = slice(p * LANE, (p + 1) * LANE)
        xs_bf = x_ref[rows, lanes]
        xs_p = xs_bf.astype(F32)
        a_cols = [jnp.broadcast_to(acs_c[:, r:r + 1], (L, LANE)) for r in (2 * p, 2 * p + 1)]
        a_pair = jnp.where(first, a_cols[0], a_cols[1])
        g_pair = jnp.where(first, g_end[:, 2 * p:2 * p + 1], g_end[:, 2 * p + 1:2 * p + 2])
        xde_ref[:, lanes] = (xs_p * jnp.exp2(g_pair)).astype(BF16)
        y_heads = []
        for h in range(2):
            a_col = a_cols[h]
            a_row = arow[2 * p + h:2 * p + h + 1, :]
            d00 = jnp.exp2(jnp.where(causal, a_col[0:HALF, :] - a_row[:, 0:HALF], -jnp.inf))
            d10 = jnp.exp2(a_col[HALF:L, :] - a_row[:, 0:HALF])
            d11 = jnp.exp2(jnp.where(causal, a_col[HALF:L, :] - a_row[:, HALF:L], -jnp.inf))
            l_top = (cb00 * d00).astype(BF16)
            l_bot = jnp.concatenate([cb10 * d10, cb11 * d11], axis=1).astype(BF16)
            y_heads.append(jnp.concatenate(
                [jnp.dot(l_top, xs_bf[0:HALF, :], preferred_element_type=F32),
                 jnp.dot(l_bot, xs_bf, preferred_element_type=F32)], axis=0))
        y_diag = jnp.where(first, y_heads[0], y_heads[1])
        ycat_ref[:, lanes] = y_diag + y_off[:, lanes] * jnp.exp2(a_pair) + dskip_ref[:, lanes] * xs_p

    upd = jnp.dot(bt_bf, xde_ref[...], preferred_element_type=F32)
    for p in range(PAIRS):
        lanes = slice(p * LANE, (p + 1) * LANE)
        dec_p = jnp.where(first, dec_chunk[:, 2 * p:2 * p + 1], dec_chunk[:, 2 * p + 1:2 * p + 2])
        state_ref[:, lanes] = state_ref[:, lanes] * dec_p + upd[:, lanes]

    y = ycat_ref[...] * _silu(z_ref[rows, :].astype(F32))
    y = y * lax.rsqrt(jnp.mean(y * y, axis=-1, keepdims=True) + SUBLN_EPS)
    y_ref[rows, :] = (y * nw_ref[...]).astype(y_ref.dtype)


def _ssd(xact, proj3, dt_col, dt_row, bias_c, alog_c, bias_r, alog_r, dskip_row, norm_w):
    b, s, _ = proj3.shape
    nc = s // L
    bblk = SSM_D_INNER // SSM_STATE
    cblk = bblk + SSM_GROUPS
    tri = jnp.tril(jnp.ones((L, L), BF16))
    const = lambda shape: pl.BlockSpec(shape, lambda bi, g, c: (0, 0))
    rows = SSD_SUB * L
    in_specs = [
        pl.BlockSpec((None, rows, GW), lambda bi, g, c: (bi, c, g)),
        pl.BlockSpec((None, rows, SSM_STATE), lambda bi, g, c: (bi, c, bblk + g)),
        pl.BlockSpec((None, rows, SSM_STATE), lambda bi, g, c: (bi, c, cblk + g)),
        pl.BlockSpec((None, rows, GW), lambda bi, g, c: (bi, c, g)),
        pl.BlockSpec((None, None, rows, HPG), lambda bi, g, c: (bi, g, c, 0)),
        pl.BlockSpec((None, None, HPG, rows), lambda bi, g, c: (bi, g, 0, c)),
        pl.BlockSpec((None, 1, HPG), lambda bi, g, c: (g, 0, 0)),
        pl.BlockSpec((None, 1, HPG), lambda bi, g, c: (g, 0, 0)),
        pl.BlockSpec((None, HPG, 1), lambda bi, g, c: (g, 0, 0)),
        pl.BlockSpec((None, HPG, 1), lambda bi, g, c: (g, 0, 0)),
        pl.BlockSpec((1, GW), lambda bi, g, c: (0, g)),
        pl.BlockSpec((1, GW), lambda bi, g, c: (0, g)),
        const((L, L)),
        const((L, L)),
    ]
    return pl.pallas_call(
        _ssd_kernel,
        out_shape=jax.ShapeDtypeStruct((b, s, SSM_D_INNER), BF16),
        grid=(b, SSM_GROUPS, nc // SSD_SUB),
        in_specs=in_specs,
        out_specs=pl.BlockSpec((None, rows, GW), lambda bi, g, c: (bi, c, g)),
        scratch_shapes=[
            pltpu.VMEM((SSM_STATE, GW), F32),
            pltpu.VMEM((SSD_SUB, L, GW), BF16),
            pltpu.VMEM((SSD_SUB, L, GW), F32),
        ],
        compiler_params=_cparams(("parallel", "parallel", "arbitrary")),
        name="ssd",
    )(xact, xact, xact, proj3, dt_col, dt_row, bias_c, alog_c, bias_r, alog_r, dskip_row, norm_w, tri, tri.T)


ATT_TQ = 512
ATT_TK = 512
ROPE_ROWS = 1024
ATT_HEADS_PER_PAIR = 2
ATT_HQ = ATT_TQ // 2


def _norm_rope(x, w, cos, sin_signed, group_ones):
    ss = jnp.dot((x * x).astype(BF16), group_ones, preferred_element_type=F32)
    xn = x * lax.rsqrt(ss * (1.0 / ATT_HEAD_DIM) + NORM_EPS) * w
    lane = lax.broadcasted_iota(jnp.int32, (1, ATT_PAIR), 1)
    low_half = (lane % ATT_HEAD_DIM) < (ATT_HEAD_DIM // 2)
    half = ATT_HEAD_DIM // 2
    rot = jnp.where(low_half, pltpu.roll(xn, ATT_PAIR - half, 1), pltpu.roll(xn, half, 1))
    return xn * cos + rot * sin_signed


def _fold_lanes(x, op):
    out = x[:, 0:LANE]
    for t in range(1, x.shape[1] // LANE):
        out = op(out, x[:, t * LANE:(t + 1) * LANE])
    return out


def _nt_dot(a, b):
    return lax.dot_general(a, b, (((1,), (1,)), ((), ())), preferred_element_type=F32)


def _attn_kernel(q_ref, k_ref, v_ref, cos_ref, sin_ref, qw_ref, kw_ref,
                  lq1_ref, lk1_ref, lq2_ref, lk2_ref, sw_ref, o_ref,
                  krot_ref, qrot_ref, s_ref, sd_ref, stat_ref, acc_ref):
    seq = k_ref.shape[0]
    ri = lax.broadcasted_iota(jnp.int32, (ATT_PAIR, ATT_PAIR), 0) // ATT_HEAD_DIM
    ci = lax.broadcasted_iota(jnp.int32, (ATT_PAIR, ATT_PAIR), 1) // ATT_HEAD_DIM
    group_ones = (ri == ci).astype(BF16)
    lane = lax.broadcasted_iota(jnp.int32, (1, ATT_PAIR), 1)

    def prep(i, carry):
        rows = pl.ds(pl.multiple_of(i * ROPE_ROWS, ROPE_ROWS), ROPE_ROWS)
        cos = cos_ref[rows, :]
        sin = sin_ref[rows, :]
        kr = _norm_rope(k_ref[rows, :].astype(F32), kw_ref[...], cos, sin, group_ones)
        krot_ref[rows, :] = kr.astype(BF16)
        qr = _norm_rope(q_ref[rows, :].astype(F32), qw_ref[...], cos, sin, group_ones)
        qr = qr * (LOG2E * ATT_HEAD_DIM ** -0.5)
        qrot_ref[0, rows, :] = jnp.where(lane < ATT_HEAD_DIM, qr, 0.0).astype(BF16)
        qrot_ref[1, rows, :] = jnp.where(lane >= ATT_HEAD_DIM, qr, 0.0).astype(BF16)
        return carry
    lax.fori_loop(0, seq // ROPE_ROWS, prep, 0, unroll=True)

    lam =(jnp.exp(jnp.sum(lq1_ref[...] * lk1_ref[...], axis=-1, keepdims=True))
           - jnp.exp(jnp.sum(lq2_ref[...] * lk2_ref[...], axis=-1, keepdims=True)) + LAM_INIT)
    tri_top = (lax.broadcasted_iota(jnp.int32, (ATT_HQ, ATT_HQ), 0)
               >= lax.broadcasted_iota(jnp.int32, (ATT_HQ, ATT_HQ), 1))
    tri_bot = (lax.broadcasted_iota(jnp.int32, (ATT_HQ, ATT_TK), 0) + ATT_HQ
               >= lax.broadcasted_iota(jnp.int32, (ATT_HQ, ATT_TK), 1))
    top = slice(0, ATT_HQ)
    bot = slice(ATT_HQ, ATT_TQ)

    for qi in range(seq // ATT_TQ):
        q0 = qi * ATT_TQ
        kd = krot_ref[q0:q0 + ATT_TQ, :]
        for h in range(ATT_HEADS_PER_PAIR):
            qh = qrot_ref[h, q0:q0 + ATT_TQ, :]
            s_top = jnp.where(tri_top, _nt_dot(qh[top, :], kd[top, :]), -jnp.inf)
            s_bot = jnp.where(tri_bot, _nt_dot(qh[bot, :], kd), -jnp.inf)
            sd_ref[h, top, 0:ATT_HQ] = s_top
            sd_ref[h, bot, :] = s_bot
            stat_ref[h, top, :] = _fold_lanes(s_top, jnp.maximum)
            stat_ref[h, bot, :] = _fold_lanes(s_bot, jnp.maximum)

        def pass1(j, carry):
            kj = krot_ref[pl.ds(pl.multiple_of(j * ATT_TK, ATT_TK), ATT_TK), :]
            for h in range(ATT_HEADS_PER_PAIR):
                s = _nt_dot(qrot_ref[h, q0:q0 + ATT_TQ, :], kj)
                s_ref[h, j] = s
                stat_ref[h] = jnp.maximum(stat_ref[h], _fold_lanes(s, jnp.maximum))
            return carry
        if qi > 0:
            lax.fori_loop(0, qi, pass1, 0, unroll=True)

        for h in range(ATT_HEADS_PER_PAIR):
            row_max = jnp.max(stat_ref[h], axis=-1, keepdims=True)
            stat_ref[ATT_HEADS_PER_PAIR + h] = jnp.broadcast_to(row_max, (ATT_TQ, LANE))

        vd = v_ref[q0:q0 + ATT_TQ, :]
        for h in range(ATT_HEADS_PER_PAIR):
            m = stat_ref[ATT_HEADS_PER_PAIR + h]
            p_top = jnp.concatenate(
                [jnp.exp2(sd_ref[h, top, t * LANE:(t + 1) * LANE] - m[top, :]) for t in range(ATT_HQ // LANE)],
                axis=1)
            p_bot = jnp.concatenate(
                [jnp.exp2(sd_ref[h, bot, t * LANE:(t + 1) * LANE] - m[bot, :]) for t in range(ATT_TK // LANE)],
                axis=1)
            stat_ref[h, top, :] = _fold_lanes(p_top, jnp.add)
            stat_ref[h, bot, :] = _fold_lanes(p_bot, jnp.add)
            acc_ref[h, top, :] = jnp.dot(p_top.astype(BF16), vd[top, :], preferred_element_type=F32)
            acc_ref[h, bot, :] = jnp.dot(p_bot.astype(BF16), vd, preferred_element_type=F32)

        def pass2(j, carry):
            vj = v_ref[pl.ds(pl.multiple_of(j * ATT_TK, ATT_TK), ATT_TK), :]
            for h in range(ATT_HEADS_PER_PAIR):
                m = stat_ref[ATT_HEADS_PER_PAIR + h]
                s = s_ref[h, j]
                p = jnp.concatenate(
                    [jnp.exp2(s[:, t * LANE:(t + 1) * LANE] - m) for t in range(ATT_TK // LANE)], axis=1)
                stat_ref[h] = stat_ref[h] + _fold_lanes(p, jnp.add)
                acc_ref[h] = acc_ref[h] + jnp.dot(p.astype(BF16), vj, preferred_element_type=F32)
            return carry
        if qi > 0:
            lax.fori_loop(0, qi, pass2, 0, unroll=True)

        l1 = jnp.sum(stat_ref[0], axis=-1, keepdims=True)
        l2 = jnp.sum(stat_ref[1], axis=-1, keepdims=True)
        o = acc_ref[0] / l1 - lam * (acc_ref[1] / l2)
        o = o * lax.rsqrt(jnp.mean(o * o, axis=-1, keepdims=True) + SUBLN_EPS) * sw_ref[...]
        o_ref[q0:q0 + ATT_TQ, :] = (o * (1.0 - LAM_INIT)).astype(o_ref.dtype)


def _attn(proj3, cos_t, sin_t, qw, kw, lq1, lk1, lq2, lk2, sw):
    b, s, _ = proj3.shape
    qblk = COL_Q // ATT_PAIR
    kblk = COL_K // ATT_PAIR
    vblk = COL_V // ATT_PAIR
    small = lambda width: pl.BlockSpec((1, width), lambda bi, h: (0, 0))
    return pl.pallas_call(
        _attn_kernel,
        out_shape=jax.ShapeDtypeStruct((b, s, ATT_V), BF16),
        grid=(b, ATT_HEADS),
        in_specs=[
            pl.BlockSpec((None, s, ATT_PAIR), lambda bi, h: (bi, 0, qblk + h)),
            pl.BlockSpec((None, s, ATT_PAIR), lambda bi, h: (bi, 0, kblk + h)),
            pl.BlockSpec((None, s, ATT_PAIR), lambda bi, h: (bi, 0, vblk + h)),
            pl.BlockSpec((s, ATT_PAIR), lambda bi, h: (0, 0)),
            pl.BlockSpec((s, ATT_PAIR), lambda bi, h: (0, 0)),
            small(ATT_PAIR), small(ATT_PAIR),
            small(ATT_HEAD_DIM), small(ATT_HEAD_DIM), small(ATT_HEAD_DIM), small(ATT_HEAD_DIM),
            small(ATT_PAIR),
        ],
        out_specs=pl.BlockSpec((None, s, ATT_PAIR), lambda bi, h: (bi, 0, h)),
        scratch_shapes=[
            pltpu.VMEM((s, ATT_PAIR), BF16),
            pltpu.VMEM((ATT_HEADS_PER_PAIR, s, ATT_PAIR), BF16),
            pltpu.VMEM((ATT_HEADS_PER_PAIR, s // ATT_TK - 1, ATT_TQ, ATT_TK), F32),
            pltpu.VMEM((ATT_HEADS_PER_PAIR, ATT_TQ, ATT_TK), F32),
            pltpu.VMEM((2 * ATT_HEADS_PER_PAIR, ATT_TQ, LANE), F32),
            pltpu.VMEM((ATT_HEADS_PER_PAIR, ATT_TQ, ATT_PAIR), F32),
        ],
        compiler_params=_cparams(("parallel", "parallel")),
        name="attn",
    )(proj3, proj3, proj3, cos_t, sin_t, qw, kw, lq1, lk1, lq2, lk2, sw)


MERGE_TM = 512
MERGE_TN = 512


def _merge_kernel(ys_ref, ws_ref, ya_ref, wa_ref, gs_ref, ga_ref, o_ref, wsbf_ref, wabf_ref):
    @pl.when(pl.program_id(1) == 0)
    def _():
        wsbf_ref[...] = ws_ref[...].astype(BF16)
        wabf_ref[...] = wa_ref[...].astype(BF16)

    bs = jnp.dot(ys_ref[...], wsbf_ref[...], preferred_element_type=F32)
    ba = jnp.dot(ya_ref[...], wabf_ref[...], preferred_element_type=F32)
    gs = jax.nn.sigmoid(gs_ref[...].astype(F32))
    ga = jax.nn.sigmoid(ga_ref[...].astype(F32))
    o_ref[...] = (gs * bs + ga * ba).astype(o_ref.dtype)


def _merge(y_ssm, w_ssm, y_att, w_att, proj):
    m = y_ssm.shape[0]
    gsblk = COL_GS // MERGE_TN
    gablk = COL_GA // MERGE_TN
    return pl.pallas_call(
        _merge_kernel,
        out_shape=jax.ShapeDtypeStruct((m, D_MODEL), BF16),
        grid=(D_MODEL // MERGE_TN, m // MERGE_TM),
        in_specs=[
            pl.BlockSpec((MERGE_TM, SSM_D_INNER), lambda j, i: (i, 0)),
            pl.BlockSpec((SSM_D_INNER, MERGE_TN), lambda j, i: (0, j)),
            pl.BlockSpec((MERGE_TM, ATT_V), lambda j, i: (i, 0)),
            pl.BlockSpec((ATT_V, MERGE_TN), lambda j, i: (0, j)),
            pl.BlockSpec((MERGE_TM, MERGE_TN), lambda j, i: (i, gsblk + j)),
            pl.BlockSpec((MERGE_TM, MERGE_TN), lambda j, i: (i, gablk + j)),
        ],
        out_specs=pl.BlockSpec((MERGE_TM, MERGE_TN), lambda j, i: (i, j)),
        scratch_shapes=[
            pltpu.VMEM((SSM_D_INNER, MERGE_TN), BF16),
            pltpu.VMEM((ATT_V, MERGE_TN), BF16),
        ],
        compiler_params=_cparams(("parallel", "arbitrary")),
        name="merge",
    )(y_ssm, w_ssm, y_att, w_att, proj, proj)


OUTPROJ_TM = 512


def _outproj_kernel(x_ref, m_ref, w_ref, nw_ref, x1_ref, h2_ref, wbf_ref):
    @pl.when(pl.program_id(0) == 0)
    def _():
        wbf_ref[...] = w_ref[...].astype(BF16)

    x1 = x_ref[...] + jnp.dot(m_ref[...], wbf_ref[...], preferred_element_type=F32)
    x1_ref[...] = x1
    h2_ref[...] = _rms_rows(x1, nw_ref[...], NORM_EPS).astype(h2_ref.dtype)


def _outproj(x2d, merged, w_out, norm_w):
    m = x2d.shape[0]
    return pl.pallas_call(
        _outproj_kernel,
        out_shape=(jax.ShapeDtypeStruct((m, D_MODEL), F32), jax.ShapeDtypeStruct((m, D_MODEL), BF16)),
        grid=(m // OUTPROJ_TM,),
        in_specs=[
            pl.BlockSpec((OUTPROJ_TM, D_MODEL), lambda i: (i, 0)),
            pl.BlockSpec((OUTPROJ_TM, D_MODEL), lambda i: (i, 0)),
            pl.BlockSpec((D_MODEL, D_MODEL), lambda i: (0, 0), pipeline_mode=pl.Buffered(1)),
            pl.BlockSpec((1, D_MODEL), lambda i: (0, 0)),
        ],
        out_specs=(
            pl.BlockSpec((OUTPROJ_TM, D_MODEL), lambda i: (i, 0)),
            pl.BlockSpec((OUTPROJ_TM, D_MODEL), lambda i: (i, 0)),
        ),
        scratch_shapes=[pltpu.VMEM((D_MODEL, D_MODEL), BF16)],
        compiler_params=_cparams(("arbitrary",)),
        name="outproj",
    )(x2d, merged, w_out, norm_w)


FFN_TM = 1024
FFN_TF = 512
DOWN_TM = 512
DOWN_TN = 512


def _ffn_up_kernel(h_ref, wg_ref, wu_ref, a_ref):
    h = h_ref[...]
    g = jnp.dot(h, wg_ref[...].astype(BF16), preferred_element_type=F32)
    u = jnp.dot(h, wu_ref[...].astype(BF16), preferred_element_type=F32)
    a_ref[...] = (g * jax.nn.sigmoid(g) * u).astype(a_ref.dtype)


def _ffn_up(h2, w_gate, w_up):
    m = h2.shape[0]
    return pl.pallas_call(
        _ffn_up_kernel,
        out_shape=jax.ShapeDtypeStruct((m, FFN_HIDDEN), BF16),
        grid=(m // FFN_TM, FFN_HIDDEN // FFN_TF),
        in_specs=[
            pl.BlockSpec((FFN_TM, D_MODEL), lambda i, j: (i, 0)),
            pl.BlockSpec((D_MODEL, FFN_TF), lambda i, j: (0, j)),
            pl.BlockSpec((D_MODEL, FFN_TF), lambda i, j: (0, j)),
        ],
        out_specs=pl.BlockSpec((FFN_TM, FFN_TF), lambda i, j: (i, j)),
        compiler_params=_cparams(("parallel", "parallel")),
        name="ffn_up",
    )(h2, w_gate, w_up)


def _ffn_down_kernel(a_ref, w_ref, x1_ref, o_ref, wbf_ref):
    @pl.when(pl.program_id(1) == 0)
    def _():
        wbf_ref[...] = w_ref[...].astype(BF16)

    o_ref[...] = x1_ref[...] + jnp.dot(a_ref[...], wbf_ref[...], preferred_element_type=F32)


def _ffn_down(a, w_down, x1):
    m = a.shape[0]
    return pl.pallas_call(
        _ffn_down_kernel,
        out_shape=jax.ShapeDtypeStruct((m, D_MODEL), F32),
        grid=(D_MODEL // DOWN_TN, m // DOWN_TM),
        in_specs=[
            pl.BlockSpec((DOWN_TM, FFN_HIDDEN), lambda j, i: (i, 0)),
            pl.BlockSpec((FFN_HIDDEN, DOWN_TN), lambda j, i: (0, j)),
            pl.BlockSpec((DOWN_TM, DOWN_TN), lambda j, i: (i, j)),
        ],
        out_specs=pl.BlockSpec((DOWN_TM, DOWN_TN), lambda j, i: (i, j)),
        scratch_shapes=[pltpu.VMEM((FFN_HIDDEN, DOWN_TN), BF16)],
        compiler_params=_cparams(("parallel", "arbitrary")),
        name="ffn_down",
    )(a, w_down, x1)


def _rope_tables(seq):
    half = ATT_HEAD_DIM // 2
    inv_freq = ROPE_THETA ** (-jnp.arange(half, dtype=F32) / half)
    ang = jnp.arange(seq, dtype=jnp.int32).astype(F32)[:, None] * inv_freq[None, :]
    cos = jnp.cos(ang)
    sin = jnp.sin(ang)
    reps = ATT_PAIR // ATT_HEAD_DIM
    cos_t = jnp.tile(jnp.concatenate([cos, cos], axis=-1), (1, reps))
    sin_t = jnp.tile(jnp.concatenate([-sin, sin], axis=-1), (1, reps))
    return cos_t, sin_t


def _layer(x, mix_norm_w, w_in, conv_w, conv_b, dt_bias, a_log, d_skip, ssm_norm_w, w_ssm_proj,
           q_norm_w, k_norm_w, lambda_q1, lambda_k1, lambda_q2, lambda_k2, subln_w, w_attn_proj,
           w_out, ffn_norm_w, w_ffn_gate, w_ffn_up, w_ffn_down):
    b, s, d = x.shape
    m = b * s
    x2d = x.reshape(m, d)

    dt0 = SSM_COLS
    dt1 = dt0 + SSM_HEADS
    norm_w = mix_norm_w.reshape(1, d)
    w_t = w_in.T
    proj_ssm, dt_raw = _inproj(x2d, norm_w, w_t, 0, SSM_COLS, "inproj_ssm", dt_row0=dt0)
    (proj_att,) = _inproj(x2d, norm_w, w_t, dt1, ATT_COLS, "inproj_att")
    proj3 = proj_ssm.reshape(b, s, SSM_COLS)
    proj_att3 = proj_att.reshape(b, s, ATT_COLS)

    xact = _xconv(proj3, conv_w, conv_b.reshape(1, SSM_CONV_CH))
    dt_g = dt_raw[:, :SSM_HEADS].reshape(b, s, SSM_GROUPS, HPG)
    dt_col = jnp.transpose(dt_g, (0, 2, 1, 3))
    dt_row = jnp.transpose(dt_g, (0, 2, 3, 1))
    per_head_c = lambda v: v.astype(F32).reshape(SSM_GROUPS, 1, HPG)
    per_head_r = lambda v: v.astype(F32).reshape(SSM_GROUPS, HPG, 1)
    dskip_row = jnp.repeat(d_skip.astype(F32), SSM_HEAD_DIM).reshape(1, SSM_D_INNER)
    y_ssm = _ssd(xact, proj3, dt_col, dt_row,
                 per_head_c(dt_bias), per_head_c(a_log), per_head_r(dt_bias), per_head_r(a_log),
                 dskip_row, ssm_norm_w.reshape(1, SSM_D_INNER))

    cos_t, sin_t = _rope_tables(s)
    pair_w = lambda v: jnp.tile(v.astype(F32), ATT_PAIR // ATT_HEAD_DIM).reshape(1, ATT_PAIR)
    lam_w = lambda v: v.astype(F32).reshape(1, ATT_HEAD_DIM)
    y_att = _attn(proj_att3, cos_t, sin_t, pair_w(q_norm_w), pair_w(k_norm_w),
                  lam_w(lambda_q1), lam_w(lambda_k1), lam_w(lambda_q2), lam_w(lambda_k2),
                  subln_w.astype(F32).reshape(1, ATT_PAIR))

    merged = _merge(y_ssm.reshape(m, SSM_D_INNER), w_ssm_proj, y_att.reshape(m, ATT_V), w_attn_proj, proj_att)
    x1, h2 = _outproj(x2d, merged, w_out, ffn_norm_w.reshape(1, d))
    a = _ffn_up(h2, w_ffn_gate, w_ffn_up)
    out = _ffn_down(a, w_ffn_down, x1)
    return out.reshape(b, s, d)


def kernel(x, mix_norm_w, w_in, conv_w, conv_b, dt_bias, a_log, d_skip, ssm_norm_w, w_ssm_proj, q_norm_w, k_norm_w, lambda_q1, lambda_k1, lambda_q2, lambda_k2, subln_w, w_attn_proj, w_out, ffn_norm_w, w_ffn_gate, w_ffn_up, w_ffn_down):
    depth = w_in.shape[0]
    assert depth == 1, "LAM_INIT is derived for a single layer"
    layer = lambda v: v[0]
    return _layer(x, *(layer(v) for v in (
        mix_norm_w, w_in, conv_w, conv_b, dt_bias, a_log, d_skip, ssm_norm_w, w_ssm_proj, q_norm_w, k_norm_w,
        lambda_q1, lambda_k1, lambda_q2, lambda_k2, subln_w, w_attn_proj, w_out, ffn_norm_w, w_ffn_gate,
        w_ffn_up, w_ffn_down)))
```

```python
import math

import jax
import jax.numpy as jnp
from jax import lax
from jax.experimental import pallas as pl
from jax.experimental.pallas import tpu as pltpu

F32 = jnp.float32
BF16 = jnp.bfloat16

D_MODEL = 2048
SSM_D_INNER = 2 * D_MODEL
SSM_HEAD_DIM = 64
SSM_HEADS = SSM_D_INNER // SSM_HEAD_DIM
SSM_GROUPS = 8
SSM_HEADS_PER_GROUP = SSM_HEADS // SSM_GROUPS
SSM_GROUP_WIDTH = SSM_D_INNER // SSM_GROUPS
SSM_STATE = 128
SSM_CONV = 4
SSM_CHUNK = 256
SSM_CONV_CH = SSM_D_INNER + 2 * SSM_GROUPS * SSM_STATE
ATT_HEAD_DIM = 64
ATT_HEADS = D_MODEL // (2 * ATT_HEAD_DIM)
ATT_PAIR = 2 * ATT_HEAD_DIM
ATT_QK = 2 * ATT_HEADS * ATT_HEAD_DIM
ATT_V = ATT_HEADS * 2 * ATT_HEAD_DIM
ROPE_THETA = 10000.0
FFN_HIDDEN = (((8 * D_MODEL + 2) // 3 + 255) // 256) * 256
NORM_EPS = 1e-6
SUBLN_EPS = 1e-5
LAM_INIT = 0.8 - 0.6 * math.exp(-0.3 * 0)
LOG2E = math.log2(math.e)

COL_Z = 0
COL_X = COL_Z + SSM_D_INNER
DT_ROW0 = COL_X + SSM_CONV_CH
COL_Q = DT_ROW0
COL_K = COL_Q + ATT_QK
COL_V = COL_K + ATT_QK
COL_GS = COL_V + ATT_V
COL_GA = COL_GS + D_MODEL
PROJ_COLS = COL_GA + D_MODEL
LANE = 128
SUBLANES = 8
DT_PAD = LANE

VMEM_LIMIT = 56 * 1024 * 1024


def _cparams(sem):
    return pltpu.CompilerParams(dimension_semantics=sem, vmem_limit_bytes=VMEM_LIMIT)


def _silu(u):
    h = 0.5 * u
    return h * jnp.tanh(h) + h


INPROJ_TM = 1024
INPROJ_TN = 1024
NORM_ROWS = 256


def _rms_rows(x, w, eps):
    ms = jnp.mean(x * x, axis=-1, keepdims=True)
    return x * lax.rsqrt(ms + eps) * w


def _norm_tile(x_ref, nw_ref, h_ref):
    def body(i, carry):
        rows = pl.ds(pl.multiple_of(i * NORM_ROWS, NORM_ROWS), NORM_ROWS)
        h_ref[rows, :] = _rms_rows(x_ref[rows, :], nw_ref[...], NORM_EPS).astype(BF16)
        return carry
    lax.fori_loop(0, INPROJ_TM // NORM_ROWS, body, 0)


def _project_rows(h_ref, wt_ref):
    wt = wt_ref[...].astype(BF16)
    return lax.dot_general(h_ref[...], wt, (((1,), (1,)), ((), ())), preferred_element_type=F32)


def _inproj_kernel(x_ref, nw_ref, wt_ref, wdt_ref, out_ref, dt_ref, h_ref):
    @pl.when(pl.program_id(1) == 0)
    def _():
        _norm_tile(x_ref, nw_ref, h_ref)
        dt_ref[...] = _project_rows(h_ref, wdt_ref)

    out_ref[...] = _project_rows(h_ref, wt_ref).astype(out_ref.dtype)


def _inproj(x2d, norm_w, w_t):
    m = x2d.shape[0]
    tiles_before_dt = DT_ROW0 // INPROJ_TN
    assert DT_ROW0 % INPROJ_TN == 0 and SSM_HEADS % SUBLANES == 0

    def w_rows(i, j):
        row0 = j * INPROJ_TN + jnp.where(j >= tiles_before_dt, SSM_HEADS, 0)
        return pl.multiple_of(row0, SUBLANES), 0

    return pl.pallas_call(
        _inproj_kernel,
        out_shape=(jax.ShapeDtypeStruct((m, PROJ_COLS), BF16), jax.ShapeDtypeStruct((m, DT_PAD), F32)),
        grid=(m // INPROJ_TM, PROJ_COLS // INPROJ_TN),
        in_specs=[
            pl.BlockSpec((INPROJ_TM, D_MODEL), lambda i, j: (i, 0)),
            pl.BlockSpec((1, D_MODEL), lambda i, j: (0, 0)),
            pl.BlockSpec((pl.Element(INPROJ_TN), pl.Element(D_MODEL)), w_rows),
            pl.BlockSpec((pl.Element(DT_PAD), pl.Element(D_MODEL)), lambda i, j: (DT_ROW0, 0)),
        ],
        out_specs=(
            pl.BlockSpec((INPROJ_TM, INPROJ_TN), lambda i, j: (i, j)),
            pl.BlockSpec((INPROJ_TM, DT_PAD), lambda i, j: (i, 0)),
        ),
        scratch_shapes=[pltpu.VMEM((INPROJ_TM, D_MODEL), BF16)],
        compiler_params=_cparams(("parallel", "arbitrary")),
        name="inproj",
    )(x2d, norm_w, w_t, w_t)


L = SSM_CHUNK
TAIL = 16
TAPS_BACK = SSM_CONV - 1
CONV_TW = 2048
CONV_SLAB = 512
CONV_SUB = 4


def _xconv_kernel(cur_ref, prev_ref, w_ref, b_ref, shift_ref, tshift_ref, o_ref):
    has_prev = pl.program_id(1) > 0
    for sub, s0 in [(sub, s0) for sub in range(CONV_SUB) for s0 in range(0, CONV_TW, CONV_SLAB)]:
        cols = slice(s0, s0 + CONV_SLAB)
        r0 = sub * L
        cur_bf = cur_ref[r0:r0 + L, cols]
        if sub == 0:
            prev_bf = jnp.where(has_prev, prev_ref[:, cols], jnp.zeros((TAIL, CONV_SLAB), BF16))
        else:
            prev_bf = cur_ref[r0 - TAIL:r0, cols]
        cur = cur_bf.astype(F32)
        shifted = jnp.dot(shift_ref[...], cur_bf, preferred_element_type=F32)
        carried = jnp.dot(tshift_ref[...], prev_bf, preferred_element_type=F32)
        w = 0.5 * w_ref[:, cols]
        h = 0.5 * b_ref[:, cols] + w[TAPS_BACK:SSM_CONV, :] * cur
        top = jnp.zeros((TAIL, CONV_SLAB), F32)
        for k in range(TAPS_BACK):
            h = h + w[k:k + 1, :] * shifted[k * L:(k + 1) * L, :]
            top = top + w[k:k + 1, :] * carried[k * TAIL:(k + 1) * TAIL, :]
        h = jnp.concatenate([h[0:TAIL, :] + top, h[TAIL:L, :]], axis=0)
        o_ref[r0:r0 + L, cols] = (h * jnp.tanh(h) + h).astype(o_ref.dtype)


def _xconv(proj3, conv_w, conv_b):
    b, s, _ = proj3.shape
    rows = CONV_SUB * L
    xblk = COL_X // CONV_TW
    tail_per_block = rows // TAIL
    shift = jnp.concatenate([jnp.eye(L, L, k=-(TAPS_BACK - k), dtype=BF16) for k in range(TAPS_BACK)], axis=0)
    tshift = jnp.concatenate(
        [jnp.eye(TAIL, TAIL, k=TAIL - (TAPS_BACK - k), dtype=BF16) for k in range(TAPS_BACK)], axis=0)
    return pl.pallas_call(
        _xconv_kernel,
        out_shape=jax.ShapeDtypeStruct((b, s, SSM_CONV_CH), BF16),
        grid=(b, s // rows, SSM_CONV_CH // CONV_TW),
        in_specs=[
            pl.BlockSpec((None, rows, CONV_TW), lambda bi, c, j: (bi, c, xblk + j)),
            pl.BlockSpec((None, TAIL, CONV_TW),
                         lambda bi, c, j: (bi, jnp.maximum(c * tail_per_block - 1, 0), xblk + j)),
            pl.BlockSpec((SSM_CONV, CONV_TW), lambda bi, c, j: (0, j)),
            pl.BlockSpec((1, CONV_TW), lambda bi, c, j: (0, j)),
            pl.BlockSpec((TAPS_BACK * L, L), lambda bi, c, j: (0, 0)),
            pl.BlockSpec((TAPS_BACK * TAIL, TAIL), lambda bi, c, j: (0, 0)),
        ],
        out_specs=pl.BlockSpec((None, rows, CONV_TW), lambda bi, c, j: (bi, c, j)),
        compiler_params=_cparams(("parallel", "parallel", "parallel")),
        name="xconv",
    )(proj3, proj3, conv_w, conv_b, shift, tshift)


HALF = L // 2
HPG = SSM_HEADS_PER_GROUP
GW = SSM_GROUP_WIDTH
PAIRS = HPG // 2
SSD_SUB = 2


def _split3(x):
    hi = x.astype(BF16)
    rest = x - hi.astype(F32)
    mid = rest.astype(BF16)
    lo = (rest - mid.astype(F32)).astype(BF16)
    return hi, mid, lo


def _ssd_kernel(x_ref, b_ref, c_ref, z_ref, dtc_ref, dtr_ref,
                biasc_ref, alogc_ref, biasr_ref, alogr_ref, dskip_ref, nw_ref, tri_ref, trit_ref,
                y_ref, state_ref, xde_ref, ycat_ref):
    @pl.when(pl.program_id(2) == 0)
    def _():
        state_ref[...] = jnp.zeros_like(state_ref)

    for sub in range(SSD_SUB):
        _ssd_chunk(slice(sub * L, (sub + 1) * L), x_ref, b_ref, c_ref, z_ref, dtc_ref, dtr_ref,
                   biasc_ref, alogc_ref, biasr_ref, alogr_ref, dskip_ref, nw_ref, tri_ref, trit_ref,
                   y_ref, state_ref, xde_ref.at[sub], ycat_ref.at[sub])


def _ssd_chunk(rows, x_ref, b_ref, c_ref, z_ref, dtc_ref, dtr_ref,
               biasc_ref, alogc_ref, biasr_ref, alogr_ref, dskip_ref, nw_ref, tri_ref, trit_ref,
               y_ref, state_ref, xde_ref, ycat_ref):
    bm_bf = b_ref[rows, :]
    cm_bf = c_ref[rows, :]
    bt_bf = bm_bf.astype(F32).T.astype(BF16)

    dtc = jax.nn.softplus(dtc_ref[rows, :] + biasc_ref[...])
    dtr = jax.nn.softplus(dtr_ref[:, rows] + biasr_ref[...])
    dac = dtc * (-LOG2E * jnp.exp(alogc_ref[...]))
    dar = dtr * (-LOG2E * jnp.exp(alogr_ref[...]))
    acs_c = sum(jnp.dot(tri_ref[...], part, preferred_element_type=F32) for part in _split3(dac))
    acs_r = sum(jnp.dot(part, trit_ref[...], preferred_element_type=F32) for part in _split3(dar))
    a_last = acs_c[L - 1:L, :]
    arow = acs_r - jnp.log2(dtr)
    g_end = jnp.log2(dtc) - acs_c + a_last
    dec_chunk = jnp.exp2(a_last)

    cb = lax.dot_general(cm_bf, bm_bf, (((1,), (1,)), ((), ())), preferred_element_type=F32)
    y_off = jnp.dot(cm_bf, state_ref[...].astype(BF16), preferred_element_type=F32)
    cb00 = cb[0:HALF, 0:HALF]
    cb10 = cb[HALF:L, 0:HALF]
    cb11 = cb[HALF:L, HALF:L]
    causal = (lax.broadcasted_iota(jnp.int32, (HALF, HALF), 0)
              >= lax.broadcasted_iota(jnp.int32, (HALF, HALF), 1))
    lane = lax.broadcasted_iota(jnp.int32, (1, LANE), 1)
    first = lane < SSM_HEAD_DIM

    for p in range(PAIRS):
        lanes = slice(p * LANE, (p + 1) * LANE)
        xs_bf = x_ref[rows, lanes]
        xs_p = xs_bf.astype(F32)
        a_cols = [jnp.broadcast_to(acs_c[:, r:r + 1], (L, LANE)) for r in (2 * p, 2 * p + 1)]
        a_pair = jnp.where(first, a_cols[0], a_cols[1])
        g_pair = jnp.where(first, g_end[:, 2 * p:2 * p + 1], g_end[:, 2 * p + 1:2 * p + 2])
        xde_ref[:, lanes] = (xs_p * jnp.exp2(g_pair)).astype(BF16)
        y_heads = []
        for h in range(2):
            a_col = a_cols[h]
            a_row = arow[2 * p + h:2 * p + h + 1, :]
            d00 = jnp.exp2(jnp.where(causal, a_col[0:HALF, :] - a_row[:, 0:HALF], -jnp.inf))
            d10 = jnp.exp2(a_col[HALF:L, :] - a_row[:, 0:HALF])
            d11 = jnp.exp2(jnp.where(causal, a_col[HALF:L, :] - a_row[:, HALF:L], -jnp.inf))
            l_top = (cb00 * d00).astype(BF16)
            l_bot = jnp.concatenate([cb10 * d10, cb11 * d11], axis=1).astype(BF16)
            y_heads.append(jnp.concatenate(
                [jnp.dot(l_top, xs_bf[0:HALF, :], preferred_element_type=F32),
                 jnp.dot(l_bot, xs_bf, preferred_element_type=F32)], axis=0))
        y_diag = jnp.where(first, y_heads[0], y_heads[1])
        ycat_ref[:, lanes] = y_diag + y_off[:, lanes] * jnp.exp2(a_pair) + dskip_ref[:, lanes] * xs_p

    upd = jnp.dot(bt_bf, xde_ref[...], preferred_element_type=F32)
    for p in range(PAIRS):
        lanes = slice(p * LANE, (p + 1) * LANE)
        dec_p = jnp.where(first, dec_chunk[:, 2 * p:2 * p + 1], dec_chunk[:, 2 * p + 1:2 * p + 2])
        state_ref[:, lanes] = state_ref[:, lanes] * dec_p + upd[:, lanes]

    y = ycat_ref[...] * _silu(z_ref[rows, :].astype(F32))
    y = y * lax.rsqrt(jnp.mean(y * y, axis=-1, keepdims=True) + SUBLN_EPS)
    y_ref[rows, :] = (y * nw_ref[...]).astype(y_ref.dtype)


def _ssd(xact, proj3, dt_col, dt_row, bias_c, alog_c, bias_r, alog_r, dskip_row, norm_w):
    b, s, _ = proj3.shape
    nc = s // L
    bblk = SSM_D_INNER // SSM_STATE
    cblk = bblk + SSM_GROUPS
    tri = jnp.tril(jnp.ones((L, L), BF16))
    const = lambda shape: pl.BlockSpec(shape, lambda bi, g, c: (0, 0))
    rows = SSD_SUB * L
    in_specs = [
        pl.BlockSpec((None, rows, GW), lambda bi, g, c: (bi, c, g)),
        pl.BlockSpec((None, rows, SSM_STATE), lambda bi, g, c: (bi, c, bblk + g)),
        pl.BlockSpec((None, rows, SSM_STATE), lambda bi, g, c: (bi, c, cblk + g)),
        pl.BlockSpec((None, rows, GW), lambda bi, g, c: (bi, c, g)),
        pl.BlockSpec((None, None, rows, HPG), lambda bi, g, c: (bi, g, c, 0)),
        pl.BlockSpec((None, None, HPG, rows), lambda bi, g, c: (bi, g, 0, c)),
        pl.BlockSpec((None, 1, HPG), lambda bi, g, c: (g, 0, 0)),
        pl.BlockSpec((None, 1, HPG), lambda bi, g, c: (g, 0, 0)),
        pl.BlockSpec((None, HPG, 1), lambda bi, g, c: (g, 0, 0)),
        pl.BlockSpec((None, HPG, 1), lambda bi, g, c: (g, 0, 0)),
        pl.BlockSpec((1, GW), lambda bi, g, c: (0, g)),
        pl.BlockSpec((1, GW), lambda bi, g, c: (0, g)),
        const((L, L)),
        const((L, L)),
    ]
    return pl.pallas_call(
        _ssd_kernel,
        out_shape=jax.ShapeDtypeStruct((b, s, SSM_D_INNER), BF16),
        grid=(b, SSM_GROUPS, nc // SSD_SUB),
        in_specs=in_specs,
        out_specs=pl.BlockSpec((None, rows, GW), lambda bi, g, c: (bi, c, g)),
        scratch_shapes=[
            pltpu.VMEM((SSM_STATE, GW), F32),
            pltpu.VMEM((SSD_SUB, L, GW), BF16),
            pltpu.VMEM((SSD_SUB, L, GW), F32),
        ],
        compiler_params=_cparams(("parallel", "parallel", "arbitrary")),
        name="ssd",
    )(xact, xact, xact, proj3, dt_col, dt_row, bias_c, alog_c, bias_r, alog_r, dskip_row, norm_w, tri, tri.T)


ATT_TQ = 512
ATT_TK = 512
ROPE_ROWS = 1024
ATT_HEADS_PER_PAIR = 2
ATT_HQ = ATT_TQ // 2


def _norm_rope(x, w, cos, sin_signed, group_ones):
    ss = jnp.dot((x * x).astype(BF16), group_ones, preferred_element_type=F32)
    xn = x * lax.rsqrt(ss * (1.0 / ATT_HEAD_DIM) + NORM_EPS) * w
    lane = lax.broadcasted_iota(jnp.int32, (1, ATT_PAIR), 1)
    low_half = (lane % ATT_HEAD_DIM) < (ATT_HEAD_DIM // 2)
    half = ATT_HEAD_DIM // 2
    rot = jnp.where(low_half, pltpu.roll(xn, ATT_PAIR - half, 1), pltpu.roll(xn, half, 1))
    return xn * cos + rot * sin_signed


def _fold_lanes(x, op):
    out = x[:, 0:LANE]
    for t in range(1, x.shape[1] // LANE):
        out = op(out, x[:, t * LANE:(t + 1) * LANE])
    return out


def _nt_dot(a, b):
    return lax.dot_general(a, b, (((1,), (1,)), ((), ())), preferred_element_type=F32)


def _attn_kernel(q_ref, k_ref, v_ref, cos_ref, sin_ref, qw_ref, kw_ref,
                  lq1_ref, lk1_ref, lq2_ref, lk2_ref, sw_ref, o_ref,
                  krot_ref, qrot_ref, s_ref, sd_ref, stat_ref, acc_ref):
    seq = k_ref.shape[0]
    ri = lax.broadcasted_iota(jnp.int32, (ATT_PAIR, ATT_PAIR), 0) // ATT_HEAD_DIM
    ci = lax.broadcasted_iota(jnp.int32, (ATT_PAIR, ATT_PAIR), 1) // ATT_HEAD_DIM
    group_ones = (ri == ci).astype(BF16)
    lane = lax.broadcasted_iota(jnp.int32, (1, ATT_PAIR), 1)

    def prep(i, carry):
        rows = pl.ds(pl.multiple_of(i * ROPE_ROWS, ROPE_ROWS), ROPE_ROWS)
        cos = cos_ref[rows, :]
        sin = sin_ref[rows, :]
        kr = _norm_rope(k_ref[rows, :].astype(F32), kw_ref[...], cos, sin, group_ones)
        krot_ref[rows, :] = kr.astype(BF16)
        qr = _norm_rope(q_ref[rows, :].astype(F32), qw_ref[...], cos, sin, group_ones)
        qr = qr * (LOG2E * ATT_HEAD_DIM ** -0.5)
        qrot_ref[0, rows, :] = jnp.where(lane < ATT_HEAD_DIM, qr, 0.0).astype(BF16)
        qrot_ref[1, rows, :] = jnp.where(lane >= ATT_HEAD_DIM, qr, 0.0).astype(BF16)
        return carry
    lax.fori_loop(0, seq // ROPE_ROWS, prep, 0, unroll=True)

    lam =(jnp.exp(jnp.sum(lq1_ref[...] * lk1_ref[...], axis=-1, keepdims=True))
           - jnp.exp(jnp.sum(lq2_ref[...] * lk2_ref[...], axis=-1, keepdims=True)) + LAM_INIT)
    tri_top = (lax.broadcasted_iota(jnp.int32, (ATT_HQ, ATT_HQ), 0)
               >= lax.broadcasted_iota(jnp.int32, (ATT_HQ, ATT_HQ), 1))
    tri_bot = (lax.broadcasted_iota(jnp.int32, (ATT_HQ, ATT_TK), 0) + ATT_HQ
               >= lax.broadcasted_iota(jnp.int32, (ATT_HQ, ATT_TK), 1))
    top = slice(0, ATT_HQ)
    bot = slice(ATT_HQ, ATT_TQ)

    for qi in range(seq // ATT_TQ):
        q0 = qi * ATT_TQ
        kd = krot_ref[q0:q0 + ATT_TQ, :]
        for h in range(ATT_HEADS_PER_PAIR):
            qh = qrot_ref[h, q0:q0 + ATT_TQ, :]
            s_top = jnp.where(tri_top, _nt_dot(qh[top, :], kd[top, :]), -jnp.inf)
            s_bot = jnp.where(tri_bot, _nt_dot(qh[bot, :], kd), -jnp.inf)
            sd_ref[h, top, 0:ATT_HQ] = s_top
            sd_ref[h, bot, :] = s_bot
            stat_ref[h, top, :] = _fold_lanes(s_top, jnp.maximum)
            stat_ref[h, bot, :] = _fold_lanes(s_bot, jnp.maximum)

        def pass1(j, carry):
            kj = krot_ref[pl.ds(pl.multiple_of(j * ATT_TK, ATT_TK), ATT_TK), :]
            for h in range(ATT_HEADS_PER_PAIR):
                s = _nt_dot(qrot_ref[h, q0:q0 + ATT_TQ, :], kj)
                s_ref[h, j] = s
                stat_ref[h] = jnp.maximum(stat_ref[h], _fold_lanes(s, jnp.maximum))
            return carry
        if qi > 0:
            lax.fori_loop(0, qi, pass1, 0, unroll=True)

        for h in range(ATT_HEADS_PER_PAIR):
            row_max = jnp.max(stat_ref[h], axis=-1, keepdims=True)
            stat_ref[ATT_HEADS_PER_PAIR + h] = jnp.broadcast_to(row_max, (ATT_TQ, LANE))

        vd = v_ref[q0:q0 + ATT_TQ, :]
        for h in range(ATT_HEADS_PER_PAIR):
            m = stat_ref[ATT_HEADS_PER_PAIR + h]
            p_top = jnp.concatenate(
                [jnp.exp2(sd_ref[h, top, t * LANE:(t + 1) * LANE] - m[top, :]) for t in range(ATT_HQ // LANE)],
                axis=1)
            p_bot = jnp.concatenate(
                [jnp.exp2(sd_ref[h, bot, t * LANE:(t + 1) * LANE] - m[bot, :]) for t in range(ATT_TK // LANE)],
                axis=1)
            stat_ref[h, top, :] = _fold_lanes(p_top, jnp.add)
            stat_ref[h, bot, :] = _fold_lanes(p_bot, jnp.add)
            acc_ref[h, top, :] = jnp.dot(p_top.astype(BF16), vd[top, :], preferred_element_type=F32)
            acc_ref[h, bot, :] = jnp.dot(p_bot.astype(BF16), vd, preferred_element_type=F32)

        def pass2(j, carry):
            vj = v_ref[pl.ds(pl.multiple_of(j * ATT_TK, ATT_TK), ATT_TK), :]
            for h in range(ATT_HEADS_PER_PAIR):
                m = stat_ref[ATT_HEADS_PER_PAIR + h]
                s = s_ref[h, j]
                p = jnp.concatenate(
                    [jnp.exp2(s[:, t * LANE:(t + 1) * LANE] - m) for t in range(ATT_TK // LANE)], axis=1)
                stat_ref[h] = stat_ref[h] + _fold_lanes(p, jnp.add)
                acc_ref[h] = acc_ref[h] + jnp.dot(p.astype(BF16), vj, preferred_element_type=F32)
            return carry
        if qi > 0:
            lax.fori_loop(0, qi, pass2, 0, unroll=True)

        l1 = jnp.sum(stat_ref[0], axis=-1, keepdims=True)
        l2 = jnp.sum(stat_ref[1], axis=-1, keepdims=True)
        o = acc_ref[0] / l1 - lam * (acc_ref[1] / l2)
        o = o * lax.rsqrt(jnp.mean(o * o, axis=-1, keepdims=True) + SUBLN_EPS) * sw_ref[...]
        o_ref[q0:q0 + ATT_TQ, :] = (o * (1.0 - LAM_INIT)).astype(o_ref.dtype)


def _attn(proj3, cos_t, sin_t, qw, kw, lq1, lk1, lq2, lk2, sw):
    b, s, _ = proj3.shape
    qblk = COL_Q // ATT_PAIR
    kblk = COL_K // ATT_PAIR
    vblk = COL_V // ATT_PAIR
    small = lambda width: pl.BlockSpec((1, width), lambda bi, h: (0, 0))
    return pl.pallas_call(
        _attn_kernel,
        out_shape=jax.ShapeDtypeStruct((b, s, ATT_V), BF16),
        grid=(b, ATT_HEADS),
        in_specs=[
            pl.BlockSpec((None, s, ATT_PAIR), lambda bi, h: (bi, 0, qblk + h)),
            pl.BlockSpec((None, s, ATT_PAIR), lambda bi, h: (bi, 0, kblk + h)),
            pl.BlockSpec((None, s, ATT_PAIR), lambda bi, h: (bi, 0, vblk + h)),
            pl.BlockSpec((s, ATT_PAIR), lambda bi, h: (0, 0)),
            pl.BlockSpec((s, ATT_PAIR), lambda bi, h: (0, 0)),
            small(ATT_PAIR), small(ATT_PAIR),
            small(ATT_HEAD_DIM), small(ATT_HEAD_DIM), small(ATT_HEAD_DIM), small(ATT_HEAD_DIM),
            small(ATT_PAIR),
        ],
        out_specs=pl.BlockSpec((None, s, ATT_PAIR), lambda bi, h: (bi, 0, h)),
        scratch_shapes=[
            pltpu.VMEM((s, ATT_PAIR), BF16),
            pltpu.VMEM((ATT_HEADS_PER_PAIR, s, ATT_PAIR), BF16),
            pltpu.VMEM((ATT_HEADS_PER_PAIR, s // ATT_TK - 1, ATT_TQ, ATT_TK), F32),
            pltpu.VMEM((ATT_HEADS_PER_PAIR, ATT_TQ, ATT_TK), F32),
            pltpu.VMEM((2 * ATT_HEADS_PER_PAIR, ATT_TQ, LANE), F32),
            pltpu.VMEM((ATT_HEADS_PER_PAIR, ATT_TQ, ATT_PAIR), F32),
        ],
        compiler_params=_cparams(("parallel", "parallel")),
        name="attn",
    )(proj3, proj3, proj3, cos_t, sin_t, qw, kw, lq1, lk1, lq2, lk2, sw)


MERGE_TM = 512
MERGE_TN = 512


def _merge_kernel(ys_ref, ws_ref, ya_ref, wa_ref, gs_ref, ga_ref, o_ref, wsbf_ref, wabf_ref):
    @pl.when(pl.program_id(1) == 0)
    def _():
        wsbf_ref[...] = ws_ref[...].astype(BF16)
        wabf_ref[...] = wa_ref[...].astype(BF16)

    bs = jnp.dot(ys_ref[...], wsbf_ref[...], preferred_element_type=F32)
    ba = jnp.dot(ya_ref[...], wabf_ref[...], preferred_element_type=F32)
    gs = jax.nn.sigmoid(gs_ref[...].astype(F32))
    ga = jax.nn.sigmoid(ga_ref[...].astype(F32))
    o_ref[...] = (gs * bs + ga * ba).astype(o_ref.dtype)


def _merge(y_ssm, w_ssm, y_att, w_att, proj):
    m = y_ssm.shape[0]
    gsblk = COL_GS // MERGE_TN
    gablk = COL_GA // MERGE_TN
    return pl.pallas_call(
        _merge_kernel,
        out_shape=jax.ShapeDtypeStruct((m, D_MODEL), BF16),
        grid=(D_MODEL // MERGE_TN, m // MERGE_TM),
        in_specs=[
            pl.BlockSpec((MERGE_TM, SSM_D_INNER), lambda j, i: (i, 0)),
            pl.BlockSpec((SSM_D_INNER, MERGE_TN), lambda j, i: (0, j)),
            pl.BlockSpec((MERGE_TM, ATT_V), lambda j, i: (i, 0)),
            pl.BlockSpec((ATT_V, MERGE_TN), lambda j, i: (0, j)),
            pl.BlockSpec((MERGE_TM, MERGE_TN), lambda j, i: (i, gsblk + j)),
            pl.BlockSpec((MERGE_TM, MERGE_TN), lambda j, i: (i, gablk + j)),
        ],
        out_specs=pl.BlockSpec((MERGE_TM, MERGE_TN), lambda j, i: (i, j)),
        scratch_shapes=[
            pltpu.VMEM((SSM_D_INNER, MERGE_TN), BF16),
            pltpu.VMEM((ATT_V, MERGE_TN), BF16),
        ],
        compiler_params=_cparams(("parallel", "arbitrary")),
        name="merge",
    )(y_ssm, w_ssm, y_att, w_att, proj, proj)


OUTPROJ_TM = 512


def _outproj_kernel(x_ref, m_ref, w_ref, nw_ref, x1_ref, h2_ref, wbf_ref):
    @pl.when(pl.program_id(0) == 0)
    def _():
        wbf_ref[...] = w_ref[...].astype(BF16)

    x1 = x_ref[...] + jnp.dot(m_ref[...], wbf_ref[...], preferred_element_type=F32)
    x1_ref[...] = x1
    h2_ref[...] = _rms_rows(x1, nw_ref[...], NORM_EPS).astype(h2_ref.dtype)


def _outproj(x2d, merged, w_out, norm_w):
    m = x2d.shape[0]
    return pl.pallas_call(
        _outproj_kernel,
        out_shape=(jax.ShapeDtypeStruct((m, D_MODEL), F32), jax.ShapeDtypeStruct((m, D_MODEL), BF16)),
        grid=(m // OUTPROJ_TM,),
        in_specs=[
            pl.BlockSpec((OUTPROJ_TM, D_MODEL), lambda i: (i, 0)),
            pl.BlockSpec((OUTPROJ_TM, D_MODEL), lambda i: (i, 0)),
            pl.BlockSpec((D_MODEL, D_MODEL), lambda i: (0, 0), pipeline_mode=pl.Buffered(1)),
            pl.BlockSpec((1, D_MODEL), lambda i: (0, 0)),
        ],
        out_specs=(
            pl.BlockSpec((OUTPROJ_TM, D_MODEL), lambda i: (i, 0)),
            pl.BlockSpec((OUTPROJ_TM, D_MODEL), lambda i: (i, 0)),
        ),
        scratch_shapes=[pltpu.VMEM((D_MODEL, D_MODEL), BF16)],
        compiler_params=_cparams(("arbitrary",)),
        name="outproj",
    )(x2d, merged, w_out, norm_w)


FFN_TM = 1024
FFN_TF = 512
DOWN_TM = 512
DOWN_TN = 512


def _ffn_up_kernel(h_ref, wg_ref, wu_ref, a_ref):
    h = h_ref[...]
    g = jnp.dot(h, wg_ref[...].astype(BF16), preferred_element_type=F32)
    u = jnp.dot(h, wu_ref[...].astype(BF16), preferred_element_type=F32)
    a_ref[...] = (g * jax.nn.sigmoid(g) * u).astype(a_ref.dtype)


def _ffn_up(h2, w_gate, w_up):
    m = h2.shape[0]
    return pl.pallas_call(
        _ffn_up_kernel,
        out_shape=jax.ShapeDtypeStruct((m, FFN_HIDDEN), BF16),
        grid=(m // FFN_TM, FFN_HIDDEN // FFN_TF),
        in_specs=[
            pl.BlockSpec((FFN_TM, D_MODEL), lambda i, j: (i, 0)),
            pl.BlockSpec((D_MODEL, FFN_TF), lambda i, j: (0, j)),
            pl.BlockSpec((D_MODEL, FFN_TF), lambda i, j: (0, j)),
        ],
        out_specs=pl.BlockSpec((FFN_TM, FFN_TF), lambda i, j: (i, j)),
        compiler_params=_cparams(("parallel", "parallel")),
        name="ffn_up",
    )(h2, w_gate, w_up)


def _ffn_down_kernel(a_ref, w_ref, x1_ref, o_ref, wbf_ref):
    @pl.when(pl.program_id(1) == 0)
    def _():
        wbf_ref[...] = w_ref[...].astype(BF16)

    o_ref[...] = x1_ref[...] + jnp.dot(a_ref[...], wbf_ref[...], preferred_element_type=F32)


def _ffn_down(a, w_down, x1):
    m = a.shape[0]
    return pl.pallas_call(
        _ffn_down_kernel,
        out_shape=jax.ShapeDtypeStruct((m, D_MODEL), F32),
        grid=(D_MODEL // DOWN_TN, m // DOWN_TM),
        in_specs=[
            pl.BlockSpec((DOWN_TM, FFN_HIDDEN), lambda j, i: (i, 0)),
            pl.BlockSpec((FFN_HIDDEN, DOWN_TN), lambda j, i: (0, j)),
            pl.BlockSpec((DOWN_TM, DOWN_TN), lambda j, i: (i, j)),
        ],
        out_specs=pl.BlockSpec((DOWN_TM, DOWN_TN), lambda j, i: (i, j)),
        scratch_shapes=[pltpu.VMEM((FFN_HIDDEN, DOWN_TN), BF16)],
        compiler_params=_cparams(("parallel", "arbitrary")),
        name="ffn_down",
    )(a, w_down, x1)


def _rope_tables(seq):
    half = ATT_HEAD_DIM // 2
    inv_freq = ROPE_THETA ** (-jnp.arange(half, dtype=F32) / half)
    ang = jnp.arange(seq, dtype=jnp.int32).astype(F32)[:, None] * inv_freq[None, :]
    cos = jnp.cos(ang)
    sin = jnp.sin(ang)
    reps = ATT_PAIR // ATT_HEAD_DIM
    cos_t = jnp.tile(jnp.concatenate([cos, cos], axis=-1), (1, reps))
    sin_t = jnp.tile(jnp.concatenate([-sin, sin], axis=-1), (1, reps))
    return cos_t, sin_t


def _layer(x, mix_norm_w, w_in, conv_w, conv_b, dt_bias, a_log, d_skip, ssm_norm_w, w_ssm_proj,
           q_norm_w, k_norm_w, lambda_q1, lambda_k1, lambda_q2, lambda_k2, subln_w, w_attn_proj,
           w_out, ffn_norm_w, w_ffn_gate, w_ffn_up, w_ffn_down):
    b, s, d = x.shape
    m = b * s
    x2d = x.reshape(m, d)

    proj, dt_raw = _inproj(x2d, mix_norm_w.reshape(1, d), w_in.T)
    proj3 = proj.reshape(b, s, PROJ_COLS)

    xact = _xconv(proj3, conv_w, conv_b.reshape(1, SSM_CONV_CH))
    dt_g = dt_raw[:, :SSM_HEADS].reshape(b, s, SSM_GROUPS, HPG)
    dt_col = jnp.transpose(dt_g, (0, 2, 1, 3))
    dt_row = jnp.transpose(dt_g, (0, 2, 3, 1))
    per_head_c = lambda v: v.astype(F32).reshape(SSM_GROUPS, 1, HPG)
    per_head_r = lambda v: v.astype(F32).reshape(SSM_GROUPS, HPG, 1)
    dskip_row = jnp.repeat(d_skip.astype(F32), SSM_HEAD_DIM).reshape(1, SSM_D_INNER)
    y_ssm = _ssd(xact, proj3, dt_col, dt_row,
                 per_head_c(dt_bias), per_head_c(a_log), per_head_r(dt_bias), per_head_r(a_log),
                 dskip_row, ssm_norm_w.reshape(1, SSM_D_INNER))

    cos_t, sin_t = _rope_tables(s)
    pair_w = lambda v: jnp.tile(v.astype(F32), ATT_PAIR // ATT_HEAD_DIM).reshape(1, ATT_PAIR)
    lam_w = lambda v: v.astype(F32).reshape(1, ATT_HEAD_DIM)
    y_att = _attn(proj3, cos_t, sin_t, pair_w(q_norm_w), pair_w(k_norm_w),
                  lam_w(lambda_q1), lam_w(lambda_k1), lam_w(lambda_q2), lam_w(lambda_k2),
                  subln_w.astype(F32).reshape(1, ATT_PAIR))

    merged = _merge(y_ssm.reshape(m, SSM_D_INNER), w_ssm_proj, y_att.reshape(m, ATT_V), w_attn_proj, proj)
    x1, h2 = _outproj(x2d, merged, w_out, ffn_norm_w.reshape(1, d))
    a = _ffn_up(h2, w_ffn_gate, w_ffn_up)
    out = _ffn_down(a, w_ffn_down, x1)
    return out.reshape(b, s, d)


def kernel(x, mix_norm_w, w_in, conv_w, conv_b, dt_bias, a_log, d_skip, ssm_norm_w, w_ssm_proj, q_norm_w, k_norm_w, lambda_q1, lambda_k1, lambda_q2, lambda_k2, subln_w, w_attn_proj, w_out, ffn_norm_w, w_ffn_gate, w_ffn_up, w_ffn_down):
    depth = w_in.shape[0]
    assert depth == 1, "LAM_INIT is derived for a single layer"
    layer = lambda v: v[0]
    return _layer(x, *(layer(v) for v in (
        mix_norm_w, w_in, conv_w, conv_b, dt_bias, a_log, d_skip, ssm_norm_w, w_ssm_proj, q_norm_w, k_norm_w,
        lambda_q1, lambda_k1, lambda_q2, lambda_k2, subln_w, w_attn_proj, w_out, ffn_norm_w, w_ffn_gate,
        w_ffn_up, w_ffn_down)))
```

```python
import math

import jax
import jax.numpy as jnp
from jax import lax
from jax.experimental import pallas as pl
from jax.experimental.pallas import tpu as pltpu

F32 = jnp.float32
BF16 = jnp.bfloat16

D_MODEL = 2048
SSM_D_INNER = 2 * D_MODEL
SSM_HEAD_DIM = 64
SSM_HEADS = SSM_D_INNER // SSM_HEAD_DIM
SSM_GROUPS = 8
SSM_HEADS_PER_GROUP = SSM_HEADS // SSM_GROUPS
SSM_GROUP_WIDTH = SSM_D_INNER // SSM_GROUPS
SSM_STATE = 128
SSM_CONV = 4
SSM_CHUNK = 256
SSM_CONV_CH = SSM_D_INNER + 2 * SSM_GROUPS * SSM_STATE
ATT_HEAD_DIM = 64
ATT_HEADS = D_MODEL // (2 * ATT_HEAD_DIM)
ATT_PAIR = 2 * ATT_HEAD_DIM
ATT_QK = 2 * ATT_HEADS * ATT_HEAD_DIM
ATT_V = ATT_HEADS * 2 * ATT_HEAD_DIM
ROPE_THETA = 10000.0
FFN_HIDDEN = (((8 * D_MODEL + 2) // 3 + 255) // 256) * 256
NORM_EPS = 1e-6
SUBLN_EPS = 1e-5
LAM_INIT = 0.8 - 0.6 * math.exp(-0.3 * 0)
LOG2E = math.log2(math.e)

COL_Z = 0
COL_X = COL_Z + SSM_D_INNER
DT_ROW0 = COL_X + SSM_CONV_CH
COL_Q = DT_ROW0
COL_K = COL_Q + ATT_QK
COL_V = COL_K + ATT_QK
COL_GS = COL_V + ATT_V
COL_GA = COL_GS + D_MODEL
PROJ_COLS = COL_GA + D_MODEL
LANE = 128
SUBLANES = 8
DT_PAD = LANE

VMEM_LIMIT = 56 * 1024 * 1024


def _cparams(sem):
    return pltpu.CompilerParams(dimension_semantics=sem, vmem_limit_bytes=VMEM_LIMIT)


def _silu(u):
    h = 0.5 * u
    return h * jnp.tanh(h) + h


INPROJ_TM = 1024
INPROJ_TN = 1024
NORM_ROWS = 256


def _rms_rows(x, w, eps):
    ms = jnp.mean(x * x, axis=-1, keepdims=True)
    return x * lax.rsqrt(ms + eps) * w


def _norm_tile(x_ref, nw_ref, h_ref):
    def body(i, carry):
        rows = pl.ds(pl.multiple_of(i * NORM_ROWS, NORM_ROWS), NORM_ROWS)
        h_ref[rows, :] = _rms_rows(x_ref[rows, :], nw_ref[...], NORM_EPS).astype(BF16)
        return carry
    lax.fori_loop(0, INPROJ_TM // NORM_ROWS, body, 0)


def _project_rows(h_ref, wt_ref):
    wt = wt_ref[...].astype(BF16)
    return lax.dot_general(h_ref[...], wt, (((1,), (1,)), ((), ())), preferred_element_type=F32)


def _inproj_kernel(x_ref, nw_ref, wt_ref, wdt_ref, out_ref, dt_ref, dtg_ref, h_ref):
    @pl.when(pl.program_id(1) == 0)
    def _():
        _norm_tile(x_ref, nw_ref, h_ref)
        dt = _project_rows(h_ref, wdt_ref)
        dt_ref[...] = dt
        for g in range(SSM_GROUPS):
            dtg_ref[g] = dt[:, g * SSM_HEADS_PER_GROUP:(g + 1) * SSM_HEADS_PER_GROUP]

    out_ref[...] = _project_rows(h_ref, wt_ref).astype(out_ref.dtype)


def _inproj(x2d, norm_w, w_t):
    m = x2d.shape[0]
    tiles_before_dt = DT_ROW0 // INPROJ_TN
    assert DT_ROW0 % INPROJ_TN == 0 and SSM_HEADS % SUBLANES == 0

    def w_rows(i, j):
        row0 = j * INPROJ_TN + jnp.where(j >= tiles_before_dt, SSM_HEADS, 0)
        return pl.multiple_of(row0, SUBLANES), 0

    return pl.pallas_call(
        _inproj_kernel,
        out_shape=(jax.ShapeDtypeStruct((m, PROJ_COLS), BF16), jax.ShapeDtypeStruct((m, DT_PAD), F32),
                   jax.ShapeDtypeStruct((SSM_GROUPS, m, SSM_HEADS_PER_GROUP), F32)),
        grid=(m // INPROJ_TM, PROJ_COLS // INPROJ_TN),
        in_specs=[
            pl.BlockSpec((INPROJ_TM, D_MODEL), lambda i, j: (i, 0)),
            pl.BlockSpec((1, D_MODEL), lambda i, j: (0, 0)),
            pl.BlockSpec((pl.Element(INPROJ_TN), pl.Element(D_MODEL)), w_rows),
            pl.BlockSpec((pl.Element(DT_PAD), pl.Element(D_MODEL)), lambda i, j: (DT_ROW0, 0)),
        ],
        out_specs=(
            pl.BlockSpec((INPROJ_TM, INPROJ_TN), lambda i, j: (i, j)),
            pl.BlockSpec((INPROJ_TM, DT_PAD), lambda i, j: (i, 0)),
            pl.BlockSpec((SSM_GROUPS, INPROJ_TM, SSM_HEADS_PER_GROUP), lambda i, j: (0, i, 0)),
        ),
        scratch_shapes=[pltpu.VMEM((INPROJ_TM, D_MODEL), BF16)],
        compiler_params=_cparams(("parallel", "arbitrary")),
        name="inproj",
    )(x2d, norm_w, w_t, w_t)


L = SSM_CHUNK
TAIL = 16
TAPS_BACK = SSM_CONV - 1
CONV_TW = 2048
CONV_SLAB = 512
CONV_SUB = 4


def _xconv_kernel(cur_ref, prev_ref, w_ref, b_ref, shift_ref, tshift_ref, o_ref):
    has_prev = pl.program_id(1) > 0
    for sub, s0 in [(sub, s0) for sub in range(CONV_SUB) for s0 in range(0, CONV_TW, CONV_SLAB)]:
        cols = slice(s0, s0 + CONV_SLAB)
        r0 = sub * L
        cur_bf = cur_ref[r0:r0 + L, cols]
        if sub == 0:
            prev_bf = jnp.where(has_prev, prev_ref[:, cols], jnp.zeros((TAIL, CONV_SLAB), BF16))
        else:
            prev_bf = cur_ref[r0 - TAIL:r0, cols]
        cur = cur_bf.astype(F32)
        shifted = jnp.dot(shift_ref[...], cur_bf, preferred_element_type=F32)
        carried = jnp.dot(tshift_ref[...], prev_bf, preferred_element_type=F32)
        w = 0.5 * w_ref[:, cols]
        h = 0.5 * b_ref[:, cols] + w[TAPS_BACK:SSM_CONV, :] * cur
        top = jnp.zeros((TAIL, CONV_SLAB), F32)
        for k in range(TAPS_BACK):
            h = h + w[k:k + 1, :] * shifted[k * L:(k + 1) * L, :]
            top = top + w[k:k + 1, :] * carried[k * TAIL:(k + 1) * TAIL, :]
        h = jnp.concatenate([h[0:TAIL, :] + top, h[TAIL:L, :]], axis=0)
        o_ref[r0:r0 + L, cols] = (h * jnp.tanh(h) + h).astype(o_ref.dtype)


def _xconv(proj3, conv_w, conv_b):
    b, s, _ = proj3.shape
    rows = CONV_SUB * L
    xblk = COL_X // CONV_TW
    tail_per_block = rows // TAIL
    shift = jnp.concatenate([jnp.eye(L, L, k=-(TAPS_BACK - k), dtype=BF16) for k in range(TAPS_BACK)], axis=0)
    tshift = jnp.concatenate(
        [jnp.eye(TAIL, TAIL, k=TAIL - (TAPS_BACK - k), dtype=BF16) for k in range(TAPS_BACK)], axis=0)
    return pl.pallas_call(
        _xconv_kernel,
        out_shape=jax.ShapeDtypeStruct((b, s, SSM_CONV_CH), BF16),
        grid=(b, s // rows, SSM_CONV_CH // CONV_TW),
        in_specs=[
            pl.BlockSpec((None, rows, CONV_TW), lambda bi, c, j: (bi, c, xblk + j)),
            pl.BlockSpec((None, TAIL, CONV_TW),
                         lambda bi, c, j: (bi, jnp.maximum(c * tail_per_block - 1, 0), xblk + j)),
            pl.BlockSpec((SSM_CONV, CONV_TW), lambda bi, c, j: (0, j)),
            pl.BlockSpec((1, CONV_TW), lambda bi, c, j: (0, j)),
            pl.BlockSpec((TAPS_BACK * L, L), lambda bi, c, j: (0, 0)),
            pl.BlockSpec((TAPS_BACK * TAIL, TAIL), lambda bi, c, j: (0, 0)),
        ],
        out_specs=pl.BlockSpec((None, rows, CONV_TW), lambda bi, c, j: (bi, c, j)),
        compiler_params=_cparams(("parallel", "parallel", "parallel")),
        name="xconv",
    )(proj3, proj3, conv_w, conv_b, shift, tshift)


HALF = L // 2
HPG = SSM_HEADS_PER_GROUP
GW = SSM_GROUP_WIDTH
PAIRS = HPG // 2
SSD_SUB = 2


def _split3(x):
    hi = x.astype(BF16)
    rest = x - hi.astype(F32)
    mid = rest.astype(BF16)
    lo = (rest - mid.astype(F32)).astype(BF16)
    return hi, mid, lo


def _ssd_kernel(x_ref, b_ref, c_ref, z_ref, dtc_ref, dtr_ref,
                biasc_ref, alogc_ref, biasr_ref, alogr_ref, dskip_ref, nw_ref, tri_ref, trit_ref,
                y_ref, state_ref, xde_ref, ycat_ref):
    @pl.when(pl.program_id(2) == 0)
    def _():
        state_ref[...] = jnp.zeros_like(state_ref)

    for sub in range(SSD_SUB):
        _ssd_chunk(slice(sub * L, (sub + 1) * L), x_ref, b_ref, c_ref, z_ref, dtc_ref, dtr_ref,
                   biasc_ref, alogc_ref, biasr_ref, alogr_ref, dskip_ref, nw_ref, tri_ref, trit_ref,
                   y_ref, state_ref, xde_ref.at[sub], ycat_ref.at[sub])


def _ssd_chunk(rows, x_ref, b_ref, c_ref, z_ref, dtc_ref, dtr_ref,
               biasc_ref, alogc_ref, biasr_ref, alogr_ref, dskip_ref, nw_ref, tri_ref, trit_ref,
               y_ref, state_ref, xde_ref, ycat_ref):
    bm_bf = b_ref[rows, :]
    cm_bf = c_ref[rows, :]
    bt_bf = bm_bf.astype(F32).T.astype(BF16)

    dtc = jax.nn.softplus(dtc_ref[rows, :] + biasc_ref[...])
    dtr = jax.nn.softplus(dtr_ref[:, rows] + biasr_ref[...])
    dac = dtc * (-LOG2E * jnp.exp(alogc_ref[...]))
    dar = dtr * (-LOG2E * jnp.exp(alogr_ref[...]))
    acs_c = sum(jnp.dot(tri_ref[...], part, preferred_element_type=F32) for part in _split3(dac))
    acs_r = sum(jnp.dot(part, trit_ref[...], preferred_element_type=F32) for part in _split3(dar))
    a_last = acs_c[L - 1:L, :]
    arow = acs_r - jnp.log2(dtr)
    g_end = jnp.log2(dtc) - acs_c + a_last
    dec_chunk = jnp.exp2(a_last)

    cb = lax.dot_general(cm_bf, bm_bf, (((1,), (1,)), ((), ())), preferred_element_type=F32)
    y_off = jnp.dot(cm_bf, state_ref[...].astype(BF16), preferred_element_type=F32)
    cb00 = cb[0:HALF, 0:HALF]
    cb10 = cb[HALF:L, 0:HALF]
    cb11 = cb[HALF:L, HALF:L]
    causal = (lax.broadcasted_iota(jnp.int32, (HALF, HALF), 0)
              >= lax.broadcasted_iota(jnp.int32, (HALF, HALF), 1))
    lane = lax.broadcasted_iota(jnp.int32, (1, LANE), 1)
    first = lane < SSM_HEAD_DIM

    for p in range(PAIRS):
        lanes = slice(p * LANE, (p + 1) * LANE)
        xs_bf = x_ref[rows, lanes]
        xs_p = xs_bf.astype(F32)
        a_cols = [jnp.broadcast_to(acs_c[:, r:r + 1], (L, LANE)) for r in (2 * p, 2 * p + 1)]
        a_pair = jnp.where(first, a_cols[0], a_cols[1])
        g_pair = jnp.where(first, g_end[:, 2 * p:2 * p + 1], g_end[:, 2 * p + 1:2 * p + 2])
        xde_ref[:, lanes] = (xs_p * jnp.exp2(g_pair)).astype(BF16)
        y_heads = []
        for h in range(2):
            a_col = a_cols[h]
            a_row = arow[2 * p + h:2 * p + h + 1, :]
            d00 = jnp.exp2(jnp.where(causal, a_col[0:HALF, :] - a_row[:, 0:HALF], -jnp.inf))
            d10 = jnp.exp2(a_col[HALF:L, :] - a_row[:, 0:HALF])
            d11 = jnp.exp2(jnp.where(causal, a_col[HALF:L, :] - a_row[:, HALF:L], -jnp.inf))
            l_top = (cb00 * d00).astype(BF16)
            l_bot = jnp.concatenate([cb10 * d10, cb11 * d11], axis=1).astype(BF16)
            y_heads.append(jnp.concatenate(
                [jnp.dot(l_top, xs_bf[0:HALF, :], preferred_element_type=F32),
                 jnp.dot(l_bot, xs_bf, preferred_element_type=F32)], axis=0))
        y_diag = jnp.where(first, y_heads[0], y_heads[1])
        ycat_ref[:, lanes] = y_diag + y_off[:, lanes] * jnp.exp2(a_pair) + dskip_ref[:, lanes] * xs_p

    upd = jnp.dot(bt_bf, xde_ref[...], preferred_element_type=F32)
    for p in range(PAIRS):
        lanes = slice(p * LANE, (p + 1) * LANE)
        dec_p = jnp.where(first, dec_chunk[:, 2 * p:2 * p + 1], dec_chunk[:, 2 * p + 1:2 * p + 2])
        state_ref[:, lanes] = state_ref[:, lanes] * dec_p + upd[:, lanes]

    y = ycat_ref[...] * _silu(z_ref[rows, :].astype(F32))
    y = y * lax.rsqrt(jnp.mean(y * y, axis=-1, keepdims=True) + SUBLN_EPS)
    y_ref[rows, :] = (y * nw_ref[...]).astype(y_ref.dtype)


def _ssd(xact, proj3, dt_col, dt_row, bias_c, alog_c, bias_r, alog_r, dskip_row, norm_w):
    b, s, _ = proj3.shape
    nc = s // L
    bblk = SSM_D_INNER // SSM_STATE
    cblk = bblk + SSM_GROUPS
    tri = jnp.tril(jnp.ones((L, L), BF16))
    const = lambda shape: pl.BlockSpec(shape, lambda bi, g, c: (0, 0))
    rows = SSD_SUB * L
    in_specs = [
        pl.BlockSpec((None, rows, GW), lambda bi, g, c: (bi, c, g)),
        pl.BlockSpec((None, rows, SSM_STATE), lambda bi, g, c: (bi, c, bblk + g)),
        pl.BlockSpec((None, rows, SSM_STATE), lambda bi, g, c: (bi, c, cblk + g)),
        pl.BlockSpec((None, rows, GW), lambda bi, g, c: (bi, c, g)),
        pl.BlockSpec((None, None, rows, HPG), lambda bi, g, c: (g, bi, c, 0)),
        pl.BlockSpec((None, None, HPG, rows), lambda bi, g, c: (bi, g, 0, c)),
        pl.BlockSpec((None, 1, HPG), lambda bi, g, c: (g, 0, 0)),
        pl.BlockSpec((None, 1, HPG), lambda bi, g, c: (g, 0, 0)),
        pl.BlockSpec((None, HPG, 1), lambda bi, g, c: (g, 0, 0)),
        pl.BlockSpec((None, HPG, 1), lambda bi, g, c: (g, 0, 0)),
        pl.BlockSpec((1, GW), lambda bi, g, c: (0, g)),
        pl.BlockSpec((1, GW), lambda bi, g, c: (0, g)),
        const((L, L)),
        const((L, L)),
    ]
    return pl.pallas_call(
        _ssd_kernel,
        out_shape=jax.ShapeDtypeStruct((b, s, SSM_D_INNER), BF16),
        grid=(b, SSM_GROUPS, nc // SSD_SUB),
        in_specs=in_specs,
        out_specs=pl.BlockSpec((None, rows, GW), lambda bi, g, c: (bi, c, g)),
        scratch_shapes=[
            pltpu.VMEM((SSM_STATE, GW), F32),
            pltpu.VMEM((SSD_SUB, L, GW), BF16),
            pltpu.VMEM((SSD_SUB, L, GW), F32),
        ],
        compiler_params=_cparams(("parallel", "parallel", "arbitrary")),
        name="ssd",
    )(xact, xact, xact, proj3, dt_col, dt_row, bias_c, alog_c, bias_r, alog_r, dskip_row, norm_w, tri, tri.T)


ATT_TQ = 512
ATT_TK = 512
ROPE_ROWS = 1024
ATT_HEADS_PER_PAIR = 2
ATT_HQ = ATT_TQ // 2


def _norm_rope(x, w, cos, sin_signed, group_ones):
    ss = jnp.dot((x * x).astype(BF16), group_ones, preferred_element_type=F32)
    xn = x * lax.rsqrt(ss * (1.0 / ATT_HEAD_DIM) + NORM_EPS) * w
    lane = lax.broadcasted_iota(jnp.int32, (1, ATT_PAIR), 1)
    low_half = (lane % ATT_HEAD_DIM) < (ATT_HEAD_DIM // 2)
    half = ATT_HEAD_DIM // 2
    rot = jnp.where(low_half, pltpu.roll(xn, ATT_PAIR - half, 1), pltpu.roll(xn, half, 1))
    return xn * cos + rot * sin_signed


def _fold_lanes(x, op):
    out = x[:, 0:LANE]
    for t in range(1, x.shape[1] // LANE):
        out = op(out, x[:, t * LANE:(t + 1) * LANE])
    return out


def _nt_dot(a, b):
    return lax.dot_general(a, b, (((1,), (1,)), ((), ())), preferred_element_type=F32)


def _attn_kernel(q_ref, k_ref, v_ref, cos_ref, sin_ref, qw_ref, kw_ref,
                  lq1_ref, lk1_ref, lq2_ref, lk2_ref, sw_ref, o_ref,
                  krot_ref, qrot_ref, s_ref, sd_ref, stat_ref, acc_ref):
    seq = k_ref.shape[0]
    ri = lax.broadcasted_iota(jnp.int32, (ATT_PAIR, ATT_PAIR), 0) // ATT_HEAD_DIM
    ci = lax.broadcasted_iota(jnp.int32, (ATT_PAIR, ATT_PAIR), 1) // ATT_HEAD_DIM
    group_ones = (ri == ci).astype(BF16)
    lane = lax.broadcasted_iota(jnp.int32, (1, ATT_PAIR), 1)

    def prep(i, carry):
        rows = pl.ds(pl.multiple_of(i * ROPE_ROWS, ROPE_ROWS), ROPE_ROWS)
        cos = cos_ref[rows, :]
        sin = sin_ref[rows, :]
        kr = _norm_rope(k_ref[rows, :].astype(F32), kw_ref[...], cos, sin, group_ones)
        krot_ref[rows, :] = kr.astype(BF16)
        qr = _norm_rope(q_ref[rows, :].astype(F32), qw_ref[...], cos, sin, group_ones)
        qr = qr * (LOG2E * ATT_HEAD_DIM ** -0.5)
        qrot_ref[0, rows, :] = jnp.where(lane < ATT_HEAD_DIM, qr, 0.0).astype(BF16)
        qrot_ref[1, rows, :] = jnp.where(lane >= ATT_HEAD_DIM, qr, 0.0).astype(BF16)
        return carry
    lax.fori_loop(0, seq // ROPE_ROWS, prep, 0, unroll=True)

    lam =(jnp.exp(jnp.sum(lq1_ref[...] * lk1_ref[...], axis=-1, keepdims=True))
           - jnp.exp(jnp.sum(lq2_ref[...] * lk2_ref[...], axis=-1, keepdims=True)) + LAM_INIT)
    tri_top = (lax.broadcasted_iota(jnp.int32, (ATT_HQ, ATT_HQ), 0)
               >= lax.broadcasted_iota(jnp.int32, (ATT_HQ, ATT_HQ), 1))
    tri_bot = (lax.broadcasted_iota(jnp.int32, (ATT_HQ, ATT_TK), 0) + ATT_HQ
               >= lax.broadcasted_iota(jnp.int32, (ATT_HQ, ATT_TK), 1))
    top = slice(0, ATT_HQ)
    bot = slice(ATT_HQ, ATT_TQ)

    for qi in range(seq // ATT_TQ):
        q0 = qi * ATT_TQ
        kd = krot_ref[q0:q0 + ATT_TQ, :]
        for h in range(ATT_HEADS_PER_PAIR):
            qh = qrot_ref[h, q0:q0 + ATT_TQ, :]
            s_top = jnp.where(tri_top, _nt_dot(qh[top, :], kd[top, :]), -jnp.inf)
            s_bot = jnp.where(tri_bot, _nt_dot(qh[bot, :], kd), -jnp.inf)
            sd_ref[h, top, 0:ATT_HQ] = s_top
            sd_ref[h, bot, :] = s_bot
            stat_ref[h, top, :] = _fold_lanes(s_top, jnp.maximum)
            stat_ref[h, bot, :] = _fold_lanes(s_bot, jnp.maximum)

        def pass1(j, carry):
            kj = krot_ref[pl.ds(pl.multiple_of(j * ATT_TK, ATT_TK), ATT_TK), :]
            for h in range(ATT_HEADS_PER_PAIR):
                s = _nt_dot(qrot_ref[h, q0:q0 + ATT_TQ, :], kj)
                s_ref[h, j] = s
                stat_ref[h] = jnp.maximum(stat_ref[h], _fold_lanes(s, jnp.maximum))
            return carry
        if qi > 0:
            lax.fori_loop(0, qi, pass1, 0, unroll=True)

        for h in range(ATT_HEADS_PER_PAIR):
            row_max = jnp.max(stat_ref[h], axis=-1, keepdims=True)
            stat_ref[ATT_HEADS_PER_PAIR + h] = jnp.broadcast_to(row_max, (ATT_TQ, LANE))

        vd = v_ref[q0:q0 + ATT_TQ, :]
        for h in range(ATT_HEADS_PER_PAIR):
            m = stat_ref[ATT_HEADS_PER_PAIR + h]
            p_top = jnp.concatenate(
                [jnp.exp2(sd_ref[h, top, t * LANE:(t + 1) * LANE] - m[top, :]) for t in range(ATT_HQ // LANE)],
                axis=1)
            p_bot = jnp.concatenate(
                [jnp.exp2(sd_ref[h, bot, t * LANE:(t + 1) * LANE] - m[bot, :]) for t in range(ATT_TK // LANE)],
                axis=1)
            stat_ref[h, top, :] = _fold_lanes(p_top, jnp.add)
            stat_ref[h, bot, :] = _fold_lanes(p_bot, jnp.add)
            acc_ref[h, top, :] = jnp.dot(p_top.astype(BF16), vd[top, :], preferred_element_type=F32)
            acc_ref[h, bot, :] = jnp.dot(p_bot.astype(BF16), vd, preferred_element_type=F32)

        def pass2(j, carry):
            vj = v_ref[pl.ds(pl.multiple_of(j * ATT_TK, ATT_TK), ATT_TK), :]
            for h in range(ATT_HEADS_PER_PAIR):
                m = stat_ref[ATT_HEADS_PER_PAIR + h]
                s = s_ref[h, j]
                p = jnp.concatenate(
                    [jnp.exp2(s[:, t * LANE:(t + 1) * LANE] - m) for t in range(ATT_TK // LANE)], axis=1)
                stat_ref[h] = stat_ref[h] + _fold_lanes(p, jnp.add)
                acc_ref[h] = acc_ref[h] + jnp.dot(p.astype(BF16), vj, preferred_element_type=F32)
            return carry
        if qi > 0:
            lax.fori_loop(0, qi, pass2, 0, unroll=True)

        l1 = jnp.sum(stat_ref[0], axis=-1, keepdims=True)
        l2 = jnp.sum(stat_ref[1], axis=-1, keepdims=True)
        o = acc_ref[0] / l1 - lam * (acc_ref[1] / l2)
        o = o * lax.rsqrt(jnp.mean(o * o, axis=-1, keepdims=True) + SUBLN_EPS) * sw_ref[...]
        o_ref[q0:q0 + ATT_TQ, :] = (o * (1.0 - LAM_INIT)).astype(o_ref.dtype)


def _attn(proj3, cos_t, sin_t, qw, kw, lq1, lk1, lq2, lk2, sw):
    b, s, _ = proj3.shape
    qblk = COL_Q // ATT_PAIR
    kblk = COL_K // ATT_PAIR
    vblk = COL_V // ATT_PAIR
    small = lambda width: pl.BlockSpec((1, width), lambda bi, h: (0, 0))
    return pl.pallas_call(
        _attn_kernel,
        out_shape=jax.ShapeDtypeStruct((b, s, ATT_V), BF16),
        grid=(b, ATT_HEADS),
        in_specs=[
            pl.BlockSpec((None, s, ATT_PAIR), lambda bi, h: (bi, 0, qblk + h)),
            pl.BlockSpec((None, s, ATT_PAIR), lambda bi, h: (bi, 0, kblk + h)),
            pl.BlockSpec((None, s, ATT_PAIR), lambda bi, h: (bi, 0, vblk + h)),
            pl.BlockSpec((s, ATT_PAIR), lambda bi, h: (0, 0)),
            pl.BlockSpec((s, ATT_PAIR), lambda bi, h: (0, 0)),
            small(ATT_PAIR), small(ATT_PAIR),
            small(ATT_HEAD_DIM), small(ATT_HEAD_DIM), small(ATT_HEAD_DIM), small(ATT_HEAD_DIM),
            small(ATT_PAIR),
        ],
        out_specs=pl.BlockSpec((None, s, ATT_PAIR), lambda bi, h: (bi, 0, h)),
        scratch_shapes=[
            pltpu.VMEM((s, ATT_PAIR), BF16),
            pltpu.VMEM((ATT_HEADS_PER_PAIR, s, ATT_PAIR), BF16),
            pltpu.VMEM((ATT_HEADS_PER_PAIR, s // ATT_TK - 1, ATT_TQ, ATT_TK), F32),
            pltpu.VMEM((ATT_HEADS_PER_PAIR, ATT_TQ, ATT_TK), F32),
            pltpu.VMEM((2 * ATT_HEADS_PER_PAIR, ATT_TQ, LANE), F32),
            pltpu.VMEM((ATT_HEADS_PER_PAIR, ATT_TQ, ATT_PAIR), F32),
        ],
        compiler_params=_cparams(("parallel", "parallel")),
        name="attn",
    )(proj3, proj3, proj3, cos_t, sin_t, qw, kw, lq1, lk1, lq2, lk2, sw)


MERGE_TM = 512
MERGE_TN = 512


def _merge_kernel(ys_ref, ws_ref, ya_ref, wa_ref, gs_ref, ga_ref, o_ref, wsbf_ref, wabf_ref):
    @pl.when(pl.program_id(1) == 0)
    def _():
        wsbf_ref[...] = ws_ref[...].astype(BF16)
        wabf_ref[...] = wa_ref[...].astype(BF16)

    bs = jnp.dot(ys_ref[...], wsbf_ref[...], preferred_element_type=F32)
    ba = jnp.dot(ya_ref[...], wabf_ref[...], preferred_element_type=F32)
    gs = jax.nn.sigmoid(gs_ref[...].astype(F32))
    ga = jax.nn.sigmoid(ga_ref[...].astype(F32))
    o_ref[...] = (gs * bs + ga * ba).astype(o_ref.dtype)


def _merge(y_ssm, w_ssm, y_att, w_att, proj):
    m = y_ssm.shape[0]
    gsblk = COL_GS // MERGE_TN
    gablk = COL_GA // MERGE_TN
    return pl.pallas_call(
        _merge_kernel,
        out_shape=jax.ShapeDtypeStruct((m, D_MODEL), BF16),
        grid=(D_MODEL // MERGE_TN, m // MERGE_TM),
        in_specs=[
            pl.BlockSpec((MERGE_TM, SSM_D_INNER), lambda j, i: (i, 0)),
            pl.BlockSpec((SSM_D_INNER, MERGE_TN), lambda j, i: (0, j)),
            pl.BlockSpec((MERGE_TM, ATT_V), lambda j, i: (i, 0)),
            pl.BlockSpec((ATT_V, MERGE_TN), lambda j, i: (0, j)),
            pl.BlockSpec((MERGE_TM, MERGE_TN), lambda j, i: (i, gsblk + j)),
            pl.BlockSpec((MERGE_TM, MERGE_TN), lambda j, i: (i, gablk + j)),
        ],
        out_specs=pl.BlockSpec((MERGE_TM, MERGE_TN), lambda j, i: (i, j)),
        scratch_shapes=[
            pltpu.VMEM((SSM_D_INNER, MERGE_TN), BF16),
            pltpu.VMEM((ATT_V, MERGE_TN), BF16),
        ],
        compiler_params=_cparams(("parallel", "arbitrary")),
        name="merge",
    )(y_ssm, w_ssm, y_att, w_att, proj, proj)


OUTPROJ_TM = 512


def _outproj_kernel(x_ref, m_ref, w_ref, nw_ref, x1_ref, h2_ref, wbf_ref):
    @pl.when(pl.program_id(0) == 0)
    def _():
        wbf_ref[...] = w_ref[...].astype(BF16)

    x1 = x_ref[...] + jnp.dot(m_ref[...], wbf_ref[...], preferred_element_type=F32)
    x1_ref[...] = x1
    h2_ref[...] = _rms_rows(x1, nw_ref[...], NORM_EPS).astype(h2_ref.dtype)


def _outproj(x2d, merged, w_out, norm_w):
    m = x2d.shape[0]
    return pl.pallas_call(
        _outproj_kernel,
        out_shape=(jax.ShapeDtypeStruct((m, D_MODEL), F32), jax.ShapeDtypeStruct((m, D_MODEL), BF16)),
        grid=(m // OUTPROJ_TM,),
        in_specs=[
            pl.BlockSpec((OUTPROJ_TM, D_MODEL), lambda i: (i, 0)),
            pl.BlockSpec((OUTPROJ_TM, D_MODEL), lambda i: (i, 0)),
            pl.BlockSpec((D_MODEL, D_MODEL), lambda i: (0, 0), pipeline_mode=pl.Buffered(1)),
            pl.BlockSpec((1, D_MODEL), lambda i: (0, 0)),
        ],
        out_specs=(
            pl.BlockSpec((OUTPROJ_TM, D_MODEL), lambda i: (i, 0)),
            pl.BlockSpec((OUTPROJ_TM, D_MODEL), lambda i: (i, 0)),
        ),
        scratch_shapes=[pltpu.VMEM((D_MODEL, D_MODEL), BF16)],
        compiler_params=_cparams(("arbitrary",)),
        name="outproj",
    )(x2d, merged, w_out, norm_w)


FFN_TM = 1024
FFN_TF = 512
DOWN_TM = 512
DOWN_TN = 512


def _ffn_up_kernel(h_ref, wg_ref, wu_ref, a_ref):
    h = h_ref[...]
    g = jnp.dot(h, wg_ref[...].astype(BF16), preferred_element_type=F32)
    u = jnp.dot(h, wu_ref[...].astype(BF16), preferred_element_type=F32)
    a_ref[...] = (g * jax.nn.sigmoid(g) * u).astype(a_ref.dtype)


def _ffn_up(h2, w_gate, w_up):
    m = h2.shape[0]
    return pl.pallas_call(
        _ffn_up_kernel,
        out_shape=jax.ShapeDtypeStruct((m, FFN_HIDDEN), BF16),
        grid=(m // FFN_TM, FFN_HIDDEN // FFN_TF),
        in_specs=[
            pl.BlockSpec((FFN_TM, D_MODEL), lambda i, j: (i, 0)),
            pl.BlockSpec((D_MODEL, FFN_TF), lambda i, j: (0, j)),
            pl.BlockSpec((D_MODEL, FFN_TF), lambda i, j: (0, j)),
        ],
        out_specs=pl.BlockSpec((FFN_TM, FFN_TF), lambda i, j: (i, j)),
        compiler_params=_cparams(("parallel", "parallel")),
        name="ffn_up",
    )(h2, w_gate, w_up)


def _ffn_down_kernel(a_ref, w_ref, x1_ref, o_ref, wbf_ref):
    @pl.when(pl.program_id(1) == 0)
    def _():
        wbf_ref[...] = w_ref[...].astype(BF16)

    o_ref[...] = x1_ref[...] + jnp.dot(a_ref[...], wbf_ref[...], preferred_element_type=F32)


def _ffn_down(a, w_down, x1):
    m = a.shape[0]
    return pl.pallas_call(
        _ffn_down_kernel,
        out_shape=jax.ShapeDtypeStruct((m, D_MODEL), F32),
        grid=(D_MODEL // DOWN_TN, m // DOWN_TM),
        in_specs=[
            pl.BlockSpec((DOWN_TM, FFN_HIDDEN), lambda j, i: (i, 0)),
            pl.BlockSpec((FFN_HIDDEN, DOWN_TN), lambda j, i: (0, j)),
            pl.BlockSpec((DOWN_TM, DOWN_TN), lambda j, i: (i, j)),
        ],
        out_specs=pl.BlockSpec((DOWN_TM, DOWN_TN), lambda j, i: (i, j)),
        scratch_shapes=[pltpu.VMEM((FFN_HIDDEN, DOWN_TN), BF16)],
        compiler_params=_cparams(("parallel", "arbitrary")),
        name="ffn_down",
    )(a, w_down, x1)


def _rope_tables(seq):
    half = ATT_HEAD_DIM // 2
    inv_freq = ROPE_THETA ** (-jnp.arange(half, dtype=F32) / half)
    ang = jnp.arange(seq, dtype=jnp.int32).astype(F32)[:, None] * inv_freq[None, :]
    cos = jnp.cos(ang)
    sin = jnp.sin(ang)
    reps = ATT_PAIR // ATT_HEAD_DIM
    cos_t = jnp.tile(jnp.concatenate([cos, cos], axis=-1), (1, reps))
    sin_t = jnp.tile(jnp.concatenate([-sin, sin], axis=-1), (1, reps))
    return cos_t, sin_t


def _layer(x, mix_norm_w, w_in, conv_w, conv_b, dt_bias, a_log, d_skip, ssm_norm_w, w_ssm_proj,
           q_norm_w, k_norm_w, lambda_q1, lambda_k1, lambda_q2, lambda_k2, subln_w, w_attn_proj,
           w_out, ffn_norm_w, w_ffn_gate, w_ffn_up, w_ffn_down):
    b, s, d = x.shape
    m = b * s
    x2d = x.reshape(m, d)

    proj, dt_raw, dt_grouped = _inproj(x2d, mix_norm_w.reshape(1, d), w_in.T)
    proj3 = proj.reshape(b, s, PROJ_COLS)

    xact = _xconv(proj3, conv_w, conv_b.reshape(1, SSM_CONV_CH))
    dt_col = dt_grouped.reshape(SSM_GROUPS, b, s, HPG)
    dt_row = jnp.transpose(dt_raw[:, :SSM_HEADS].reshape(b, s, SSM_GROUPS, HPG), (0, 2, 3, 1))
    per_head_c = lambda v: v.astype(F32).reshape(SSM_GROUPS, 1, HPG)
    per_head_r = lambda v: v.astype(F32).reshape(SSM_GROUPS, HPG, 1)
    dskip_row = jnp.repeat(d_skip.astype(F32), SSM_HEAD_DIM).reshape(1, SSM_D_INNER)
    y_ssm = _ssd(xact, proj3, dt_col, dt_row,
                 per_head_c(dt_bias), per_head_c(a_log), per_head_r(dt_bias), per_head_r(a_log),
                 dskip_row, ssm_norm_w.reshape(1, SSM_D_INNER))

    cos_t, sin_t = _rope_tables(s)
    pair_w = lambda v: jnp.tile(v.astype(F32), ATT_PAIR // ATT_HEAD_DIM).reshape(1, ATT_PAIR)
    lam_w = lambda v: v.astype(F32).reshape(1, ATT_HEAD_DIM)
    y_att = _attn(proj3, cos_t, sin_t, pair_w(q_norm_w), pair_w(k_norm_w),
                  lam_w(lambda_q1), lam_w(lambda_k1), lam_w(lambda_q2), lam_w(lambda_k2),
                  subln_w.astype(F32).reshape(1, ATT_PAIR))

    merged = _merge(y_ssm.reshape(m, SSM_D_INNER), w_ssm_proj, y_att.reshape(m, ATT_V), w_attn_proj, proj)
    x1, h2 = _outproj(x2d, merged, w_out, ffn_norm_w.reshape(1, d))
    a = _ffn_up(h2, w_ffn_gate, w_ffn_up)
    out = _ffn_down(a, w_ffn_down, x1)
    return out.reshape(b, s, d)


def kernel(x, mix_norm_w, w_in, conv_w, conv_b, dt_bias, a_log, d_skip, ssm_norm_w, w_ssm_proj, q_norm_w, k_norm_w, lambda_q1, lambda_k1, lambda_q2, lambda_k2, subln_w, w_attn_proj, w_out, ffn_norm_w, w_ffn_gate, w_ffn_up, w_ffn_down):
    depth = w_in.shape[0]
    assert depth == 1, "LAM_INIT is derived for a single layer"
    layer = lambda v: v[0]
    return _layer(x, *(layer(v) for v in (
        mix_norm_w, w_in, conv_w, conv_b, dt_bias, a_log, d_skip, ssm_norm_w, w_ssm_proj, q_norm_w, k_norm_w,
        lambda_q1, lambda_k1, lambda_q2, lambda_k2, subln_w, w_attn_proj, w_out, ffn_norm_w, w_ffn_gate,
        w_ffn_up, w_ffn_down)))
```

```python
import math

import jax
import jax.numpy as jnp
from jax import lax
from jax.experimental import pallas as pl
from jax.experimental.pallas import tpu as pltpu

F32 = jnp.float32
BF16 = jnp.bfloat16

D_MODEL = 2048
SSM_D_INNER = 2 * D_MODEL
SSM_HEAD_DIM = 64
SSM_HEADS = SSM_D_INNER // SSM_HEAD_DIM
SSM_GROUPS = 8
SSM_HEADS_PER_GROUP = SSM_HEADS // SSM_GROUPS
SSM_GROUP_WIDTH = SSM_D_INNER // SSM_GROUPS
SSM_STATE = 128
SSM_CONV = 4
SSM_CHUNK = 256
SSM_CONV_CH = SSM_D_INNER + 2 * SSM_GROUPS * SSM_STATE
ATT_HEAD_DIM = 64
ATT_HEADS = D_MODEL // (2 * ATT_HEAD_DIM)
ATT_PAIR = 2 * ATT_HEAD_DIM
ATT_QK = 2 * ATT_HEADS * ATT_HEAD_DIM
ATT_V = ATT_HEADS * 2 * ATT_HEAD_DIM
ROPE_THETA = 10000.0
FFN_HIDDEN = (((8 * D_MODEL + 2) // 3 + 255) // 256) * 256
NORM_EPS = 1e-6
SUBLN_EPS = 1e-5
LAM_INIT = 0.8 - 0.6 * math.exp(-0.3 * 0)
LOG2E = math.log2(math.e)

COL_Z = 0
COL_X = COL_Z + SSM_D_INNER
DT_ROW0 = COL_X + SSM_CONV_CH
COL_Q = DT_ROW0
COL_K = COL_Q + ATT_QK
COL_V = COL_K + ATT_QK
COL_GS = COL_V + ATT_V
COL_GA = COL_GS + D_MODEL
PROJ_COLS = COL_GA + D_MODEL
LANE = 128
SUBLANES = 8
DT_PAD = LANE

VMEM_LIMIT = 56 * 1024 * 1024


def _cparams(sem):
    return pltpu.CompilerParams(dimension_semantics=sem, vmem_limit_bytes=VMEM_LIMIT)


def _silu(u):
    h = 0.5 * u
    return h * jnp.tanh(h) + h


INPROJ_TM = 1024
INPROJ_TN = 1024
NORM_ROWS = 256


def _rms_rows(x, w, eps):
    ms = jnp.mean(x * x, axis=-1, keepdims=True)
    return x * lax.rsqrt(ms + eps) * w


def _norm_tile(x_ref, nw_ref, h_ref):
    def body(i, carry):
        rows = pl.ds(pl.multiple_of(i * NORM_ROWS, NORM_ROWS), NORM_ROWS)
        h_ref[rows, :] = _rms_rows(x_ref[rows, :], nw_ref[...], NORM_EPS).astype(BF16)
        return carry
    lax.fori_loop(0, INPROJ_TM // NORM_ROWS, body, 0)


def _project_rows(h_ref, wt_ref):
    wt = wt_ref[...].astype(BF16)
    return lax.dot_general(h_ref[...], wt, (((1,), (1,)), ((), ())), preferred_element_type=F32)


def _inproj_kernel(x_ref, nw_ref, wt_ref, wdt_ref, out_ref, dt_ref, dtg_ref, h_ref):
    @pl.when(pl.program_id(1) == 0)
    def _():
        _norm_tile(x_ref, nw_ref, h_ref)
        dt = _project_rows(h_ref, wdt_ref)
        dt_ref[...] = dt
        for g in range(SSM_GROUPS):
            dtg_ref[g] = dt[:, g * SSM_HEADS_PER_GROUP:(g + 1) * SSM_HEADS_PER_GROUP]

    out_ref[...] = _project_rows(h_ref, wt_ref).astype(out_ref.dtype)


def _inproj(x2d, norm_w, w_t):
    m = x2d.shape[0]
    tiles_before_dt = DT_ROW0 // INPROJ_TN
    assert DT_ROW0 % INPROJ_TN == 0 and SSM_HEADS % SUBLANES == 0

    def w_rows(i, j):
        row0 = j * INPROJ_TN + jnp.where(j >= tiles_before_dt, SSM_HEADS, 0)
        return pl.multiple_of(row0, SUBLANES), 0

    return pl.pallas_call(
        _inproj_kernel,
        out_shape=(jax.ShapeDtypeStruct((m, PROJ_COLS), BF16), jax.ShapeDtypeStruct((m, DT_PAD), F32),
                   jax.ShapeDtypeStruct((SSM_GROUPS, m, SSM_HEADS_PER_GROUP), F32)),
        grid=(m // INPROJ_TM, PROJ_COLS // INPROJ_TN),
        in_specs=[
            pl.BlockSpec((INPROJ_TM, D_MODEL), lambda i, j: (i, 0)),
            pl.BlockSpec((1, D_MODEL), lambda i, j: (0, 0)),
            pl.BlockSpec((pl.Element(INPROJ_TN), pl.Element(D_MODEL)), w_rows),
            pl.BlockSpec((pl.Element(DT_PAD), pl.Element(D_MODEL)), lambda i, j: (DT_ROW0, 0)),
        ],
        out_specs=(
            pl.BlockSpec((INPROJ_TM, INPROJ_TN), lambda i, j: (i, j)),
            pl.BlockSpec((INPROJ_TM, DT_PAD), lambda i, j: (i, 0)),
            pl.BlockSpec((SSM_GROUPS, INPROJ_TM, SSM_HEADS_PER_GROUP), lambda i, j: (0, i, 0)),
        ),
        scratch_shapes=[pltpu.VMEM((INPROJ_TM, D_MODEL), BF16)],
        compiler_params=_cparams(("parallel", "arbitrary")),
        name="inproj",
    )(x2d, norm_w, w_t, w_t)


L = SSM_CHUNK
TAIL = 16
TAPS_BACK = SSM_CONV - 1
CONV_TW = 2048
CONV_SLAB = 512
CONV_SUB = 4


def _xconv_kernel(cur_ref, prev_ref, w_ref, b_ref, shift_ref, tshift_ref, o_ref):
    has_prev = pl.program_id(1) > 0
    for sub, s0 in [(sub, s0) for sub in range(CONV_SUB) for s0 in range(0, CONV_TW, CONV_SLAB)]:
        cols = slice(s0, s0 + CONV_SLAB)
        r0 = sub * L
        cur_bf = cur_ref[r0:r0 + L, cols]
        if sub == 0:
            prev_bf = jnp.where(has_prev, prev_ref[:, cols], jnp.zeros((TAIL, CONV_SLAB), BF16))
        else:
            prev_bf = cur_ref[r0 - TAIL:r0, cols]
        cur = cur_bf.astype(F32)
        shifted = jnp.dot(shift_ref[...], cur_bf, preferred_element_type=F32)
        carried = jnp.dot(tshift_ref[...], prev_bf, preferred_element_type=F32)
        w = 0.5 * w_ref[:, cols]
        h = 0.5 * b_ref[:, cols] + w[TAPS_BACK:SSM_CONV, :] * cur
        top = jnp.zeros((TAIL, CONV_SLAB), F32)
        for k in range(TAPS_BACK):
            h = h + w[k:k + 1, :] * shifted[k * L:(k + 1) * L, :]
            top = top + w[k:k + 1, :] * carried[k * TAIL:(k + 1) * TAIL, :]
        h = jnp.concatenate([h[0:TAIL, :] + top, h[TAIL:L, :]], axis=0)
        o_ref[r0:r0 + L, cols] = (h * jnp.tanh(h) + h).astype(o_ref.dtype)


def _xconv(proj3, conv_w, conv_b):
    b, s, _ = proj3.shape
    rows = CONV_SUB * L
    xblk = COL_X // CONV_TW
    tail_per_block = rows // TAIL
    shift = jnp.concatenate([jnp.eye(L, L, k=-(TAPS_BACK - k), dtype=BF16) for k in range(TAPS_BACK)], axis=0)
    tshift = jnp.concatenate(
        [jnp.eye(TAIL, TAIL, k=TAIL - (TAPS_BACK - k), dtype=BF16) for k in range(TAPS_BACK)], axis=0)
    return pl.pallas_call(
        _xconv_kernel,
        out_shape=jax.ShapeDtypeStruct((b, s, SSM_CONV_CH), BF16),
        grid=(b, s // rows, SSM_CONV_CH // CONV_TW),
        in_specs=[
            pl.BlockSpec((None, rows, CONV_TW), lambda bi, c, j: (bi, c, xblk + j)),
            pl.BlockSpec((None, TAIL, CONV_TW),
                         lambda bi, c, j: (bi, jnp.maximum(c * tail_per_block - 1, 0), xblk + j)),
            pl.BlockSpec((SSM_CONV, CONV_TW), lambda bi, c, j: (0, j)),
            pl.BlockSpec((1, CONV_TW), lambda bi, c, j: (0, j)),
            pl.BlockSpec((TAPS_BACK * L, L), lambda bi, c, j: (0, 0)),
            pl.BlockSpec((TAPS_BACK * TAIL, TAIL), lambda bi, c, j: (0, 0)),
        ],
        out_specs=pl.BlockSpec((None, rows, CONV_TW), lambda bi, c, j: (bi, c, j)),
        compiler_params=_cparams(("parallel", "parallel", "parallel")),
        name="xconv",
    )(proj3, proj3, conv_w, conv_b, shift, tshift)


HALF = L // 2
HPG = SSM_HEADS_PER_GROUP
GW = SSM_GROUP_WIDTH
PAIRS = HPG // 2
SSD_SUB = 2


def _split3(x):
    hi = x.astype(BF16)
    rest = x - hi.astype(F32)
    mid = rest.astype(BF16)
    lo = (rest - mid.astype(F32)).astype(BF16)
    return hi, mid, lo


def _ssd_kernel(x_ref, b_ref, c_ref, z_ref, dtc_ref, dtr_ref,
                biasc_ref, alogc_ref, biasr_ref, alogr_ref, dskip_ref, nw_ref, tri_ref, trit_ref,
                y_ref, state_ref, xde_ref, ycat_ref):
    @pl.when(pl.program_id(2) == 0)
    def _():
        state_ref[...] = jnp.zeros_like(state_ref)

    for sub in range(SSD_SUB):
        _ssd_chunk(slice(sub * L, (sub + 1) * L), x_ref, b_ref, c_ref, z_ref, dtc_ref, dtr_ref,
                   biasc_ref, alogc_ref, biasr_ref, alogr_ref, dskip_ref, nw_ref, tri_ref, trit_ref,
                   y_ref, state_ref, xde_ref.at[sub], ycat_ref.at[sub])


def _ssd_chunk(rows, x_ref, b_ref, c_ref, z_ref, dtc_ref, dtr_ref,
               biasc_ref, alogc_ref, biasr_ref, alogr_ref, dskip_ref, nw_ref, tri_ref, trit_ref,
               y_ref, state_ref, xde_ref, ycat_ref):
    bm_bf = b_ref[rows, :]
    cm_bf = c_ref[rows, :]
    bt_bf = bm_bf.astype(F32).T.astype(BF16)

    dtc = jax.nn.softplus(dtc_ref[rows, :] + biasc_ref[...])
    dtr = jax.nn.softplus(dtr_ref[:, rows] + biasr_ref[...])
    dac = dtc * (-LOG2E * jnp.exp(alogc_ref[...]))
    dar = dtr * (-LOG2E * jnp.exp(alogr_ref[...]))
    acs_c = sum(jnp.dot(tri_ref[...], part, preferred_element_type=F32) for part in _split3(dac))
    acs_r = sum(jnp.dot(part, trit_ref[...], preferred_element_type=F32) for part in _split3(dar))
    a_last = acs_c[L - 1:L, :]
    arow = acs_r - jnp.log2(dtr)
    g_end = jnp.log2(dtc) - acs_c + a_last
    dec_chunk = jnp.exp2(a_last)

    cb = lax.dot_general(cm_bf, bm_bf, (((1,), (1,)), ((), ())), preferred_element_type=F32)
    y_off = jnp.dot(cm_bf, state_ref[...].astype(BF16), preferred_element_type=F32)
    cb00 = cb[0:HALF, 0:HALF]
    cb10 = cb[HALF:L, 0:HALF]
    cb11 = cb[HALF:L, HALF:L]
    causal = (lax.broadcasted_iota(jnp.int32, (HALF, HALF), 0)
              >= lax.broadcasted_iota(jnp.int32, (HALF, HALF), 1))
    lane = lax.broadcasted_iota(jnp.int32, (1, LANE), 1)
    first = lane < SSM_HEAD_DIM

    for p in range(PAIRS):
        lanes = slice(p * LANE, (p + 1) * LANE)
        xs_bf = x_ref[rows, lanes]
        xs_p = xs_bf.astype(F32)
        a_cols = [jnp.broadcast_to(acs_c[:, r:r + 1], (L, LANE)) for r in (2 * p, 2 * p + 1)]
        a_pair = jnp.where(first, a_cols[0], a_cols[1])
        g_pair = jnp.where(first, g_end[:, 2 * p:2 * p + 1], g_end[:, 2 * p + 1:2 * p + 2])
        xde_ref[:, lanes] = (xs_p * jnp.exp2(g_pair)).astype(BF16)
        y_heads = []
        for h in range(2):
            a_col = a_cols[h]
            a_row = arow[2 * p + h:2 * p + h + 1, :]
            d00 = jnp.exp2(jnp.where(causal, a_col[0:HALF, :] - a_row[:, 0:HALF], -jnp.inf))
            d10 = jnp.exp2(a_col[HALF:L, :] - a_row[:, 0:HALF])
            d11 = jnp.exp2(jnp.where(causal, a_col[HALF:L, :] - a_row[:, HALF:L], -jnp.inf))
            l_top = (cb00 * d00).astype(BF16)
            l_bot = jnp.concatenate([cb10 * d10, cb11 * d11], axis=1).astype(BF16)
            y_heads.append(jnp.concatenate(
                [jnp.dot(l_top, xs_bf[0:HALF, :], preferred_element_type=F32),
                 jnp.dot(l_bot, xs_bf, preferred_element_type=F32)], axis=0))
        y_diag = jnp.where(first, y_heads[0], y_heads[1])
        ycat_ref[:, lanes] = y_diag + y_off[:, lanes] * jnp.exp2(a_pair) + dskip_ref[:, lanes] * xs_p

    upd = jnp.dot(bt_bf, xde_ref[...], preferred_element_type=F32)
    for p in range(PAIRS):
        lanes = slice(p * LANE, (p + 1) * LANE)
        dec_p = jnp.where(first, dec_chunk[:, 2 * p:2 * p + 1], dec_chunk[:, 2 * p + 1:2 * p + 2])
        state_ref[:, lanes] = state_ref[:, lanes] * dec_p + upd[:, lanes]

    y = ycat_ref[...] * _silu(z_ref[rows, :].astype(F32))
    y = y * lax.rsqrt(jnp.mean(y * y, axis=-1, keepdims=True) + SUBLN_EPS)
    y_ref[rows, :] = (y * nw_ref[...]).astype(y_ref.dtype)


def _ssd(xact, proj3, dt_col, dt_row, bias_c, alog_c, bias_r, alog_r, dskip_row, norm_w):
    b, s, _ = proj3.shape
    nc = s // L
    bblk = SSM_D_INNER // SSM_STATE
    cblk = bblk + SSM_GROUPS
    tri = jnp.tril(jnp.ones((L, L), BF16))
    const = lambda shape: pl.BlockSpec(shape, lambda bi, g, c: (0, 0))
    rows = SSD_SUB * L
    in_specs = [
        pl.BlockSpec((None, rows, GW), lambda bi, g, c: (bi, c, g)),
        pl.BlockSpec((None, rows, SSM_STATE), lambda bi, g, c: (bi, c, bblk + g)),
        pl.BlockSpec((None, rows, SSM_STATE), lambda bi, g, c: (bi, c, cblk + g)),
        pl.BlockSpec((None, rows, GW), lambda bi, g, c: (bi, c, g)),
        pl.BlockSpec((None, rows, HPG), lambda bi, g, c: (g, bi * (nc // SSD_SUB) + c, 0)),
        pl.BlockSpec((None, None, HPG, rows), lambda bi, g, c: (bi, g, 0, c)),
        pl.BlockSpec((None, 1, HPG), lambda bi, g, c: (g, 0, 0)),
        pl.BlockSpec((None, 1, HPG), lambda bi, g, c: (g, 0, 0)),
        pl.BlockSpec((None, HPG, 1), lambda bi, g, c: (g, 0, 0)),
        pl.BlockSpec((None, HPG, 1), lambda bi, g, c: (g, 0, 0)),
        pl.BlockSpec((1, GW), lambda bi, g, c: (0, g)),
        pl.BlockSpec((1, GW), lambda bi, g, c: (0, g)),
        const((L, L)),
        const((L, L)),
    ]
    return pl.pallas_call(
        _ssd_kernel,
        out_shape=jax.ShapeDtypeStruct((b, s, SSM_D_INNER), BF16),
        grid=(b, SSM_GROUPS, nc // SSD_SUB),
        in_specs=in_specs,
        out_specs=pl.BlockSpec((None, rows, GW), lambda bi, g, c: (bi, c, g)),
        scratch_shapes=[
            pltpu.VMEM((SSM_STATE, GW), F32),
            pltpu.VMEM((SSD_SUB, L, GW), BF16),
            pltpu.VMEM((SSD_SUB, L, GW), F32),
        ],
        compiler_params=_cparams(("parallel", "parallel", "arbitrary")),
        name="ssd",
    )(xact, xact, xact, proj3, dt_col, dt_row, bias_c, alog_c, bias_r, alog_r, dskip_row, norm_w, tri, tri.T)


ATT_TQ = 512
ATT_TK = 512
ROPE_ROWS = 1024
ATT_HEADS_PER_PAIR = 2
ATT_HQ = ATT_TQ // 2


def _norm_rope(x, w, cos, sin_signed, group_ones):
    ss = jnp.dot((x * x).astype(BF16), group_ones, preferred_element_type=F32)
    xn = x * lax.rsqrt(ss * (1.0 / ATT_HEAD_DIM) + NORM_EPS) * w
    lane = lax.broadcasted_iota(jnp.int32, (1, ATT_PAIR), 1)
    low_half = (lane % ATT_HEAD_DIM) < (ATT_HEAD_DIM // 2)
    half = ATT_HEAD_DIM // 2
    rot = jnp.where(low_half, pltpu.roll(xn, ATT_PAIR - half, 1), pltpu.roll(xn, half, 1))
    return xn * cos + rot * sin_signed


def _fold_lanes(x, op):
    out = x[:, 0:LANE]
    for t in range(1, x.shape[1] // LANE):
        out = op(out, x[:, t * LANE:(t + 1) * LANE])
    return out


def _nt_dot(a, b):
    return lax.dot_general(a, b, (((1,), (1,)), ((), ())), preferred_element_type=F32)


def _attn_kernel(q_ref, k_ref, v_ref, cos_ref, sin_ref, qw_ref, kw_ref,
                  lq1_ref, lk1_ref, lq2_ref, lk2_ref, sw_ref, o_ref,
                  krot_ref, qrot_ref, s_ref, sd_ref, stat_ref, acc_ref):
    seq = k_ref.shape[0]
    ri = lax.broadcasted_iota(jnp.int32, (ATT_PAIR, ATT_PAIR), 0) // ATT_HEAD_DIM
    ci = lax.broadcasted_iota(jnp.int32, (ATT_PAIR, ATT_PAIR), 1) // ATT_HEAD_DIM
    group_ones = (ri == ci).astype(BF16)
    lane = lax.broadcasted_iota(jnp.int32, (1, ATT_PAIR), 1)

    def prep(i, carry):
        rows = pl.ds(pl.multiple_of(i * ROPE_ROWS, ROPE_ROWS), ROPE_ROWS)
        cos = cos_ref[rows, :]
        sin = sin_ref[rows, :]
        kr = _norm_rope(k_ref[rows, :].astype(F32), kw_ref[...], cos, sin, group_ones)
        krot_ref[rows, :] = kr.astype(BF16)
        qr = _norm_rope(q_ref[rows, :].astype(F32), qw_ref[...], cos, sin, group_ones)
        qr = qr * (LOG2E * ATT_HEAD_DIM ** -0.5)
        qrot_ref[0, rows, :] = jnp.where(lane < ATT_HEAD_DIM, qr, 0.0).astype(BF16)
        qrot_ref[1, rows, :] = jnp.where(lane >= ATT_HEAD_DIM, qr, 0.0).astype(BF16)
        return carry
    lax.fori_loop(0, seq // ROPE_ROWS, prep, 0, unroll=True)

    lam =(jnp.exp(jnp.sum(lq1_ref[...] * lk1_ref[...], axis=-1, keepdims=True))
           - jnp.exp(jnp.sum(lq2_ref[...] * lk2_ref[...], axis=-1, keepdims=True)) + LAM_INIT)
    tri_top = (lax.broadcasted_iota(jnp.int32, (ATT_HQ, ATT_HQ), 0)
               >= lax.broadcasted_iota(jnp.int32, (ATT_HQ, ATT_HQ), 1))
    tri_bot = (lax.broadcasted_iota(jnp.int32, (ATT_HQ, ATT_TK), 0) + ATT_HQ
               >= lax.broadcasted_iota(jnp.int32, (ATT_HQ, ATT_TK), 1))
    top = slice(0, ATT_HQ)
    bot = slice(ATT_HQ, ATT_TQ)

    for qi in range(seq // ATT_TQ):
        q0 = qi * ATT_TQ
        kd = krot_ref[q0:q0 + ATT_TQ, :]
        for h in range(ATT_HEADS_PER_PAIR):
            qh = qrot_ref[h, q0:q0 + ATT_TQ, :]
            s_top = jnp.where(tri_top, _nt_dot(qh[top, :], kd[top, :]), -jnp.inf)
            s_bot = jnp.where(tri_bot, _nt_dot(qh[bot, :], kd), -jnp.inf)
            sd_ref[h, top, 0:ATT_HQ] = s_top
            sd_ref[h, bot, :] = s_bot
            stat_ref[h, top, :] = _fold_lanes(s_top, jnp.maximum)
            stat_ref[h, bot, :] = _fold_lanes(s_bot, jnp.maximum)

        def pass1(j, carry):
            kj = krot_ref[pl.ds(pl.multiple_of(j * ATT_TK, ATT_TK), ATT_TK), :]
            for h in range(ATT_HEADS_PER_PAIR):
                s = _nt_dot(qrot_ref[h, q0:q0 + ATT_TQ, :], kj)
                s_ref[h, j] = s
                stat_ref[h] = jnp.maximum(stat_ref[h], _fold_lanes(s, jnp.maximum))
            return carry
        if qi > 0:
            lax.fori_loop(0, qi, pass1, 0, unroll=True)

        for h in range(ATT_HEADS_PER_PAIR):
            row_max = jnp.max(stat_ref[h], axis=-1, keepdims=True)
            stat_ref[ATT_HEADS_PER_PAIR + h] = jnp.broadcast_to(row_max, (ATT_TQ, LANE))

        vd = v_ref[q0:q0 + ATT_TQ, :]
        for h in range(ATT_HEADS_PER_PAIR):
            m = stat_ref[ATT_HEADS_PER_PAIR + h]
            p_top = jnp.concatenate(
                [jnp.exp2(sd_ref[h, top, t * LANE:(t + 1) * LANE] - m[top, :]) for t in range(ATT_HQ // LANE)],
                axis=1)
            p_bot = jnp.concatenate(
                [jnp.exp2(sd_ref[h, bot, t * LANE:(t + 1) * LANE] - m[bot, :]) for t in range(ATT_TK // LANE)],
                axis=1)
            stat_ref[h, top, :] = _fold_lanes(p_top, jnp.add)
            stat_ref[h, bot, :] = _fold_lanes(p_bot, jnp.add)
            acc_ref[h, top, :] = jnp.dot(p_top.astype(BF16), vd[top, :], preferred_element_type=F32)
            acc_ref[h, bot, :] = jnp.dot(p_bot.astype(BF16), vd, preferred_element_type=F32)

        def pass2(j, carry):
            vj = v_ref[pl.ds(pl.multiple_of(j * ATT_TK, ATT_TK), ATT_TK), :]
            for h in range(ATT_HEADS_PER_PAIR):
                m = stat_ref[ATT_HEADS_PER_PAIR + h]
                s = s_ref[h, j]
                p = jnp.concatenate(
                    [jnp.exp2(s[:, t * LANE:(t + 1) * LANE] - m) for t in range(ATT_TK // LANE)], axis=1)
                stat_ref[h] = stat_ref[h] + _fold_lanes(p, jnp.add)
                acc_ref[h] = acc_ref[h] + jnp.dot(p.astype(BF16), vj, preferred_element_type=F32)
            return carry
        if qi > 0:
            lax.fori_loop(0, qi, pass2, 0, unroll=True)

        l1 = jnp.sum(stat_ref[0], axis=-1, keepdims=True)
        l2 = jnp.sum(stat_ref[1], axis=-1, keepdims=True)
        o = acc_ref[0] / l1 - lam * (acc_ref[1] / l2)
        o = o * lax.rsqrt(jnp.mean(o * o, axis=-1, keepdims=True) + SUBLN_EPS) * sw_ref[...]
        o_ref[q0:q0 + ATT_TQ, :] = (o * (1.0 - LAM_INIT)).astype(o_ref.dtype)


def _attn(proj3, cos_t, sin_t, qw, kw, lq1, lk1, lq2, lk2, sw):
    b, s, _ = proj3.shape
    qblk = COL_Q // ATT_PAIR
    kblk = COL_K // ATT_PAIR
    vblk = COL_V // ATT_PAIR
    small = lambda width: pl.BlockSpec((1, width), lambda bi, h: (0, 0))
    return pl.pallas_call(
        _attn_kernel,
        out_shape=jax.ShapeDtypeStruct((b, s, ATT_V), BF16),
        grid=(b, ATT_HEADS),
        in_specs=[
            pl.BlockSpec((None, s, ATT_PAIR), lambda bi, h: (bi, 0, qblk + h)),
            pl.BlockSpec((None, s, ATT_PAIR), lambda bi, h: (bi, 0, kblk + h)),
            pl.BlockSpec((None, s, ATT_PAIR), lambda bi, h: (bi, 0, vblk + h)),
            pl.BlockSpec((s, ATT_PAIR), lambda bi, h: (0, 0)),
            pl.BlockSpec((s, ATT_PAIR), lambda bi, h: (0, 0)),
            small(ATT_PAIR), small(ATT_PAIR),
            small(ATT_HEAD_DIM), small(ATT_HEAD_DIM), small(ATT_HEAD_DIM), small(ATT_HEAD_DIM),
            small(ATT_PAIR),
        ],
        out_specs=pl.BlockSpec((None, s, ATT_PAIR), lambda bi, h: (bi, 0, h)),
        scratch_shapes=[
            pltpu.VMEM((s, ATT_PAIR), BF16),
            pltpu.VMEM((ATT_HEADS_PER_PAIR, s, ATT_PAIR), BF16),
            pltpu.VMEM((ATT_HEADS_PER_PAIR, s // ATT_TK - 1, ATT_TQ, ATT_TK), F32),
            pltpu.VMEM((ATT_HEADS_PER_PAIR, ATT_TQ, ATT_TK), F32),
            pltpu.VMEM((2 * ATT_HEADS_PER_PAIR, ATT_TQ, LANE), F32),
            pltpu.VMEM((ATT_HEADS_PER_PAIR, ATT_TQ, ATT_PAIR), F32),
        ],
        compiler_params=_cparams(("parallel", "parallel")),
        name="attn",
    )(proj3, proj3, proj3, cos_t, sin_t, qw, kw, lq1, lk1, lq2, lk2, sw)


MERGE_TM = 512
MERGE_TN = 512


def _merge_kernel(ys_ref, ws_ref, ya_ref, wa_ref, gs_ref, ga_ref, o_ref, wsbf_ref, wabf_ref):
    @pl.when(pl.program_id(1) == 0)
    def _():
        wsbf_ref[...] = ws_ref[...].astype(BF16)
        wabf_ref[...] = wa_ref[...].astype(BF16)

    bs = jnp.dot(ys_ref[...], wsbf_ref[...], preferred_element_type=F32)
    ba = jnp.dot(ya_ref[...], wabf_ref[...], preferred_element_type=F32)
    gs = jax.nn.sigmoid(gs_ref[...].astype(F32))
    ga = jax.nn.sigmoid(ga_ref[...].astype(F32))
    o_ref[...] = (gs * bs + ga * ba).astype(o_ref.dtype)


def _merge(y_ssm, w_ssm, y_att, w_att, proj):
    m = y_ssm.shape[0]
    gsblk = COL_GS // MERGE_TN
    gablk = COL_GA // MERGE_TN
    return pl.pallas_call(
        _merge_kernel,
        out_shape=jax.ShapeDtypeStruct((m, D_MODEL), BF16),
        grid=(D_MODEL // MERGE_TN, m // MERGE_TM),
        in_specs=[
            pl.BlockSpec((MERGE_TM, SSM_D_INNER), lambda j, i: (i, 0)),
            pl.BlockSpec((SSM_D_INNER, MERGE_TN), lambda j, i: (0, j)),
            pl.BlockSpec((MERGE_TM, ATT_V), lambda j, i: (i, 0)),
            pl.BlockSpec((ATT_V, MERGE_TN), lambda j, i: (0, j)),
            pl.BlockSpec((MERGE_TM, MERGE_TN), lambda j, i: (i, gsblk + j)),
            pl.BlockSpec((MERGE_TM, MERGE_TN), lambda j, i: (i, gablk + j)),
        ],
        out_specs=pl.BlockSpec((MERGE_TM, MERGE_TN), lambda j, i: (i, j)),
        scratch_shapes=[
            pltpu.VMEM((SSM_D_INNER, MERGE_TN), BF16),
            pltpu.VMEM((ATT_V, MERGE_TN), BF16),
        ],
        compiler_params=_cparams(("parallel", "arbitrary")),
        name="merge",
    )(y_ssm, w_ssm, y_att, w_att, proj, proj)


OUTPROJ_TM = 512


def _outproj_kernel(x_ref, m_ref, w_ref, nw_ref, x1_ref, h2_ref, wbf_ref):
    @pl.when(pl.program_id(0) == 0)
    def _():
        wbf_ref[...] = w_ref[...].astype(BF16)

    x1 = x_ref[...] + jnp.dot(m_ref[...], wbf_ref[...], preferred_element_type=F32)
    x1_ref[...] = x1
    h2_ref[...] = _rms_rows(x1, nw_ref[...], NORM_EPS).astype(h2_ref.dtype)


def _outproj(x2d, merged, w_out, norm_w):
    m = x2d.shape[0]
    return pl.pallas_call(
        _outproj_kernel,
        out_shape=(jax.ShapeDtypeStruct((m, D_MODEL), F32), jax.ShapeDtypeStruct((m, D_MODEL), BF16)),
        grid=(m // OUTPROJ_TM,),
        in_specs=[
            pl.BlockSpec((OUTPROJ_TM, D_MODEL), lambda i: (i, 0)),
            pl.BlockSpec((OUTPROJ_TM, D_MODEL), lambda i: (i, 0)),
            pl.BlockSpec((D_MODEL, D_MODEL), lambda i: (0, 0), pipeline_mode=pl.Buffered(1)),
            pl.BlockSpec((1, D_MODEL), lambda i: (0, 0)),
        ],
        out_specs=(
            pl.BlockSpec((OUTPROJ_TM, D_MODEL), lambda i: (i, 0)),
            pl.BlockSpec((OUTPROJ_TM, D_MODEL), lambda i: (i, 0)),
        ),
        scratch_shapes=[pltpu.VMEM((D_MODEL, D_MODEL), BF16)],
        compiler_params=_cparams(("arbitrary",)),
        name="outproj",
    )(x2d, merged, w_out, norm_w)


FFN_TM = 1024
FFN_TF = 512
DOWN_TM = 512
DOWN_TN = 512


def _ffn_up_kernel(h_ref, wg_ref, wu_ref, a_ref):
    h = h_ref[...]
    g = jnp.dot(h, wg_ref[...].astype(BF16), preferred_element_type=F32)
    u = jnp.dot(h, wu_ref[...].astype(BF16), preferred_element_type=F32)
    a_ref[...] = (g * jax.nn.sigmoid(g) * u).astype(a_ref.dtype)


def _ffn_up(h2, w_gate, w_up):
    m = h2.shape[0]
    return pl.pallas_call(
        _ffn_up_kernel,
        out_shape=jax.ShapeDtypeStruct((m, FFN_HIDDEN), BF16),
        grid=(m // FFN_TM, FFN_HIDDEN // FFN_TF),
        in_specs=[
            pl.BlockSpec((FFN_TM, D_MODEL), lambda i, j: (i, 0)),
            pl.BlockSpec((D_MODEL, FFN_TF), lambda i, j: (0, j)),
            pl.BlockSpec((D_MODEL, FFN_TF), lambda i, j: (0, j)),
        ],
        out_specs=pl.BlockSpec((FFN_TM, FFN_TF), lambda i, j: (i, j)),
        compiler_params=_cparams(("parallel", "parallel")),
        name="ffn_up",
    )(h2, w_gate, w_up)


def _ffn_down_kernel(a_ref, w_ref, x1_ref, o_ref, wbf_ref):
    @pl.when(pl.program_id(1) == 0)
    def _():
        wbf_ref[...] = w_ref[...].astype(BF16)

    o_ref[...] = x1_ref[...] + jnp.dot(a_ref[...], wbf_ref[...], preferred_element_type=F32)


def _ffn_down(a, w_down, x1):
    m = a.shape[0]
    return pl.pallas_call(
        _ffn_down_kernel,
        out_shape=jax.ShapeDtypeStruct((m, D_MODEL), F32),
        grid=(D_MODEL // DOWN_TN, m // DOWN_TM),
        in_specs=[
            pl.BlockSpec((DOWN_TM, FFN_HIDDEN), lambda j, i: (i, 0)),
            pl.BlockSpec((FFN_HIDDEN, DOWN_TN), lambda j, i: (0, j)),
            pl.BlockSpec((DOWN_TM, DOWN_TN), lambda j, i: (i, j)),
        ],
        out_specs=pl.BlockSpec((DOWN_TM, DOWN_TN), lambda j, i: (i, j)),
        scratch_shapes=[pltpu.VMEM((FFN_HIDDEN, DOWN_TN), BF16)],
        compiler_params=_cparams(("parallel", "arbitrary")),
        name="ffn_down",
    )(a, w_down, x1)


def _rope_tables(seq):
    half = ATT_HEAD_DIM // 2
    inv_freq = ROPE_THETA ** (-jnp.arange(half, dtype=F32) / half)
    ang = jnp.arange(seq, dtype=jnp.int32).astype(F32)[:, None] * inv_freq[None, :]
    cos = jnp.cos(ang)
    sin = jnp.sin(ang)
    reps = ATT_PAIR // ATT_HEAD_DIM
    cos_t = jnp.tile(jnp.concatenate([cos, cos], axis=-1), (1, reps))
    sin_t = jnp.tile(jnp.concatenate([-sin, sin], axis=-1), (1, reps))
    return cos_t, sin_t


def _layer(x, mix_norm_w, w_in, conv_w, conv_b, dt_bias, a_log, d_skip, ssm_norm_w, w_ssm_proj,
           q_norm_w, k_norm_w, lambda_q1, lambda_k1, lambda_q2, lambda_k2, subln_w, w_attn_proj,
           w_out, ffn_norm_w, w_ffn_gate, w_ffn_up, w_ffn_down):
    b, s, d = x.shape
    m = b * s
    x2d = x.reshape(m, d)

    proj, dt_raw, dt_grouped = _inproj(x2d, mix_norm_w.reshape(1, d), w_in.T)
    proj3 = proj.reshape(b, s, PROJ_COLS)

    xact = _xconv(proj3, conv_w, conv_b.reshape(1, SSM_CONV_CH))
    dt_col = dt_grouped
    dt_row = jnp.transpose(dt_raw[:, :SSM_HEADS].reshape(b, s, SSM_GROUPS, HPG), (0, 2, 3, 1))
    per_head_c = lambda v: v.astype(F32).reshape(SSM_GROUPS, 1, HPG)
    per_head_r = lambda v: v.astype(F32).reshape(SSM_GROUPS, HPG, 1)
    dskip_row = jnp.repeat(d_skip.astype(F32), SSM_HEAD_DIM).reshape(1, SSM_D_INNER)
    y_ssm = _ssd(xact, proj3, dt_col, dt_row,
                 per_head_c(dt_bias), per_head_c(a_log), per_head_r(dt_bias), per_head_r(a_log),
                 dskip_row, ssm_norm_w.reshape(1, SSM_D_INNER))

    cos_t, sin_t = _rope_tables(s)
    pair_w = lambda v: jnp.tile(v.astype(F32), ATT_PAIR // ATT_HEAD_DIM).reshape(1, ATT_PAIR)
    lam_w = lambda v: v.astype(F32).reshape(1, ATT_HEAD_DIM)
    y_att = _attn(proj3, cos_t, sin_t, pair_w(q_norm_w), pair_w(k_norm_w),
                  lam_w(lambda_q1), lam_w(lambda_k1), lam_w(lambda_q2), lam_w(lambda_k2),
                  subln_w.astype(F32).reshape(1, ATT_PAIR))

    merged = _merge(y_ssm.reshape(m, SSM_D_INNER), w_ssm_proj, y_att.reshape(m, ATT_V), w_attn_proj, proj)
    x1, h2 = _outproj(x2d, merged, w_out, ffn_norm_w.reshape(1, d))
    a = _ffn_up(h2, w_ffn_gate, w_ffn_up)
    out = _ffn_down(a, w_ffn_down, x1)
    return out.reshape(b, s, d)


def kernel(x, mix_norm_w, w_in, conv_w, conv_b, dt_bias, a_log, d_skip, ssm_norm_w, w_ssm_proj, q_norm_w, k_norm_w, lambda_q1, lambda_k1, lambda_q2, lambda_k2, subln_w, w_attn_proj, w_out, ffn_norm_w, w_ffn_gate, w_ffn_up, w_ffn_down):
    depth = w_in.shape[0]
    assert depth == 1, "LAM_INIT is derived for a single layer"
    layer = lambda v: v[0]
    return _layer(x, *(layer(v) for v in (
        mix_norm_w, w_in, conv_w, conv_b, dt_bias, a_log, d_skip, ssm_norm_w, w_ssm_proj, q_norm_w, k_norm_w,
        lambda_q1, lambda_k1, lambda_q2, lambda_k2, subln_w, w_attn_proj, w_out, ffn_norm_w, w_ffn_gate,
        w_ffn_up, w_ffn_down)))
```
